```python
import jax
import jax.numpy as jnp
from jax import lax
import numpy as np

D_MODEL = 1024
BATCH = 2
SEQ = 8192
DEPTH = 2

GRID_W = 64
CTX_LEN = 256
HEAD_DIM = 64
A_Q_HEADS = 12
A_KV_HEADS = 4
B_GROUPS = 4
B_GROUP_DIM = 64
C_CHANNELS = 512
C_KERNEL = 31
D_HEADS = 8
NA_WIN_ROWS = 8
NA_WIN_COLS = 16
D_FF = 2816
ROPE_THETA = 10000.0
Q_BLOCK = 128
NORM_EPS = 1e-6
N_MOD = 9
FFN_RES_WEIGHT = 0.5

A_Q_W = A_Q_HEADS * HEAD_DIM
A_KV_W = A_KV_HEADS * HEAD_DIM
B_W = B_GROUPS * B_GROUP_DIM
AB_IN = A_Q_W + 2 * A_KV_W + B_W
AB_OUT = A_Q_W + B_W
D_W = D_HEADS * HEAD_DIM
CD_IN = 2 * C_CHANNELS + 3 * D_W
CD_OUT = C_CHANNELS + D_W

kernel_name = 'hybrid_diffusion_gqa_fourier_conformer_natten'


def _rmsnorm(x):
    xf = x.astype(jnp.float32)
    return (xf * lax.rsqrt(jnp.mean(xf * xf, axis=-1, keepdims=True) + NORM_EPS)).astype(x.dtype)


def _layernorm(x, w, b):
    xf = x.astype(jnp.float32)
    mu = jnp.mean(xf, axis=-1, keepdims=True)
    var = jnp.mean(jnp.square(xf - mu), axis=-1, keepdims=True)
    return ((xf - mu) * lax.rsqrt(var + NORM_EPS)).astype(x.dtype) * w + b


def _modulate(x, shift, scale):
    return _rmsnorm(x) * (1 + scale) + shift


def _swiglu(h, w_gate, w_up, w_down):
    return (jax.nn.silu(h @ w_gate) * (h @ w_up)) @ w_down


def _macaron_half(x, shift, scale, gate, w_gate, w_up, w_down):
    return x + FFN_RES_WEIGHT * gate * _swiglu(_modulate(x, shift, scale), w_gate, w_up, w_down)


def _rope_2d(x, pos_row, pos_col):
    half = x.shape[-1] // 2
    n_ax = half // 2
    inv = ROPE_THETA ** (-jnp.arange(n_ax, dtype=jnp.float32) / n_ax)
    ang = jnp.concatenate([pos_row[:, None] * inv, pos_col[:, None] * inv], axis=-1)
    cos = jnp.cos(ang)[None, :, None, :].astype(x.dtype)
    sin = jnp.sin(ang)[None, :, None, :].astype(x.dtype)
    x1, x2 = x[..., :half], x[..., half:]
    return jnp.concatenate([x1 * cos - x2 * sin, x2 * cos + x1 * sin], axis=-1)


def _attend(q, k, v):
    bsz, lq, heads, hd = q.shape
    kvh = k.shape[2]
    qg = q.reshape(bsz, lq, kvh, heads // kvh, hd)
    s = jnp.einsum('bqkgd,blkd->bkgql', qg, k).astype(jnp.float32) * (hd ** -0.5)
    p = jax.nn.softmax(s, axis=-1).astype(v.dtype)
    o = jnp.einsum('bkgql,blkd->bqkgd', p, v)
    return o.reshape(bsz, lq, heads * hd)


def _gqa_blocks(q, k_all, v_all):
    bsz, seq, heads, hd = q.shape
    qb = jnp.moveaxis(q.reshape(bsz, seq // Q_BLOCK, Q_BLOCK, heads, hd), 1, 0)
    out = lax.map(lambda qblk: _attend(qblk, k_all, v_all), qb)
    return jnp.moveaxis(out, 0, 1).reshape(bsz, seq, heads * hd)


def _fourier(h):
    bsz, length, _ = h.shape
    hg = h.reshape(bsz, length, B_GROUPS, B_GROUP_DIM).astype(jnp.float32)
    f = jnp.fft.fft2(hg, axes=(1, 3), norm='ortho').real
    return f.reshape(bsz, length, B_W).astype(h.dtype)


def _conv_module(a, g, dw_w, dw_b, ln_w, ln_b):
    u = a * jax.nn.sigmoid(g)
    y = lax.conv_general_dilated(u, dw_w[:, None, :], window_strides=(1,),
                                 padding=[(C_KERNEL // 2, C_KERNEL // 2)],
                                 dimension_numbers=('NWC', 'WIO', 'NWC'),
                                 feature_group_count=u.shape[-1]) + dw_b
    return jax.nn.silu(_layernorm(y, ln_w, ln_b))


def _neighbourhood_attention(q, k, v, kc, vc, rpb):
    bsz, seq, heads, hd = q.shape
    rows = seq // GRID_W
    wr = min(NA_WIN_ROWS, rows)
    wc = NA_WIN_COLS
    qg = q.reshape(bsz, rows, GRID_W, heads, hd)
    kg = k.reshape(bsz, rows, GRID_W, heads, hd)
    vg = v.reshape(bsz, rows, GRID_W, heads, hd)
    col_q = jnp.arange(GRID_W)
    col_idx = jnp.clip(col_q - wc // 2, 0, GRID_W - wc)[:, None] + jnp.arange(wc)[None, :]
    col_off = col_idx - col_q[:, None] + (NA_WIN_COLS - 1)
    scale = hd ** -0.5

    def row_block(r):
        start = jnp.clip(r - wr // 2, 0, rows - wr)
        qr = lax.dynamic_index_in_dim(qg, r, axis=1, keepdims=False)
        kn = lax.dynamic_slice_in_dim(kg, start, wr, axis=1)[:, :, col_idx]
        vn = lax.dynamic_slice_in_dim(vg, start, wr, axis=1)[:, :, col_idx]
        row_off = start + jnp.arange(wr) - r + (NA_WIN_ROWS - 1)
        bias = rpb[:, row_off[None, :, None], col_off[:, None, :]]
        s_nb = (jnp.einsum('bqhd,brqjhd->bhqrj', qr, kn).astype(jnp.float32) * scale
                + bias[None].astype(jnp.float32)).reshape(bsz, heads, GRID_W, wr * wc)
        s_cx = jnp.einsum('bqhd,blhd->bhql', qr, kc).astype(jnp.float32) * scale
        p = jax.nn.softmax(jnp.concatenate([s_nb, s_cx], axis=-1), axis=-1).astype(v.dtype)
        p_nb = p[..., :wr * wc].reshape(bsz, heads, GRID_W, wr, wc)
        p_cx = p[..., wr * wc:]
        return (jnp.einsum('bhqrj,brqjhd->bqhd', p_nb, vn)
                + jnp.einsum('bhql,blhd->bqhd', p_cx, vc))

    out = lax.map(row_block, jnp.arange(rows))
    return jnp.moveaxis(out, 0, 1).reshape(bsz, seq, heads * hd)


def _mixer_ab(hl, hc, w_in, w_out, q_norm, k_norm, with_ctx):
    bsz, seq, _ = hl.shape
    lc = hc.shape[1]
    t = jnp.arange(seq)
    row = (t // GRID_W).astype(jnp.float32)
    col = (t % GRID_W).astype(jnp.float32)
    cuts = [A_Q_W, A_Q_W + A_KV_W, A_Q_W + 2 * A_KV_W]
    q, k, v, f = jnp.split(hl @ w_in, cuts, axis=-1)
    q = _rope_2d(_rmsnorm(q.reshape(bsz, seq, A_Q_HEADS, HEAD_DIM)) * q_norm, row, col)
    k = _rope_2d(_rmsnorm(k.reshape(bsz, seq, A_KV_HEADS, HEAD_DIM)) * k_norm, row, col)
    v = v.reshape(bsz, seq, A_KV_HEADS, HEAD_DIM)
    if with_ctx:
        qc, kc, vc, fc = jnp.split(hc @ w_in, cuts, axis=-1)
    else:
        kc, vc = jnp.split(hc @ w_in[:, A_Q_W:A_Q_W + 2 * A_KV_W], [A_KV_W], axis=-1)
    kc = _rmsnorm(kc.reshape(bsz, lc, A_KV_HEADS, HEAD_DIM)) * k_norm
    vc = vc.reshape(bsz, lc, A_KV_HEADS, HEAD_DIM)
    k_all = jnp.concatenate([kc, k], axis=1)
    v_all = jnp.concatenate([vc, v], axis=1)
    yl = jnp.concatenate([_gqa_blocks(q, k_all, v_all), _fourier(f)], axis=-1) @ w_out
    if not with_ctx:
        return yl, None
    qc = _rmsnorm(qc.reshape(bsz, lc, A_Q_HEADS, HEAD_DIM)) * q_norm
    yc = jnp.concatenate([_attend(qc, kc, vc), _fourier(fc)], axis=-1) @ w_out
    return yl, yc


def _mixer_cd(hl, hc, w_in, w_out, dw_w, dw_b, ln_w, ln_b, rpb, with_ctx):
    bsz, seq, _ = hl.shape
    lc = hc.shape[1]
    cuts = [C_CHANNELS, 2 * C_CHANNELS, 2 * C_CHANNELS + D_W, 2 * C_CHANNELS + 2 * D_W]
    ga, gb, q, k, v = jnp.split(hl @ w_in, cuts, axis=-1)
    if with_ctx:
        gac, gbc, qc, kc, vc = jnp.split(hc @ w_in, cuts, axis=-1)
    else:
        kc, vc = jnp.split(hc @ w_in[:, 2 * C_CHANNELS + D_W:], [D_W], axis=-1)
    kc = kc.reshape(bsz, lc, D_HEADS, HEAD_DIM)
    vc = vc.reshape(bsz, lc, D_HEADS, HEAD_DIM)
    y_conv = _conv_module(ga, gb, dw_w, dw_b, ln_w, ln_b)
    y_na = _neighbourhood_attention(q.reshape(bsz, seq, D_HEADS, HEAD_DIM),
                                    k.reshape(bsz, seq, D_HEADS, HEAD_DIM),
                                    v.reshape(bsz, seq, D_HEADS, HEAD_DIM), kc, vc, rpb)
    yl = jnp.concatenate([y_conv, y_na], axis=-1) @ w_out
    if not with_ctx:
        return yl, None
    yc_conv = _conv_module(gac, gbc, dw_w, dw_b, ln_w, ln_b)
    yc_att = _attend(qc.reshape(bsz, lc, D_HEADS, HEAD_DIM), kc, vc)
    yc = jnp.concatenate([yc_conv, yc_att], axis=-1) @ w_out
    return yl, yc


def setup_inputs(seed: int = 0) -> dict:
    key = jax.random.key(seed)
    ks = jax.random.split(key, 21)
    n_even = (DEPTH + 1) // 2
    n_odd = DEPTH // 2

    def nrm(k, shape, scale):
        return jax.random.normal(k, shape, jnp.float32) * scale

    return {
        'x': nrm(ks[0], (BATCH, SEQ, D_MODEL), 1.0),
        'c': nrm(ks[1], (BATCH, D_MODEL), 1.0),
        'ctx': nrm(ks[2], (BATCH, CTX_LEN, D_MODEL), 1.0),
        'c_ctx': nrm(ks[3], (D_MODEL,), 1.0),
        'w_mod': nrm(ks[4], (DEPTH, D_MODEL, N_MOD * D_MODEL), 0.5 * D_MODEL ** -0.5),
        'b_mod': nrm(ks[5], (DEPTH, N_MOD * D_MODEL), 0.02),
        'ffn_w_gate': nrm(ks[6], (DEPTH, 2, D_MODEL, D_FF), D_MODEL ** -0.5),
        'ffn_w_up': nrm(ks[7], (DEPTH, 2, D_MODEL, D_FF), D_MODEL ** -0.5),
        'ffn_w_down': nrm(ks[8], (DEPTH, 2, D_FF, D_MODEL), D_FF ** -0.5),
        'ab_w_in': nrm(ks[9], (n_even, D_MODEL, AB_IN), D_MODEL ** -0.5),
        'ab_w_out': nrm(ks[10], (n_even, AB_OUT, D_MODEL), AB_OUT ** -0.5),
        'ab_q_norm': 1.0 + nrm(ks[11], (n_even, HEAD_DIM), 0.05),
        'ab_k_norm': 1.0 + nrm(ks[12], (n_even, HEAD_DIM), 0.05),
        'cd_w_in': nrm(ks[13], (n_odd, D_MODEL, CD_IN), D_MODEL ** -0.5),
        'cd_w_out': nrm(ks[14], (n_odd, CD_OUT, D_MODEL), CD_OUT ** -0.5),
        'cd_dw_w': nrm(ks[15], (n_odd, C_KERNEL, C_CHANNELS), C_KERNEL ** -0.5),
        'cd_dw_b': nrm(ks[16], (n_odd, C_CHANNELS), 0.02),
        'cd_ln_w': 1.0 + nrm(ks[17], (n_odd, C_CHANNELS), 0.05),
        'cd_ln_b': nrm(ks[18], (n_odd, C_CHANNELS), 0.02),
        'cd_rpb': nrm(ks[19], (n_odd, D_HEADS, 2 * NA_WIN_ROWS - 1, 2 * NA_WIN_COLS - 1), 0.1),
        'final_norm': 1.0 + nrm(ks[20], (D_MODEL,), 0.05),
    }


def reference(x, c, ctx, c_ctx, w_mod, b_mod, ffn_w_gate, ffn_w_up, ffn_w_down,
              ab_w_in, ab_w_out, ab_q_norm, ab_k_norm,
              cd_w_in, cd_w_out, cd_dw_w, cd_dw_b, cd_ln_w, cd_ln_b, cd_rpb, final_norm):
    bsz = x.shape[0]
    xl, xc = x, ctx
    for layer in range(DEPTH):
        last = layer == DEPTH - 1
        ml = (jax.nn.silu(c) @ w_mod[layer] + b_mod[layer]).reshape(bsz, N_MOD, 1, D_MODEL)
        mc = (jax.nn.silu(c_ctx) @ w_mod[layer] + b_mod[layer]).reshape(N_MOD, D_MODEL)
        xl = _macaron_half(xl, ml[:, 0], ml[:, 1], ml[:, 2],
                           ffn_w_gate[layer, 0], ffn_w_up[layer, 0], ffn_w_down[layer, 0])
        xc = _macaron_half(xc, mc[0], mc[1], mc[2],
                           ffn_w_gate[layer, 0], ffn_w_up[layer, 0], ffn_w_down[layer, 0])
        hl = _modulate(xl, ml[:, 3], ml[:, 4])
        hc = _modulate(xc, mc[3], mc[4])
        i = layer // 2
        if layer % 2 == 0:
            yl, yc = _mixer_ab(hl, hc, ab_w_in[i], ab_w_out[i], ab_q_norm[i], ab_k_norm[i], not last)
        else:
            yl, yc = _mixer_cd(hl, hc, cd_w_in[i], cd_w_out[i], cd_dw_w[i], cd_dw_b[i],
                               cd_ln_w[i], cd_ln_b[i], cd_rpb[i], not last)
        xl = xl + ml[:, 5] * yl
        xl = _macaron_half(xl, ml[:, 6], ml[:, 7], ml[:, 8],
                           ffn_w_gate[layer, 1], ffn_w_up[layer, 1], ffn_w_down[layer, 1])
        if not last:
            xc = xc + mc[5] * yc
            xc = _macaron_half(xc, mc[6], mc[7], mc[8],
                               ffn_w_gate[layer, 1], ffn_w_up[layer, 1], ffn_w_down[layer, 1])
    return _rmsnorm(xl) * final_norm
```

```python
import functools
import math

import numpy as np
import jax
import jax.numpy as jnp
from jax import lax
from jax.experimental import pallas as pl
from jax.experimental.pallas import tpu as pltpu

F32 = jnp.float32
BF16 = jnp.bfloat16

GRID_W = 64
HEAD_DIM = 64
A_Q_HEADS = 12
A_KV_HEADS = 4
B_GROUPS = 4
B_GROUP_DIM = 64
C_CHANNELS = 512
C_KERNEL = 31
D_HEADS = 8
NA_WIN_ROWS = 8
NA_WIN_COLS = 16
ROPE_THETA = 10000.0
NORM_EPS = 1e-6
N_MOD = 9
FFN_RES_WEIGHT = 0.5

A_Q_W = A_Q_HEADS * HEAD_DIM
A_KV_W = A_KV_HEADS * HEAD_DIM
B_W = B_GROUPS * B_GROUP_DIM
D_W = D_HEADS * HEAD_DIM

LANES = 128
SUBLANES = 8
MXU_DIM = 256

TM = 512
FFN_CHUNK = 256
ATT_TQ = 512
ATT_TK = 512
NA_QROWS = 4
CONV_HALO = 16
NEG_BIG = -1e30

_MiB = 1 << 20


def _cparams(sem, vmem_mib):
    return pltpu.CompilerParams(dimension_semantics=sem, vmem_limit_bytes=vmem_mib * _MiB)


def _dot(a, b):
    return jnp.dot(a, b, preferred_element_type=F32)


def _dot_nt(a, b):
    return lax.dot_general(a, b, (((1,), (1,)), ((), ())), preferred_element_type=F32)


def _sigmoid(x):
    return 1.0 / (1.0 + jnp.exp(-x))


def _cos_sin(n):
    idx = np.arange(n)
    ang = 2.0 * np.pi * (np.outer(idx, idx) % n) / n
    return np.cos(ang), np.sin(ang)


def _const_bf16(a):
    return jnp.asarray(np.asarray(a, np.float32)).astype(BF16)


def _modulate(x, shift, scale):
    ms = jnp.mean(x * x, axis=-1, keepdims=True)
    return (x * lax.rsqrt(ms + NORM_EPS)) * (1.0 + scale) + shift


def _mod_kernel(ct_ref, w_ref, b_ref, o_ref, *, n_rows):
    ct = ct_ref[...]
    a = ct * _sigmoid(ct)
    w = w_ref[...]
    rows = [jnp.sum(w * a[:, r:r + 1], axis=0, keepdims=True) for r in range(n_rows)]
    rows.append(jnp.zeros((SUBLANES - n_rows, w.shape[1]), F32))
    o_ref[...] = jnp.concatenate(rows, axis=0) + b_ref[...]


def _mod_params(c, c_ctx, w_mod, b_mod):
    depth, d, nmd = w_mod.shape
    bsz = c.shape[0]
    n_rows = bsz + 1
    assert n_rows <= SUBLANES
    ct = jnp.concatenate([c, c_ctx[None], jnp.zeros((SUBLANES - n_rows, d), F32)], axis=0).T
    tn = d
    out = pl.pallas_call(
        functools.partial(_mod_kernel, n_rows=n_rows),
        grid=(depth, nmd // tn),
        in_specs=[
            pl.BlockSpec((d, SUBLANES), lambda l, n: (0, 0)),
            pl.BlockSpec((None, d, tn), lambda l, n: (l, 0, n)),
            pl.BlockSpec((None, 1, tn), lambda l, n: (l, 0, n)),
        ],
        out_specs=pl.BlockSpec((None, SUBLANES, tn), lambda l, n: (l, 0, n)),
        out_shape=jax.ShapeDtypeStruct((depth, SUBLANES, nmd), F32),
        compiler_params=_cparams(("arbitrary", "arbitrary"), 32),
        name="adaln_params",
    )(ct, w_mod, b_mod.reshape(depth, 1, nmd))
    return out.reshape(depth, SUBLANES, N_MOD, d)


def _ffn_kernel(x_ref, mod_ref, wg_ref, wu_ref, wd_ref, *rest, base, final, n_chunks):
    if final:
        fn_ref, o_ref, acc_ref = rest
    else:
        o_ref, acc_ref = rest
    x = x_ref[...]
    m = mod_ref[...]
    h = _modulate(x, m[base:base + 1], m[base + 1:base + 2]).astype(BF16)
    for c in range(n_chunks):
        sl = slice(c * FFN_CHUNK, (c + 1) * FFN_CHUNK)
        g = _dot(h, wg_ref[:, sl])
        u = _dot(h, wu_ref[:, sl])
        a = ((g * _sigmoid(g)) * u).astype(BF16)
        part = _dot(a, wd_ref[sl, :])
        if c == 0:
            acc_ref[...] = part
        else:
            acc_ref[...] += part
    y = x + (FFN_RES_WEIGHT * m[base + 2:base + 3]) * acc_ref[...]
    if final:
        ms = jnp.mean(y * y, axis=-1, keepdims=True)
        y = (y * lax.rsqrt(ms + NORM_EPS)) * fn_ref[...]
    o_ref[...] = y


def _ffn(x, mod_l, wg, wu, wd, *, base, n_tiles, mod_row, final_norm=None):
    ntok, d = x.shape
    dff = wg.shape[1]
    assert dff % FFN_CHUNK == 0
    final = final_norm is not None
    resident = dict(pipeline_mode=pl.Buffered(1))
    in_specs = [
        pl.BlockSpec((TM, d), lambda i: (i, 0)),
        pl.BlockSpec((None, N_MOD, d), lambda i: (mod_row(i), 0, 0)),
        pl.BlockSpec((d, dff), lambda i: (0, 0), **resident),
        pl.BlockSpec((d, dff), lambda i: (0, 0), **resident),
        pl.BlockSpec((dff, d), lambda i: (0, 0), **resident),
    ]
    args = [x, mod_l, wg, wu, wd]
    if final:
        in_specs.append(pl.BlockSpec((1, d), lambda i: (0, 0)))
        args.append(final_norm.reshape(1, d))
    return pl.pallas_call(
        functools.partial(_ffn_kernel, base=base, final=final, n_chunks=dff // FFN_CHUNK),
        grid=(n_tiles,),
        in_specs=in_specs,
        out_specs=pl.BlockSpec((TM, d), lambda i: (i, 0)),
        out_shape=jax.ShapeDtypeStruct((n_tiles * TM, d), F32),
        scratch_shapes=[pltpu.VMEM((TM, d), F32)],
        compiler_params=_cparams(("arbitrary",), 48),
        name="ffn_final" if final else "ffn",
    )(*args)


def _outproj_kernel(x_ref, mod_ref, a1_ref, a2_ref, w1_ref, w2_ref, o_ref):
    y = _dot(a1_ref[...].astype(BF16), w1_ref[...]) + _dot(a2_ref[...].astype(BF16), w2_ref[...])
    o_ref[...] = x_ref[...] + mod_ref[...][5:6] * y


def _outproj(x, mod_l, a1, a2, w_out, *, n_tiles, mod_row):
    d = x.shape[1]
    d1, d2 = a1.shape[1], a2.shape[1]
    w1, w2 = w_out[:d1], w_out[d1:]
    return pl.pallas_call(
        _outproj_kernel,
        grid=(n_tiles,),
        in_specs=[
            pl.BlockSpec((TM, d), lambda i: (i, 0)),
            pl.BlockSpec((None, N_MOD, d), lambda i: (mod_row(i), 0, 0)),
            pl.BlockSpec((TM, d1), lambda i: (i, 0)),
            pl.BlockSpec((TM, d2), lambda i: (i, 0)),
            pl.BlockSpec((d1, d), lambda i: (0, 0)),
            pl.BlockSpec((d2, d), lambda i: (0, 0)),
        ],
        out_specs=pl.BlockSpec((TM, d), lambda i: (i, 0)),
        out_shape=jax.ShapeDtypeStruct((n_tiles * TM, d), F32),
        compiler_params=_cparams(("arbitrary",), 32),
        name="out_proj",
    )(x, mod_l, a1, a2, w1, w2)


def _swap_halves(z, low_half):
    n = z.shape[1]
    return jnp.where(low_half, pltpu.roll(z, n - HEAD_DIM // 2, 1), pltpu.roll(z, HEAD_DIM // 2, 1))


def _head_sumsq(z, ones_bd):
    zz = z * z
    hi = zz.astype(BF16)
    lo = (zz - hi.astype(F32)).astype(BF16)
    return _dot(hi, ones_bd) + _dot(lo, ones_bd)


def _inproj_ab_kernel(x_ref, mod_ref, w_ref, qn_ref, kn_ref, cos_ref, sin_ref, ones_ref, dft_ref,
                      q_ref, k_ref, v_ref, y_ref):
    x = x_ref[...]
    m = mod_ref[...]
    h = _modulate(x, m[3:4], m[4:5]).astype(BF16)
    y = _dot(h, w_ref[...])
    ones_bd = ones_ref[...]
    cos = jnp.concatenate([cos_ref[...]] * 2, axis=1)
    sin = jnp.concatenate([sin_ref[...]] * 2, axis=1)
    lane = lax.broadcasted_iota(jnp.int32, (1, MXU_DIM), 1)
    low_half = (lane % HEAD_DIM) < (HEAD_DIM // 2)

    def norm_rope(z, nw):
        ss = _head_sumsq(z, ones_bd)
        zn = (z * lax.rsqrt(ss * (1.0 / HEAD_DIM) + NORM_EPS)) * nw
        return zn * cos + _swap_halves(zn, low_half) * sin

    qn = qn_ref[...]
    kn = kn_ref[...]
    for blk in range(A_Q_W // MXU_DIM):
        sl = slice(blk * MXU_DIM, (blk + 1) * MXU_DIM)
        q_ref[:, sl] = (norm_rope(y[:, sl], qn) * (HEAD_DIM ** -0.5)).astype(BF16)
    k_ref[...] = norm_rope(y[:, A_Q_W:A_Q_W + A_KV_W], kn).astype(BF16)
    v_ref[...] = y[:, A_Q_W + A_KV_W:A_Q_W + 2 * A_KV_W].astype(BF16)
    f = y[:, A_Q_W + 2 * A_KV_W:].astype(BF16)
    y_ref[...] = _dot(f, dft_ref[...])


def _inproj_ab(x, mod_l, w_in, q_norm, k_norm, cos_t, sin_t, *, n_tiles, mod_row, rope_row):
    ntok, d = x.shape
    n_in = w_in.shape[1]
    ones_bd = _const_bf16(np.kron(np.eye(MXU_DIM // HEAD_DIM), np.ones((HEAD_DIM, HEAD_DIM))))
    gc, gs = _cos_sin(B_GROUP_DIM)
    eye = np.eye(B_GROUPS)
    dft = _const_bf16(np.concatenate([np.kron(eye, gc), np.kron(eye, gs)], axis=1))
    qn = jnp.tile(q_norm, MXU_DIM // HEAD_DIM).reshape(1, MXU_DIM)
    kn = jnp.tile(k_norm, MXU_DIM // HEAD_DIM).reshape(1, MXU_DIM)
    const = lambda i: (0, 0)
    row = lambda i: (i, 0)
    return pl.pallas_call(
        _inproj_ab_kernel,
        grid=(n_tiles,),
        in_specs=[
            pl.BlockSpec((TM, d), row),
            pl.BlockSpec((None, N_MOD, d), lambda i: (mod_row(i), 0, 0)),
            pl.BlockSpec((d, n_in), const),
            pl.BlockSpec((1, MXU_DIM), const),
            pl.BlockSpec((1, MXU_DIM), const),
            pl.BlockSpec((TM, LANES), lambda i: (rope_row(i), 0)),
            pl.BlockSpec((TM, LANES), lambda i: (rope_row(i), 0)),
            pl.BlockSpec((MXU_DIM, MXU_DIM), const),
            pl.BlockSpec((B_W, 2 * B_W), const),
        ],
        out_specs=[
            pl.BlockSpec((TM, A_Q_W), row),
            pl.BlockSpec((TM, A_KV_W), row),
            pl.BlockSpec((TM, A_KV_W), row),
            pl.BlockSpec((TM, 2 * B_W), row),
        ],
        out_shape=[
            jax.ShapeDtypeStruct((ntok, A_Q_W), BF16),
            jax.ShapeDtypeStruct((ntok, A_KV_W), BF16),
            jax.ShapeDtypeStruct((ntok, A_KV_W), BF16),
            jax.ShapeDtypeStruct((ntok, 2 * B_W), F32),
        ],
        compiler_params=_cparams(("arbitrary",), 40),
        name="in_proj_ab",
    )(x, mod_l, w_in, qn, kn, cos_t, sin_t, ones_bd, dft)


def _rope_tables(seq):
    t = jnp.arange(seq)
    row = (t // GRID_W).astype(F32)
    col = (t % GRID_W).astype(F32)
    n_ax = HEAD_DIM // 4
    inv = ROPE_THETA ** (-jnp.arange(n_ax, dtype=F32) / n_ax)
    ang = jnp.concatenate([row[:, None] * inv, col[:, None] * inv], axis=-1)
    cos, sin = jnp.cos(ang), jnp.sin(ang)
    cos_h = jnp.concatenate([cos, cos], axis=-1)
    sin_h = jnp.concatenate([-sin, sin], axis=-1)
    cos_t = jnp.concatenate([jnp.tile(cos_h, (1, 2)), jnp.ones((TM, LANES), F32)], axis=0)
    sin_t = jnp.concatenate([jnp.tile(sin_h, (1, 2)), jnp.zeros((TM, LANES), F32)], axis=0)
    return cos_t, sin_t


def _gqa_kernel(q_ref, kc_ref, vc_ref, *rest, has_lat):
    if has_lat:
        kl_ref, vl_ref, o_ref, qm_ref, m_ref, l_ref, acc_ref = rest
    else:
        _, o_ref, qm_ref, m_ref, l_ref, acc_ref = rest
    j = pl.program_id(2)
    group = A_Q_HEADS // A_KV_HEADS
    lane = lax.broadcasted_iota(jnp.int32, (1, LANES), 1)
    hi_half = lane >= HEAD_DIM

    @pl.when(j == 0)
    def _first():
        for h in range(A_Q_HEADS):
            qp = q_ref[:, (h // 2) * LANES:(h // 2 + 1) * LANES].astype(F32)
            kv_half = (h // group) % 2
            if kv_half != h % 2:
                qp = pltpu.roll(qp, HEAD_DIM, 1)
            keep = hi_half if kv_half else jnp.logical_not(hi_half)
            qm_ref[h] = jnp.where(keep, qp, jnp.zeros_like(qp)).astype(BF16)
        for h in range(A_Q_HEADS):
            kvp = (h // group) // 2
            s = _dot_nt(qm_ref[h], kc_ref[:, kvp * LANES:(kvp + 1) * LANES])
            mx = jnp.max(s, axis=-1, keepdims=True)
            p = jnp.exp(s - mx)
            m_ref[h] = mx
            l_ref[h] = jnp.sum(p, axis=-1, keepdims=True)
            acc_ref[h] = _dot(p.astype(BF16), vc_ref[:, kvp * LANES:(kvp + 1) * LANES])

    if has_lat:
        @pl.when(j > 0)
        def _rest():
            for h in range(A_Q_HEADS):
                kvp = (h // group) // 2
                s = _dot_nt(qm_ref[h], kl_ref[:, kvp * LANES:(kvp + 1) * LANES])
                m_prev = m_ref[h]
                m_new = jnp.maximum(m_prev, jnp.max(s, axis=-1, keepdims=True))
                alpha = jnp.exp(m_prev - m_new)
                p = jnp.exp(s - m_new)
                m_ref[h] = m_new
                l_ref[h] = alpha * l_ref[h] + jnp.sum(p, axis=-1, keepdims=True)
                acc_ref[h] = alpha * acc_ref[h] + _dot(p.astype(BF16), vl_ref[:, kvp * LANES:(kvp + 1) * LANES])

    @pl.when(j == pl.num_programs(2) - 1)
    def _final():
        for pb in range(A_Q_HEADS // 2):
            halves = []
            for h in (2 * pb, 2 * pb + 1):
                o = acc_ref[h] / l_ref[h]
                if ((h // group) % 2) != (h % 2):
                    o = pltpu.roll(o, HEAD_DIM, 1)
                halves.append(o)
            o_ref[:, pb * LANES:(pb + 1) * LANES] = jnp.where(hi_half, halves[1], halves[0]).astype(BF16)


def _gqa_scratch(tq):
    return [
        pltpu.VMEM((A_Q_HEADS, tq, LANES), BF16),
        pltpu.VMEM((A_Q_HEADS, tq, 1), F32),
        pltpu.VMEM((A_Q_HEADS, tq, 1), F32),
        pltpu.VMEM((A_Q_HEADS, tq, LANES), F32),
    ]


def _gqa(q, k, v, *, bsz, seq, lc):
    ntok = q.shape[0]
    n_lat = bsz * seq
    tq, tk = ATT_TQ, ATT_TK
    ctx_blk = lambda b: n_lat // lc + b
    o = pl.pallas_call(
        functools.partial(_gqa_kernel, has_lat=True),
        grid=(bsz, seq // tq, 1 + seq // tk),
        in_specs=[
            pl.BlockSpec((tq, A_Q_W), lambda b, i, j: (b * (seq // tq) + i, 0)),
            pl.BlockSpec((lc, A_KV_W), lambda b, i, j: (ctx_blk(b), 0)),
            pl.BlockSpec((lc, A_KV_W), lambda b, i, j: (ctx_blk(b), 0)),
            pl.BlockSpec((tk, A_KV_W), lambda b, i, j: (b * (seq // tk) + jnp.maximum(j - 1, 0), 0)),
            pl.BlockSpec((tk, A_KV_W), lambda b, i, j: (b * (seq // tk) + jnp.maximum(j - 1, 0), 0)),
        ],
        out_specs=pl.BlockSpec((tq, A_Q_W), lambda b, i, j: (b * (seq // tq) + i, 0)),
        out_shape=jax.ShapeDtypeStruct((ntok, A_Q_W), BF16),
        scratch_shapes=_gqa_scratch(tq),
        compiler_params=_cparams(("arbitrary", "arbitrary", "arbitrary"), 48),
        name="gqa_latent",
    )(q, k, v, k, v)
    return pl.pallas_call(
        functools.partial(_gqa_kernel, has_lat=False),
        grid=(bsz, 1, 1),
        in_specs=[
            pl.BlockSpec((lc, A_Q_W), lambda b, i, j: (ctx_blk(b), 0)),
            pl.BlockSpec((lc, A_KV_W), lambda b, i, j: (ctx_blk(b), 0)),
            pl.BlockSpec((lc, A_KV_W), lambda b, i, j: (ctx_blk(b), 0)),
            pl.BlockSpec(memory_space=pl.ANY),
        ],
        out_specs=pl.BlockSpec((lc, A_Q_W), lambda b, i, j: (ctx_blk(b), 0)),
        out_shape=jax.ShapeDtypeStruct((ntok, A_Q_W), BF16),
        scratch_shapes=_gqa_scratch(lc),
        input_output_aliases={3: 0},
        compiler_params=_cparams(("arbitrary", "arbitrary", "arbitrary"), 32),
        name="gqa_context",
    )(q, k, v, o)


def _dft_a_kernel(y_ref, ca_ref, tc_ref, ts_ref, d_ref, *, nbb):
    ca = ca_ref[...]
    ra = ca.shape[1]
    for j in range(nbb):
        yj = y_ref[:, j, :].astype(BF16)
        p = _dot(ca, yj)
        br = p[:ra, :B_W] - p[ra:, B_W:]
        bi = -p[:ra, B_W:] - p[ra:, :B_W]
        tc = jnp.concatenate([tc_ref[j]] * (B_W // LANES), axis=1)
        ts = jnp.concatenate([ts_ref[j]] * (B_W // LANES), axis=1)
        d_ref[:, 0, j, :] = br * tc + bi * ts
        d_ref[:, 1, j, :] = bi * tc - br * ts


def _dft_b_kernel(d_ref, cb_ref, o_ref, *, kb, scale):
    cb = cb_ref[...]
    for j in range(kb):
        o_ref[:, j, :] = _dot(cb, d_ref[j].astype(BF16)) * scale


def _dft_ctx_kernel(y_ref, cs_ref, prev_ref, o_ref, *, scale):
    del prev_ref
    y = y_ref[...]
    cs = cs_ref[...]
    lc = cs.shape[0]
    yc = y[:, :B_W].astype(BF16)
    ys = y[:, B_W:].astype(BF16)
    o_ref[...] = (_dot(cs[:, :lc], yc) - _dot(cs[:, lc:], ys)) * scale


def _fourier(y, *, bsz, seq, lc):
    ntok = y.shape[0]
    ra = seq // GRID_W
    nb = GRID_W
    nbb = SUBLANES
    kb = SUBLANES
    assert ntok % nb == 0 and ra % SUBLANES == 0 and (ntok // nb) % 1 == 0
    ca_c, ca_s = _cos_sin(ra)
    ca = _const_bf16(np.concatenate([ca_c, ca_s], axis=0))
    th = 2.0 * np.pi * np.outer(np.arange(nb), np.arange(ra)) / seq
    tc = jnp.asarray(np.broadcast_to(np.cos(th)[:, :, None], (nb, ra, LANES)).astype(np.float32))
    ts = jnp.asarray(np.broadcast_to(np.sin(th)[:, :, None], (nb, ra, LANES)).astype(np.float32))
    cb_c, cb_s = _cos_sin(nb)
    cb = _const_bf16(np.concatenate([cb_c, cb_s], axis=1))

    y3 = y.reshape(ntok // nb, nb, 2 * B_W)
    d = pl.pallas_call(
        functools.partial(_dft_a_kernel, nbb=nbb),
        grid=(bsz, nb // nbb),
        in_specs=[
            pl.BlockSpec((ra, nbb, 2 * B_W), lambda b, i: (b, i, 0)),
            pl.BlockSpec((2 * ra, ra), lambda b, i: (0, 0)),
            pl.BlockSpec((nbb, ra, LANES), lambda b, i: (i, 0, 0)),
            pl.BlockSpec((nbb, ra, LANES), lambda b, i: (i, 0, 0)),
        ],
        out_specs=pl.BlockSpec((None, ra, 2, nbb, B_W), lambda b, i: (b, 0, 0, i, 0)),
        out_shape=jax.ShapeDtypeStruct((bsz, ra, 2, nb, B_W), F32),
        compiler_params=_cparams(("arbitrary", "arbitrary"), 32),
        name="fourier_rows",
    )(y3, ca, tc, ts)
    d = d.reshape(bsz, ra, 2 * nb, B_W)

    scale = 1.0 / math.sqrt(seq * B_GROUP_DIM)
    n_out = ntok // ra
    assert ntok % ra == 0
    z3 = pl.pallas_call(
        functools.partial(_dft_b_kernel, kb=kb, scale=scale),
        grid=(bsz, ra // kb),
        in_specs=[
            pl.BlockSpec((None, kb, 2 * nb, B_W), lambda b, i: (b, i, 0, 0)),
            pl.BlockSpec((nb, 2 * nb), lambda b, i: (0, 0)),
        ],
        out_specs=pl.BlockSpec((nb, kb, B_W), lambda b, i: (b, i, 0)),
        out_shape=jax.ShapeDtypeStruct((n_out, ra, B_W), F32),
        compiler_params=_cparams(("arbitrary", "arbitrary"), 32),
        name="fourier_cols",
    )(d, cb)
    z = z3.reshape(ntok, B_W)

    cc, cs_ = _cos_sin(lc)
    csm = _const_bf16(np.concatenate([cc, cs_], axis=1))
    ctx_blk = lambda b: (bsz * seq) // lc + b
    return pl.pallas_call(
        functools.partial(_dft_ctx_kernel, scale=1.0 / math.sqrt(lc * B_GROUP_DIM)),
        grid=(bsz,),
        in_specs=[
            pl.BlockSpec((lc, 2 * B_W), lambda b: (ctx_blk(b), 0)),
            pl.BlockSpec((lc, 2 * lc), lambda b: (0, 0)),
            pl.BlockSpec(memory_space=pl.ANY),
        ],
        out_specs=pl.BlockSpec((lc, B_W), lambda b: (ctx_blk(b), 0)),
        out_shape=jax.ShapeDtypeStruct((ntok, B_W), F32),
        input_output_aliases={2: 0},
        compiler_params=_cparams(("arbitrary",), 32),
        name="fourier_context",
    )(y, csm, z)


def _inproj_cd_kernel(x_ref, mod_ref, w_ref, u_ref, q_ref, k_ref, v_ref):
    x = x_ref[...]
    m = mod_ref[...]
    h = _modulate(x, m[3:4], m[4:5]).astype(BF16)
    y = _dot(h, w_ref[...])
    c = C_CHANNELS
    u_ref[...] = y[:, :c] * _sigmoid(y[:, c:2 * c])
    q_ref[...] = (y[:, 2 * c:2 * c + D_W] * (HEAD_DIM ** -0.5)).astype(BF16)
    k_ref[...] = y[:, 2 * c + D_W:2 * c + 2 * D_W].astype(BF16)
    v_ref[...] = y[:, 2 * c + 2 * D_W:].astype(BF16)


def _inproj_cd(x, mod_l, w_in, *, n_tiles, mod_row):
    ntok, d = x.shape
    n_in = w_in.shape[1]
    row = lambda i: (i, 0)
    return pl.pallas_call(
        _inproj_cd_kernel,
        grid=(n_tiles,),
        in_specs=[
            pl.BlockSpec((TM, d), row),
            pl.BlockSpec((None, N_MOD, d), lambda i: (mod_row(i), 0, 0)),
            pl.BlockSpec((d, n_in), lambda i: (0, 0)),
        ],
        out_specs=[
            pl.BlockSpec((TM, C_CHANNELS), row),
            pl.BlockSpec((TM, D_W), row),
            pl.BlockSpec((TM, D_W), row),
            pl.BlockSpec((TM, D_W), row),
        ],
        out_shape=[
            jax.ShapeDtypeStruct((ntok, C_CHANNELS), F32),
            jax.ShapeDtypeStruct((ntok, D_W), BF16),
            jax.ShapeDtypeStruct((ntok, D_W), BF16),
            jax.ShapeDtypeStruct((ntok, D_W), BF16),
        ],
        compiler_params=_cparams(("arbitrary",), 40),
        name="in_proj_cd",
    )(x, mod_l, w_in)


def _conv_kernel(prev_ref, cur_ref, next_ref, w_ref, b_ref, lnw_ref, lnb_ref, o_ref, buf_ref, *, tiles_per_seq):
    i = pl.program_id(1)
    halo = CONV_HALO
    tm = cur_ref.shape[0]
    zeros = jnp.zeros((halo, cur_ref.shape[1]), F32)
    buf_ref[0:halo, :] = jnp.where(i > 0, prev_ref[...], zeros)
    buf_ref[halo:halo + tm, :] = cur_ref[...]
    buf_ref[halo + tm:, :] = jnp.where(i < tiles_per_seq - 1, next_ref[...], zeros)
    w = w_ref[...]
    off = halo - C_KERNEL // 2
    acc = buf_ref[off:off + tm, :] * w[0:1]
    for t in range(1, C_KERNEL):
        acc = acc + buf_ref[off + t:off + t + tm, :] * w[t:t + 1]
    y = acc + b_ref[...]
    mu = jnp.mean(y, axis=-1, keepdims=True)
    yc = y - mu
    var = jnp.mean(yc * yc, axis=-1, keepdims=True)
    z = (yc * lax.rsqrt(var + NORM_EPS)) * lnw_ref[...] + lnb_ref[...]
    o_ref[...] = (z * _sigmoid(z)).astype(BF16)


def _conv_module(u, dw_w, dw_b, ln_w, ln_b, *, bsz, seq):
    c = u.shape[1]
    tiles = seq // TM
    hpt = TM // CONV_HALO
    n_halo_blocks = u.shape[0] // CONV_HALO
    cur = lambda b, i: (b * tiles + i, 0)
    prev = lambda b, i: (jnp.maximum((b * tiles + i) * hpt - 1, 0), 0)
    nxt = lambda b, i: (jnp.minimum((b * tiles + i + 1) * hpt, n_halo_blocks - 1), 0)
    vec = lambda b, i: (0, 0)
    return pl.pallas_call(
        functools.partial(_conv_kernel, tiles_per_seq=tiles),
        grid=(bsz, tiles),
        in_specs=[
            pl.BlockSpec((CONV_HALO, c), prev),
            pl.BlockSpec((TM, c), cur),
            pl.BlockSpec((CONV_HALO, c), nxt),
            pl.BlockSpec((C_KERNEL, c), vec),
            pl.BlockSpec((1, c), vec),
            pl.BlockSpec((1, c), vec),
            pl.BlockSpec((1, c), vec),
        ],
        out_specs=pl.BlockSpec((TM, c), cur),
        out_shape=jax.ShapeDtypeStruct((bsz * seq, c), BF16),
        scratch_shapes=[pltpu.VMEM((TM + 2 * CONV_HALO, c), F32)],
        compiler_params=_cparams(("arbitrary", "arbitrary"), 32),
        name="conv_module",
    )(u, u, u, dw_w, dw_b.reshape(1, c), ln_w.reshape(1, c), ln_b.reshape(1, c))


def _na_kernel(q_ref, k0_ref, k1_ref, k2_ref, v0_ref, v1_ref, v2_ref, kc_ref, vc_ref, bias_ref, o_ref):
    lane = lax.broadcasted_iota(jnp.int32, (1, LANES), 1)
    hi_half = lane >= HEAD_DIM
    for pb in range(D_HEADS // 2):
        sl = slice(pb * LANES, (pb + 1) * LANES)
        qp = q_ref[:, sl]
        kn = jnp.concatenate([k0_ref[:, sl], k1_ref[:, sl], k2_ref[:, sl]], axis=0)
        vn = jnp.concatenate([v0_ref[:, sl], v1_ref[:, sl], v2_ref[:, sl]], axis=0)
        kc = kc_ref[:, sl]
        vc = vc_ref[:, sl]
        halves = []
        for half in range(2):
            h = 2 * pb + half
            keep = hi_half if half else jnp.logical_not(hi_half)
            qh = jnp.where(keep, qp, jnp.zeros_like(qp))
            s_nb = _dot_nt(qh, kn) + bias_ref[h]
            s_cx = _dot_nt(qh, kc)
            mx = jnp.maximum(jnp.max(s_nb, axis=-1, keepdims=True), jnp.max(s_cx, axis=-1, keepdims=True))
            p_nb = jnp.exp(s_nb - mx)
            p_cx = jnp.exp(s_cx - mx)
            den = jnp.sum(p_nb, axis=-1, keepdims=True) + jnp.sum(p_cx, axis=-1, keepdims=True)
            o = _dot(p_nb.astype(BF16), vn) + _dot(p_cx.astype(BF16), vc)
            halves.append(o / den)
        o_ref[:, sl] = jnp.where(hi_half, halves[1], halves[0]).astype(BF16)


def _na_bias(rpb, rows):
    qr = NA_QROWS
    kinds = [0, qr, rows - qr]
    a = np.arange(qr)[:, None, None, None]
    qc = np.arange(GRID_W)[None, :, None, None]
    b = np.arange(3 * qr)[None, None, :, None]
    kc = np.arange(GRID_W)[None, None, None, :]
    row_idx, col_idx, valid = [], [], []
    for r0 in kinds:
        r = r0 + a
        rk = r0 - qr + b
        start = np.clip(r - NA_WIN_ROWS // 2, 0, rows - NA_WIN_ROWS)
        ok_r = (rk >= start) & (rk < start + NA_WIN_ROWS) & (rk >= 0) & (rk < rows)
        cs = np.clip(qc - NA_WIN_COLS // 2, 0, GRID_W - NA_WIN_COLS)
        ok_c = (kc >= cs) & (kc < cs + NA_WIN_COLS)
        ok = np.broadcast_to(ok_r & ok_c, (qr, GRID_W, 3 * qr, GRID_W))
        ro = np.broadcast_to(np.clip(rk - r + NA_WIN_ROWS - 1, 0, 2 * NA_WIN_ROWS - 2), ok.shape)
        co = np.broadcast_to(np.clip(kc - qc + NA_WIN_COLS - 1, 0, 2 * NA_WIN_COLS - 2), ok.shape)
        row_idx.append(ro.reshape(qr * GRID_W, 3 * qr * GRID_W))
        col_idx.append(co.reshape(qr * GRID_W, 3 * qr * GRID_W))
        valid.append(ok.reshape(qr * GRID_W, 3 * qr * GRID_W))
    row_idx, col_idx, valid = np.stack(row_idx), np.stack(col_idx), np.stack(valid)
    vals = rpb[:, row_idx, col_idx]
    return jnp.where(valid[None], vals, NEG_BIG).transpose(1, 0, 2, 3)


def _na(q, k, v, rpb, *, bsz, seq, lc):
    rows = seq // GRID_W
    tq = NA_QROWS * GRID_W
    nblk = seq // tq
    assert rows >= NA_WIN_ROWS and rows % NA_QROWS == 0 and NA_WIN_ROWS == 2 * NA_QROWS and nblk >= 3
    bias = _na_bias(rpb, rows)
    n_lat = bsz * seq
    qmap = lambda b, i: (b * nblk + i, 0)

    def kmap(off):
        return lambda b, i: (b * nblk + jnp.clip(i + off, 0, nblk - 1), 0)

    ctx = lambda b, i: (n_lat // lc + b, 0)
    kind = lambda b, i: (jnp.where(i == 0, 0, jnp.where(i == nblk - 1, 2, 1)), 0, 0, 0)
    blk = lambda m: pl.BlockSpec((tq, D_W), m)
    return pl.pallas_call(
        _na_kernel,
        grid=(bsz, nblk),
        in_specs=[
            blk(qmap), blk(kmap(-1)), blk(kmap(0)), blk(kmap(1)), blk(kmap(-1)), blk(kmap(0)), blk(kmap(1)),
            pl.BlockSpec((lc, D_W), ctx), pl.BlockSpec((lc, D_W), ctx),
            pl.BlockSpec((None, D_HEADS, tq, 3 * tq), kind),
        ],
        out_specs=blk(qmap),
        out_shape=jax.ShapeDtypeStruct((n_lat, D_W), BF16),
        compiler_params=_cparams(("arbitrary", "arbitrary"), 48),
        name="neighbourhood_attention",
    )(q, k, k, k, v, v, v, k, v, bias)


def kernel(x, c, ctx, c_ctx, w_mod, b_mod, ffn_w_gate, ffn_w_up, ffn_w_down, ab_w_in, ab_w_out, ab_q_norm,
           ab_k_norm, cd_w_in, cd_w_out, cd_dw_w, cd_dw_b, cd_ln_w, cd_ln_b, cd_rpb, final_norm):
    bsz, seq, d = x.shape
    lc = ctx.shape[1]
    depth = w_mod.shape[0]
    assert bsz * lc == TM and seq % TM == 0 and seq % ATT_TQ == 0 and seq % ATT_TK == 0
    n_lat_tiles = bsz * seq // TM
    n_all_tiles = n_lat_tiles + 1
    tiles_per_seq = seq // TM
    mod_row = lambda i: jnp.where(i == n_lat_tiles, bsz, i // tiles_per_seq)
    rope_row = lambda i: jnp.where(i == n_lat_tiles, tiles_per_seq, i % tiles_per_seq)

    mod = _mod_params(c, c_ctx, w_mod, b_mod)
    cos_t, sin_t = _rope_tables(seq)
    xt = jnp.concatenate([x.reshape(bsz * seq, d), ctx.reshape(bsz * lc, d)], axis=0)
    wg, wu, wd = ffn_w_gate.astype(BF16), ffn_w_up.astype(BF16), ffn_w_down.astype(BF16)

    for layer in range(depth):
        last = layer == depth - 1
        i = layer // 2
        mod_l = mod[layer]
        xt = _ffn(xt, mod_l, wg[layer, 0], wu[layer, 0], wd[layer, 0], base=0, n_tiles=n_all_tiles, mod_row=mod_row)
        n_out_tiles = n_lat_tiles if last else n_all_tiles
        if layer % 2 == 0:
            q, k, v, y = _inproj_ab(xt, mod_l, ab_w_in[i].astype(BF16), ab_q_norm[i], ab_k_norm[i], cos_t, sin_t,
                                    n_tiles=n_all_tiles, mod_row=mod_row, rope_row=rope_row)
            a1 = _gqa(q, k, v, bsz=bsz, seq=seq, lc=lc)
            a2 = _fourier(y, bsz=bsz, seq=seq, lc=lc)
            w_out = ab_w_out[i].astype(BF16)
        else:
            u, q, k, v = _inproj_cd(xt, mod_l, cd_w_in[i].astype(BF16), n_tiles=n_all_tiles, mod_row=mod_row)
            a1 = _conv_module(u, cd_dw_w[i], cd_dw_b[i], cd_ln_w[i], cd_ln_b[i], bsz=bsz, seq=seq)
            a2 = _na(q, k, v, cd_rpb[i], bsz=bsz, seq=seq, lc=lc)
            w_out = cd_w_out[i].astype(BF16)
        xt = _outproj(xt, mod_l, a1, a2, w_out, n_tiles=n_out_tiles, mod_row=mod_row)
        xt = _ffn(xt, mod_l, wg[layer, 1], wu[layer, 1], wd[layer, 1], base=6, n_tiles=n_out_tiles, mod_row=mod_row,
                  final_norm=final_norm if last else None)
    return xt.reshape(bsz, seq, d)
```

```python
import functools
import math

import numpy as np
import jax
import jax.numpy as jnp
from jax import lax
from jax.experimental import pallas as pl
from jax.experimental.pallas import tpu as pltpu

F32 = jnp.float32
BF16 = jnp.bfloat16

GRID_W = 64
HEAD_DIM = 64
A_Q_HEADS = 12
A_KV_HEADS = 4
B_GROUPS = 4
B_GROUP_DIM = 64
C_CHANNELS = 512
C_KERNEL = 31
D_HEADS = 8
NA_WIN_ROWS = 8
NA_WIN_COLS = 16
ROPE_THETA = 10000.0
NORM_EPS = 1e-6
N_MOD = 9
FFN_RES_WEIGHT = 0.5

A_Q_W = A_Q_HEADS * HEAD_DIM
A_KV_W = A_KV_HEADS * HEAD_DIM
B_W = B_GROUPS * B_GROUP_DIM
D_W = D_HEADS * HEAD_DIM

LANES = 128
SUBLANES = 8
MXU_DIM = 256

TM = 512
FFN_CHUNK = 256
ATT_TQ = 512
ATT_TK = 512
NA_QROWS = 4
CONV_HALO = 16
ATT_QCHUNK = 256
ATT_LOOKAHEAD = 4
NEG_BIG = -1e30
Q_SCALE_LOG2 = (HEAD_DIM ** -0.5) * math.log2(math.e)

_MiB = 1 << 20


def _cparams(sem, vmem_mib):
    return pltpu.CompilerParams(dimension_semantics=sem, vmem_limit_bytes=vmem_mib * _MiB)


def _dot(a, b):
    return jnp.dot(a, b, preferred_element_type=F32)


def _dot_nt(a, b):
    return lax.dot_general(a, b, (((1,), (1,)), ((), ())), preferred_element_type=F32)


def _sigmoid(x):
    return 1.0 / (1.0 + jnp.exp(-x))


def _cos_sin(n):
    idx = np.arange(n)
    ang = 2.0 * np.pi * (np.outer(idx, idx) % n) / n
    return np.cos(ang), np.sin(ang)


def _const_bf16(a):
    return jnp.asarray(np.asarray(a, np.float32)).astype(BF16)


def _modulate(x, shift, scale):
    ms = jnp.mean(x * x, axis=-1, keepdims=True)
    return (x * lax.rsqrt(ms + NORM_EPS)) * (1.0 + scale) + shift


def _mod_kernel(ct_ref, w_ref, b_ref, o_ref, *, n_rows):
    ct = ct_ref[...]
    a = ct * _sigmoid(ct)
    w = w_ref[...]
    rows = [jnp.sum(w * a[:, r:r + 1], axis=0, keepdims=True) for r in range(n_rows)]
    rows.append(jnp.zeros((SUBLANES - n_rows, w.shape[1]), F32))
    o_ref[...] = jnp.concatenate(rows, axis=0) + b_ref[...]


def _mod_params(c, c_ctx, w_mod, b_mod):
    depth, d, nmd = w_mod.shape
    bsz = c.shape[0]
    n_rows = bsz + 1
    assert n_rows <= SUBLANES
    ct = jnp.concatenate([c, c_ctx[None], jnp.zeros((SUBLANES - n_rows, d), F32)], axis=0).T
    tn = d
    out = pl.pallas_call(
        functools.partial(_mod_kernel, n_rows=n_rows),
        grid=(depth, nmd // tn),
        in_specs=[
            pl.BlockSpec((d, SUBLANES), lambda l, n: (0, 0)),
            pl.BlockSpec((None, d, tn), lambda l, n: (l, 0, n)),
            pl.BlockSpec((None, 1, tn), lambda l, n: (l, 0, n)),
        ],
        out_specs=pl.BlockSpec((None, SUBLANES, tn), lambda l, n: (l, 0, n)),
        out_shape=jax.ShapeDtypeStruct((depth, SUBLANES, nmd), F32),
        compiler_params=_cparams(("arbitrary", "arbitrary"), 32),
        name="adaln_params",
    )(ct, w_mod, b_mod.reshape(depth, 1, nmd))
    return out.reshape(depth, SUBLANES, N_MOD, d)


def _ffn_kernel(x_ref, mod_ref, wg_ref, wu_ref, wd_ref, *rest, base, final, n_chunks):
    if final:
        fn_ref, o_ref, acc_ref = rest
    else:
        o_ref, acc_ref = rest
    x = x_ref[...]
    m = mod_ref[...]
    h = _modulate(x, m[base:base + 1], m[base + 1:base + 2]).astype(BF16)
    for c in range(n_chunks):
        sl = slice(c * FFN_CHUNK, (c + 1) * FFN_CHUNK)
        g = _dot(h, wg_ref[:, sl])
        u = _dot(h, wu_ref[:, sl])
        a = ((g * _sigmoid(g)) * u).astype(BF16)
        part = _dot(a, wd_ref[sl, :])
        if c == 0:
            acc_ref[...] = part
        else:
            acc_ref[...] += part
    y = x + (FFN_RES_WEIGHT * m[base + 2:base + 3]) * acc_ref[...]
    if final:
        ms = jnp.mean(y * y, axis=-1, keepdims=True)
        y = (y * lax.rsqrt(ms + NORM_EPS)) * fn_ref[...]
    o_ref[...] = y


def _ffn(x, mod_l, wg, wu, wd, *, base, n_tiles, mod_row, final_norm=None):
    ntok, d = x.shape
    dff = wg.shape[1]
    assert dff % FFN_CHUNK == 0
    final = final_norm is not None
    resident = dict(pipeline_mode=pl.Buffered(1))
    in_specs = [
        pl.BlockSpec((TM, d), lambda i: (i, 0)),
        pl.BlockSpec((None, N_MOD, d), lambda i: (mod_row(i), 0, 0)),
        pl.BlockSpec((d, dff), lambda i: (0, 0), **resident),
        pl.BlockSpec((d, dff), lambda i: (0, 0), **resident),
        pl.BlockSpec((dff, d), lambda i: (0, 0), **resident),
    ]
    args = [x, mod_l, wg, wu, wd]
    if final:
        in_specs.append(pl.BlockSpec((1, d), lambda i: (0, 0)))
        args.append(final_norm.reshape(1, d))
    return pl.pallas_call(
        functools.partial(_ffn_kernel, base=base, final=final, n_chunks=dff // FFN_CHUNK),
        grid=(n_tiles,),
        in_specs=in_specs,
        out_specs=pl.BlockSpec((TM, d), lambda i: (i, 0)),
        out_shape=jax.ShapeDtypeStruct((n_tiles * TM, d), F32),
        scratch_shapes=[pltpu.VMEM((TM, d), F32)],
        compiler_params=_cparams(("arbitrary",), 48),
        name="ffn_final" if final else "ffn",
    )(*args)


def _outproj_kernel(x_ref, mod_ref, a1_ref, a2_ref, w1_ref, w2_ref, o_ref):
    y = _dot(a1_ref[...].astype(BF16), w1_ref[...]) + _dot(a2_ref[...].astype(BF16), w2_ref[...])
    o_ref[...] = x_ref[...] + mod_ref[...][5:6] * y


def _outproj(x, mod_l, a1, a2, w_out, *, n_tiles, mod_row):
    d = x.shape[1]
    d1, d2 = a1.shape[1], a2.shape[1]
    w1, w2 = w_out[:d1], w_out[d1:]
    return pl.pallas_call(
        _outproj_kernel,
        grid=(n_tiles,),
        in_specs=[
            pl.BlockSpec((TM, d), lambda i: (i, 0)),
            pl.BlockSpec((None, N_MOD, d), lambda i: (mod_row(i), 0, 0)),
            pl.BlockSpec((TM, d1), lambda i: (i, 0)),
            pl.BlockSpec((TM, d2), lambda i: (i, 0)),
            pl.BlockSpec((d1, d), lambda i: (0, 0)),
            pl.BlockSpec((d2, d), lambda i: (0, 0)),
        ],
        out_specs=pl.BlockSpec((TM, d), lambda i: (i, 0)),
        out_shape=jax.ShapeDtypeStruct((n_tiles * TM, d), F32),
        compiler_params=_cparams(("arbitrary",), 32),
        name="out_proj",
    )(x, mod_l, a1, a2, w1, w2)


def _swap_halves(z, low_half):
    n = z.shape[1]
    return jnp.where(low_half, pltpu.roll(z, n - HEAD_DIM // 2, 1), pltpu.roll(z, HEAD_DIM // 2, 1))


def _head_sumsq(z, ones_bd):
    zz = z * z
    hi = zz.astype(BF16)
    lo = (zz - hi.astype(F32)).astype(BF16)
    return _dot(hi, ones_bd) + _dot(lo, ones_bd)


def _inproj_ab_kernel(x_ref, mod_ref, w_ref, qn_ref, kn_ref, cos_ref, sin_ref, ones_ref, dft_ref,
                      q_ref, k_ref, vt_ref, y_ref):
    x = x_ref[...]
    m = mod_ref[...]
    h = _modulate(x, m[3:4], m[4:5]).astype(BF16)
    y = _dot(h, w_ref[...])
    ones_bd = ones_ref[...]
    cos = jnp.concatenate([cos_ref[...]] * 2, axis=1)
    sin = jnp.concatenate([sin_ref[...]] * 2, axis=1)
    lane = lax.broadcasted_iota(jnp.int32, (1, MXU_DIM), 1)
    low_half = (lane % HEAD_DIM) < (HEAD_DIM // 2)

    def norm_rope(z, nw):
        ss = _head_sumsq(z, ones_bd)
        zn = (z * lax.rsqrt(ss * (1.0 / HEAD_DIM) + NORM_EPS)) * nw
        return zn * cos + _swap_halves(zn, low_half) * sin

    qn = qn_ref[...]
    kn = kn_ref[...]
    for blk in range(A_Q_W // MXU_DIM):
        sl = slice(blk * MXU_DIM, (blk + 1) * MXU_DIM)
        q_ref[:, sl] = (norm_rope(y[:, sl], qn) * Q_SCALE_LOG2).astype(BF16)
    k_ref[...] = norm_rope(y[:, A_Q_W:A_Q_W + A_KV_W], kn).astype(BF16)
    vt = y[:, A_Q_W + A_KV_W:A_Q_W + 2 * A_KV_W].T
    ones = jnp.ones((HEAD_DIM, vt.shape[1]), F32)
    groups = []
    for kvh in range(A_KV_HEADS):
        groups += [vt[kvh * HEAD_DIM:(kvh + 1) * HEAD_DIM], ones]
    vt_ref[...] = jnp.concatenate(groups, axis=0).astype(BF16)
    f = y[:, A_Q_W + 2 * A_KV_W:].astype(BF16)
    y_ref[...] = _dot(f, dft_ref[...])


def _inproj_ab(x, mod_l, w_in, q_norm, k_norm, cos_t, sin_t, *, n_tiles, mod_row, rope_row):
    ntok, d = x.shape
    n_in = w_in.shape[1]
    ones_bd = _const_bf16(np.kron(np.eye(MXU_DIM // HEAD_DIM), np.ones((HEAD_DIM, HEAD_DIM))))
    gc, gs = _cos_sin(B_GROUP_DIM)
    eye = np.eye(B_GROUPS)
    dft = _const_bf16(np.concatenate([np.kron(eye, gc), np.kron(eye, gs)], axis=1))
    qn = jnp.tile(q_norm, MXU_DIM // HEAD_DIM).reshape(1, MXU_DIM)
    kn = jnp.tile(k_norm, MXU_DIM // HEAD_DIM).reshape(1, MXU_DIM)
    const = lambda i: (0, 0)
    row = lambda i: (i, 0)
    return pl.pallas_call(
        _inproj_ab_kernel,
        grid=(n_tiles,),
        in_specs=[
            pl.BlockSpec((TM, d), row),
            pl.BlockSpec((None, N_MOD, d), lambda i: (mod_row(i), 0, 0)),
            pl.BlockSpec((d, n_in), const),
            pl.BlockSpec((1, MXU_DIM), const),
            pl.BlockSpec((1, MXU_DIM), const),
            pl.BlockSpec((TM, LANES), lambda i: (rope_row(i), 0)),
            pl.BlockSpec((TM, LANES), lambda i: (rope_row(i), 0)),
            pl.BlockSpec((MXU_DIM, MXU_DIM), const),
            pl.BlockSpec((B_W, 2 * B_W), const),
        ],
        out_specs=[
            pl.BlockSpec((TM, A_Q_W), row),
            pl.BlockSpec((TM, A_KV_W), row),
            pl.BlockSpec((A_KV_HEADS * LANES, TM), lambda i: (0, i)),
            pl.BlockSpec((TM, 2 * B_W), row),
        ],
        out_shape=[
            jax.ShapeDtypeStruct((ntok, A_Q_W), BF16),
            jax.ShapeDtypeStruct((ntok, A_KV_W), BF16),
            jax.ShapeDtypeStruct((A_KV_HEADS * LANES, ntok), BF16),
            jax.ShapeDtypeStruct((ntok, 2 * B_W), F32),
        ],
        compiler_params=_cparams(("arbitrary",), 40),
        name="in_proj_ab",
    )(x, mod_l, w_in, qn, kn, cos_t, sin_t, ones_bd, dft)


def _rope_tables(seq):
    t = jnp.arange(seq)
    row = (t // GRID_W).astype(F32)
    col = (t % GRID_W).astype(F32)
    n_ax = HEAD_DIM // 4
    inv = ROPE_THETA ** (-jnp.arange(n_ax, dtype=F32) / n_ax)
    ang = jnp.concatenate([row[:, None] * inv, col[:, None] * inv], axis=-1)
    cos, sin = jnp.cos(ang), jnp.sin(ang)
    cos_h = jnp.concatenate([cos, cos], axis=-1)
    sin_h = jnp.concatenate([-sin, sin], axis=-1)
    cos_t = jnp.concatenate([jnp.tile(cos_h, (1, 2)), jnp.ones((TM, LANES), F32)], axis=0)
    sin_t = jnp.concatenate([jnp.tile(sin_h, (1, 2)), jnp.zeros((TM, LANES), F32)], axis=0)
    return cos_t, sin_t


def _gqa_kernel(q_ref, kc_ref, vtc_ref, *rest, has_lat):
    if has_lat:
        kl_ref, vtl_ref, o_ref, qm_ref, m_ref, acc_ref = rest
    else:
        _, o_ref, qm_ref, m_ref, acc_ref = rest
    j = pl.program_id(2)
    group = A_Q_HEADS // A_KV_HEADS
    lane = lax.broadcasted_iota(jnp.int32, (1, LANES), 1)
    hi_half = lane >= HEAD_DIM
    n_chunks = q_ref.shape[0] // ATT_QCHUNK

    def update(k_ref, vt_ref):
        bodies = [(h, c) for h in range(A_Q_HEADS) for c in range(n_chunks)]

        def scores(h, c):
            kvp = (h // group) // 2
            kp = k_ref[:, kvp * LANES:(kvp + 1) * LANES]
            return _dot_nt(kp, qm_ref[h, c * ATT_QCHUNK:(c + 1) * ATT_QCHUNK, :])

        def accumulate(h, c, alpha, pv):
            cs = slice(c * ATT_QCHUNK, (c + 1) * ATT_QCHUNK)
            acc_ref[h, :, cs] = alpha * acc_ref[h, :, cs] + pv

        queue = [scores(*bodies[t]) for t in range(min(ATT_LOOKAHEAD, len(bodies)))]
        pending = None
        for idx, (h, c) in enumerate(bodies):
            s = queue.pop(0)
            if idx + ATT_LOOKAHEAD < len(bodies):
                queue.append(scores(*bodies[idx + ATT_LOOKAHEAD]))
            cs = slice(c * ATT_QCHUNK, (c + 1) * ATT_QCHUNK)
            kvh = h // group
            vt = vt_ref[kvh * LANES:(kvh + 1) * LANES, :]
            m_prev = m_ref[h, :, cs]
            m_new = jnp.maximum(m_prev, jnp.max(s, axis=0, keepdims=True))
            alpha = jnp.exp2(m_prev - m_new)
            p = jnp.exp2(s - m_new).astype(BF16)
            m_ref[h, :, cs] = m_new
            pv = _dot(vt, p)
            if pending is not None:
                accumulate(*pending)
            pending = (h, c, alpha, pv)
        accumulate(*pending)

    @pl.when(j == 0)
    def _first():
        m_ref[...] = jnp.full(m_ref.shape, NEG_BIG, F32)
        acc_ref[...] = jnp.zeros(acc_ref.shape, F32)
        for h in range(A_Q_HEADS):
            qp = q_ref[:, (h // 2) * LANES:(h // 2 + 1) * LANES].astype(F32)
            kv_half = (h // group) % 2
            if kv_half != h % 2:
                qp = pltpu.roll(qp, HEAD_DIM, 1)
            keep = hi_half if kv_half else jnp.logical_not(hi_half)
            qm_ref[h] = jnp.where(keep, qp, jnp.zeros_like(qp)).astype(BF16)
        update(kc_ref, vtc_ref)

    if has_lat:
        @pl.when(j > 0)
        def _rest():
            update(kl_ref, vtl_ref)

    @pl.when(j == pl.num_programs(2) - 1)
    def _final():
        for pb in range(A_Q_HEADS // 2):
            for c in range(n_chunks):
                cs = slice(c * ATT_QCHUNK, (c + 1) * ATT_QCHUNK)
                parts = []
                for h in (2 * pb, 2 * pb + 1):
                    a = acc_ref[h, :, cs]
                    parts.append(a[:HEAD_DIM] / a[HEAD_DIM:HEAD_DIM + 1])
                o_ref[cs, pb * LANES:(pb + 1) * LANES] = jnp.concatenate(parts, axis=0).T.astype(BF16)


def _gqa_scratch(tq):
    return [
        pltpu.VMEM((A_Q_HEADS, tq, LANES), BF16),
        pltpu.VMEM((A_Q_HEADS, 1, tq), F32),
        pltpu.VMEM((A_Q_HEADS, LANES, tq), F32),
    ]


def _gqa(q, k, vt, *, bsz, seq, lc):
    ntok = q.shape[0]
    n_lat = bsz * seq
    tq, tk = ATT_TQ, ATT_TK
    ctx_blk = lambda b: n_lat // lc + b
    o = pl.pallas_call(
        functools.partial(_gqa_kernel, has_lat=True),
        grid=(bsz, seq // tq, 1 + seq // tk),
        in_specs=[
            pl.BlockSpec((tq, A_Q_W), lambda b, i, j: (b * (seq // tq) + i, 0)),
            pl.BlockSpec((lc, A_KV_W), lambda b, i, j: (ctx_blk(b), 0)),
            pl.BlockSpec((A_KV_HEADS * LANES, lc), lambda b, i, j: (0, ctx_blk(b))),
            pl.BlockSpec((tk, A_KV_W), lambda b, i, j: (b * (seq // tk) + jnp.maximum(j - 1, 0), 0)),
            pl.BlockSpec((A_KV_HEADS * LANES, tk), lambda b, i, j: (0, b * (seq // tk) + jnp.maximum(j - 1, 0))),
        ],
        out_specs=pl.BlockSpec((tq, A_Q_W), lambda b, i, j: (b * (seq // tq) + i, 0)),
        out_shape=jax.ShapeDtypeStruct((ntok, A_Q_W), BF16),
        scratch_shapes=_gqa_scratch(tq),
        compiler_params=_cparams(("arbitrary", "arbitrary", "arbitrary"), 48),
        name="gqa_latent",
    )(q, k, vt, k, vt)
    return pl.pallas_call(
        functools.partial(_gqa_kernel, has_lat=False),
        grid=(bsz, 1, 1),
        in_specs=[
            pl.BlockSpec((lc, A_Q_W), lambda b, i, j: (ctx_blk(b), 0)),
            pl.BlockSpec((lc, A_KV_W), lambda b, i, j: (ctx_blk(b), 0)),
            pl.BlockSpec((A_KV_HEADS * LANES, lc), lambda b, i, j: (0, ctx_blk(b))),
            pl.BlockSpec(memory_space=pl.ANY),
        ],
        out_specs=pl.BlockSpec((lc, A_Q_W), lambda b, i, j: (ctx_blk(b), 0)),
        out_shape=jax.ShapeDtypeStruct((ntok, A_Q_W), BF16),
        scratch_shapes=_gqa_scratch(lc),
        input_output_aliases={3: 0},
        compiler_params=_cparams(("arbitrary", "arbitrary", "arbitrary"), 32),
        name="gqa_context",
    )(q, k, vt, o)


def _dft_a_kernel(y_ref, ca_ref, tc_ref, ts_ref, d_ref, *, nbb):
    ca = ca_ref[...]
    ra = ca.shape[1]
    for j in range(nbb):
        yj = y_ref[:, j, :].astype(BF16)
        p = _dot(ca, yj)
        br = p[:ra, :B_W] - p[ra:, B_W:]
        bi = -p[:ra, B_W:] - p[ra:, :B_W]
        tc = jnp.concatenate([tc_ref[j]] * (B_W // LANES), axis=1)
        ts = jnp.concatenate([ts_ref[j]] * (B_W // LANES), axis=1)
        d_ref[:, 0, j, :] = br * tc + bi * ts
        d_ref[:, 1, j, :] = bi * tc - br * ts


def _dft_b_kernel(d_ref, cb_ref, o_ref, *, kb, scale):
    cb = cb_ref[...]
    for j in range(kb):
        o_ref[:, j, :] = _dot(cb, d_ref[j].astype(BF16)) * scale


def _dft_ctx_kernel(y_ref, cs_ref, prev_ref, o_ref, *, scale):
    del prev_ref
    y = y_ref[...]
    cs = cs_ref[...]
    lc = cs.shape[0]
    yc = y[:, :B_W].astype(BF16)
    ys = y[:, B_W:].astype(BF16)
    o_ref[...] = (_dot(cs[:, :lc], yc) - _dot(cs[:, lc:], ys)) * scale


def _fourier(y, *, bsz, seq, lc):
    ntok = y.shape[0]
    ra = seq // GRID_W
    nb = GRID_W
    nbb = SUBLANES
    kb = SUBLANES
    assert ntok % nb == 0 and ra % SUBLANES == 0 and (ntok // nb) % 1 == 0
    ca_c, ca_s = _cos_sin(ra)
    ca = _const_bf16(np.concatenate([ca_c, ca_s], axis=0))
    th = 2.0 * np.pi * np.outer(np.arange(nb), np.arange(ra)) / seq
    tc = jnp.asarray(np.broadcast_to(np.cos(th)[:, :, None], (nb, ra, LANES)).astype(np.float32))
    ts = jnp.asarray(np.broadcast_to(np.sin(th)[:, :, None], (nb, ra, LANES)).astype(np.float32))
    cb_c, cb_s = _cos_sin(nb)
    cb = _const_bf16(np.concatenate([cb_c, cb_s], axis=1))

    y3 = y.reshape(ntok // nb, nb, 2 * B_W)
    d = pl.pallas_call(
        functools.partial(_dft_a_kernel, nbb=nbb),
        grid=(bsz, nb // nbb),
        in_specs=[
            pl.BlockSpec((ra, nbb, 2 * B_W), lambda b, i: (b, i, 0)),
            pl.BlockSpec((2 * ra, ra), lambda b, i: (0, 0)),
            pl.BlockSpec((nbb, ra, LANES), lambda b, i: (i, 0, 0)),
            pl.BlockSpec((nbb, ra, LANES), lambda b, i: (i, 0, 0)),
        ],
        out_specs=pl.BlockSpec((None, ra, 2, nbb, B_W), lambda b, i: (b, 0, 0, i, 0)),
        out_shape=jax.ShapeDtypeStruct((bsz, ra, 2, nb, B_W), F32),
        compiler_params=_cparams(("arbitrary", "arbitrary"), 32),
        name="fourier_rows",
    )(y3, ca, tc, ts)
    d = d.reshape(bsz, ra, 2 * nb, B_W)

    scale = 1.0 / math.sqrt(seq * B_GROUP_DIM)
    n_out = ntok // ra
    assert ntok % ra == 0
    z3 = pl.pallas_call(
        functools.partial(_dft_b_kernel, kb=kb, scale=scale),
        grid=(bsz, ra // kb),
        in_specs=[
            pl.BlockSpec((None, kb, 2 * nb, B_W), lambda b, i: (b, i, 0, 0)),
            pl.BlockSpec((nb, 2 * nb), lambda b, i: (0, 0)),
        ],
        out_specs=pl.BlockSpec((nb, kb, B_W), lambda b, i: (b, i, 0)),
        out_shape=jax.ShapeDtypeStruct((n_out, ra, B_W), F32),
        compiler_params=_cparams(("arbitrary", "arbitrary"), 32),
        name="fourier_cols",
    )(d, cb)
    z = z3.reshape(ntok, B_W)

    cc, cs_ = _cos_sin(lc)
    csm = _const_bf16(np.concatenate([cc, cs_], axis=1))
    ctx_blk = lambda b: (bsz * seq) // lc + b
    return pl.pallas_call(
        functools.partial(_dft_ctx_kernel, scale=1.0 / math.sqrt(lc * B_GROUP_DIM)),
        grid=(bsz,),
        in_specs=[
            pl.BlockSpec((lc, 2 * B_W), lambda b: (ctx_blk(b), 0)),
            pl.BlockSpec((lc, 2 * lc), lambda b: (0, 0)),
            pl.BlockSpec(memory_space=pl.ANY),
        ],
        out_specs=pl.BlockSpec((lc, B_W), lambda b: (ctx_blk(b), 0)),
        out_shape=jax.ShapeDtypeStruct((ntok, B_W), F32),
        input_output_aliases={2: 0},
        compiler_params=_cparams(("arbitrary",), 32),
        name="fourier_context",
    )(y, csm, z)


def _inproj_cd_kernel(x_ref, mod_ref, w_ref, u_ref, q_ref, k_ref, v_ref):
    x = x_ref[...]
    m = mod_ref[...]
    h = _modulate(x, m[3:4], m[4:5]).astype(BF16)
    y = _dot(h, w_ref[...])
    c = C_CHANNELS
    u_ref[...] = y[:, :c] * _sigmoid(y[:, c:2 * c])
    q_ref[...] = (y[:, 2 * c:2 * c + D_W] * (HEAD_DIM ** -0.5)).astype(BF16)
    k_ref[...] = y[:, 2 * c + D_W:2 * c + 2 * D_W].astype(BF16)
    v_ref[...] = y[:, 2 * c + 2 * D_W:].astype(BF16)


def _inproj_cd(x, mod_l, w_in, *, n_tiles, mod_row):
    ntok, d = x.shape
    n_in = w_in.shape[1]
    row = lambda i: (i, 0)
    return pl.pallas_call(
        _inproj_cd_kernel,
        grid=(n_tiles,),
        in_specs=[
            pl.BlockSpec((TM, d), row),
            pl.BlockSpec((None, N_MOD, d), lambda i: (mod_row(i), 0, 0)),
            pl.BlockSpec((d, n_in), lambda i: (0, 0)),
        ],
        out_specs=[
            pl.BlockSpec((TM, C_CHANNELS), row),
            pl.BlockSpec((TM, D_W), row),
            pl.BlockSpec((TM, D_W), row),
            pl.BlockSpec((TM, D_W), row),
        ],
        out_shape=[
            jax.ShapeDtypeStruct((ntok, C_CHANNELS), F32),
            jax.ShapeDtypeStruct((ntok, D_W), BF16),
            jax.ShapeDtypeStruct((ntok, D_W), BF16),
            jax.ShapeDtypeStruct((ntok, D_W), BF16),
        ],
        compiler_params=_cparams(("arbitrary",), 40),
        name="in_proj_cd",
    )(x, mod_l, w_in)


def _conv_kernel(prev_ref, cur_ref, next_ref, w_ref, b_ref, lnw_ref, lnb_ref, o_ref, buf_ref, *, tiles_per_seq):
    i = pl.program_id(1)
    halo = CONV_HALO
    tm = cur_ref.shape[0]
    zeros = jnp.zeros((halo, cur_ref.shape[1]), F32)
    buf_ref[0:halo, :] = jnp.where(i > 0, prev_ref[...], zeros)
    buf_ref[halo:halo + tm, :] = cur_ref[...]
    buf_ref[halo + tm:, :] = jnp.where(i < tiles_per_seq - 1, next_ref[...], zeros)
    w = w_ref[...]
    off = halo - C_KERNEL // 2
    acc = buf_ref[off:off + tm, :] * w[0:1]
    for t in range(1, C_KERNEL):
        acc = acc + buf_ref[off + t:off + t + tm, :] * w[t:t + 1]
    y = acc + b_ref[...]
    mu = jnp.mean(y, axis=-1, keepdims=True)
    yc = y - mu
    var = jnp.mean(yc * yc, axis=-1, keepdims=True)
    z = (yc * lax.rsqrt(var + NORM_EPS)) * lnw_ref[...] + lnb_ref[...]
    o_ref[...] = (z * _sigmoid(z)).astype(BF16)


def _conv_module(u, dw_w, dw_b, ln_w, ln_b, *, bsz, seq):
    c = u.shape[1]
    tiles = seq // TM
    hpt = TM // CONV_HALO
    n_halo_blocks = u.shape[0] // CONV_HALO
    cur = lambda b, i: (b * tiles + i, 0)
    prev = lambda b, i: (jnp.maximum((b * tiles + i) * hpt - 1, 0), 0)
    nxt = lambda b, i: (jnp.minimum((b * tiles + i + 1) * hpt, n_halo_blocks - 1), 0)
    vec = lambda b, i: (0, 0)
    return pl.pallas_call(
        functools.partial(_conv_kernel, tiles_per_seq=tiles),
        grid=(bsz, tiles),
        in_specs=[
            pl.BlockSpec((CONV_HALO, c), prev),
            pl.BlockSpec((TM, c), cur),
            pl.BlockSpec((CONV_HALO, c), nxt),
            pl.BlockSpec((C_KERNEL, c), vec),
            pl.BlockSpec((1, c), vec),
            pl.BlockSpec((1, c), vec),
            pl.BlockSpec((1, c), vec),
        ],
        out_specs=pl.BlockSpec((TM, c), cur),
        out_shape=jax.ShapeDtypeStruct((bsz * seq, c), BF16),
        scratch_shapes=[pltpu.VMEM((TM + 2 * CONV_HALO, c), F32)],
        compiler_params=_cparams(("arbitrary", "arbitrary"), 32),
        name="conv_module",
    )(u, u, u, dw_w, dw_b.reshape(1, c), ln_w.reshape(1, c), ln_b.reshape(1, c))


def _na_kernel(q_ref, k0_ref, k1_ref, k2_ref, v0_ref, v1_ref, v2_ref, kc_ref, vc_ref, bias_ref, o_ref):
    lane = lax.broadcasted_iota(jnp.int32, (1, LANES), 1)
    hi_half = lane >= HEAD_DIM
    for pb in range(D_HEADS // 2):
        sl = slice(pb * LANES, (pb + 1) * LANES)
        qp = q_ref[:, sl]
        kn = jnp.concatenate([k0_ref[:, sl], k1_ref[:, sl], k2_ref[:, sl]], axis=0)
        vn = jnp.concatenate([v0_ref[:, sl], v1_ref[:, sl], v2_ref[:, sl]], axis=0)
        kc = kc_ref[:, sl]
        vc = vc_ref[:, sl]
        halves = []
        for half in range(2):
            h = 2 * pb + half
            keep = hi_half if half else jnp.logical_not(hi_half)
            qh = jnp.where(keep, qp, jnp.zeros_like(qp))
            s_nb = _dot_nt(qh, kn) + bias_ref[h]
            s_cx = _dot_nt(qh, kc)
            mx = jnp.maximum(jnp.max(s_nb, axis=-1, keepdims=True), jnp.max(s_cx, axis=-1, keepdims=True))
            p_nb = jnp.exp(s_nb - mx)
            p_cx = jnp.exp(s_cx - mx)
            den = jnp.sum(p_nb, axis=-1, keepdims=True) + jnp.sum(p_cx, axis=-1, keepdims=True)
            o = _dot(p_nb.astype(BF16), vn) + _dot(p_cx.astype(BF16), vc)
            halves.append(o / den)
        o_ref[:, sl] = jnp.where(hi_half, halves[1], halves[0]).astype(BF16)


def _na_bias_kernel(rpb_ref, o_ref, *, rows):
    h = pl.program_id(0)
    n_ro, n_co = 2 * NA_WIN_ROWS - 1, 2 * NA_WIN_COLS - 1
    qc = lax.broadcasted_iota(jnp.int32, (GRID_W, LANES), 0)
    lane = lax.broadcasted_iota(jnp.int32, (GRID_W, LANES), 1)
    kc = lane % GRID_W
    d = kc - qc + (NA_WIN_COLS - 1)
    cs = jnp.clip(qc - NA_WIN_COLS // 2, 0, GRID_W - NA_WIN_COLS)
    col_ok = (kc >= cs) & (kc < cs + NA_WIN_COLS)
    neg = jnp.full((GRID_W, LANES), NEG_BIG, F32)
    band = []
    for ro in range(n_ro):
        acc = neg
        for co in range(n_co):
            acc = jnp.where(d == co, rpb_ref[(h * n_ro + ro) * n_co + co], acc)
        band.append(jnp.where(col_ok, acc, neg))
    qr = NA_QROWS
    for kind, r0 in enumerate((0, qr, rows - qr)):
        for a in range(qr):
            r = r0 + a
            start = min(max(r - NA_WIN_ROWS // 2, 0), rows - NA_WIN_ROWS)
            for pair in range(3 * qr // 2):
                halves = []
                for b in (2 * pair, 2 * pair + 1):
                    rk = r0 - qr + b
                    ok = start <= rk < start + NA_WIN_ROWS and 0 <= rk < rows
                    halves.append(band[rk - r + NA_WIN_ROWS - 1] if ok else neg)
                o_ref[kind, a * GRID_W:(a + 1) * GRID_W, pair * LANES:(pair + 1) * LANES] = jnp.where(
                    lane < GRID_W, halves[0], halves[1])


def _na_bias(rpb, rows):
    heads = rpb.shape[0]
    tq = NA_QROWS * GRID_W
    return pl.pallas_call(
        functools.partial(_na_bias_kernel, rows=rows),
        grid=(heads,),
        in_specs=[pl.BlockSpec(memory_space=pltpu.SMEM)],
        out_specs=pl.BlockSpec((3, None, tq, 3 * tq), lambda h: (0, h, 0, 0)),
        out_shape=jax.ShapeDtypeStruct((3, heads, tq, 3 * tq), F32),
        compiler_params=_cparams(("arbitrary",), 32),
        name="na_bias",
    )(rpb.reshape(-1))


def _na(q, k, v, rpb, *, bsz, seq, lc):
    rows = seq // GRID_W
    tq = NA_QROWS * GRID_W
    nblk = seq // tq
    assert rows >= NA_WIN_ROWS and rows % NA_QROWS == 0 and NA_WIN_ROWS == 2 * NA_QROWS and nblk >= 3
    bias = _na_bias(rpb, rows)
    n_lat = bsz * seq
    qmap = lambda b, i: (b * nblk + i, 0)

    def kmap(off):
        return lambda b, i: (b * nblk + jnp.clip(i + off, 0, nblk - 1), 0)

    ctx = lambda b, i: (n_lat // lc + b, 0)
    kind = lambda b, i: (jnp.where(i == 0, 0, jnp.where(i == nblk - 1, 2, 1)), 0, 0, 0)
    blk = lambda m: pl.BlockSpec((tq, D_W), m)
    return pl.pallas_call(
        _na_kernel,
        grid=(bsz, nblk),
        in_specs=[
            blk(qmap), blk(kmap(-1)), blk(kmap(0)), blk(kmap(1)), blk(kmap(-1)), blk(kmap(0)), blk(kmap(1)),
            pl.BlockSpec((lc, D_W), ctx), pl.BlockSpec((lc, D_W), ctx),
            pl.BlockSpec((None, D_HEADS, tq, 3 * tq), kind),
        ],
        out_specs=blk(qmap),
        out_shape=jax.ShapeDtypeStruct((n_lat, D_W), BF16),
        compiler_params=_cparams(("arbitrary", "arbitrary"), 48),
        name="neighbourhood_attention",
    )(q, k, k, k, v, v, v, k, v, bias)


def kernel(x, c, ctx, c_ctx, w_mod, b_mod, ffn_w_gate, ffn_w_up, ffn_w_down, ab_w_in, ab_w_out, ab_q_norm,
           ab_k_norm, cd_w_in, cd_w_out, cd_dw_w, cd_dw_b, cd_ln_w, cd_ln_b, cd_rpb, final_norm):
    bsz, seq, d = x.shape
    lc = ctx.shape[1]
    depth = w_mod.shape[0]
    assert depth == 2, "layer 0 (A/B mixer) updates the context, layer 1 (C/D mixer) is the last layer"
    assert bsz * lc == TM and seq % TM == 0 and seq % ATT_TQ == 0 and seq % ATT_TK == 0
    n_lat_tiles = bsz * seq // TM
    n_all_tiles = n_lat_tiles + 1
    tiles_per_seq = seq // TM
    mod_row = lambda i: jnp.where(i == n_lat_tiles, bsz, i // tiles_per_seq)
    rope_row = lambda i: jnp.where(i == n_lat_tiles, tiles_per_seq, i % tiles_per_seq)

    mod = _mod_params(c, c_ctx, w_mod, b_mod)
    cos_t, sin_t = _rope_tables(seq)
    xt = jnp.concatenate([x.reshape(bsz * seq, d), ctx.reshape(bsz * lc, d)], axis=0)
    wg, wu, wd = ffn_w_gate.astype(BF16), ffn_w_up.astype(BF16), ffn_w_down.astype(BF16)

    for layer in range(depth):
        last = layer == depth - 1
        i = layer // 2
        mod_l = mod[layer]
        xt = _ffn(xt, mod_l, wg[layer, 0], wu[layer, 0], wd[layer, 0], base=0, n_tiles=n_all_tiles, mod_row=mod_row)
        n_out_tiles = n_lat_tiles if last else n_all_tiles
        if layer % 2 == 0:
            q, k, vt, y = _inproj_ab(xt, mod_l, ab_w_in[i].astype(BF16), ab_q_norm[i], ab_k_norm[i], cos_t, sin_t,
                                     n_tiles=n_all_tiles, mod_row=mod_row, rope_row=rope_row)
            a1 = _gqa(q, k, vt, bsz=bsz, seq=seq, lc=lc)
            a2 = _fourier(y, bsz=bsz, seq=seq, lc=lc)
            w_out = ab_w_out[i].astype(BF16)
        else:
            u, q, k, v = _inproj_cd(xt, mod_l, cd_w_in[i].astype(BF16), n_tiles=n_all_tiles, mod_row=mod_row)
            a1 = _conv_module(u, cd_dw_w[i], cd_dw_b[i], cd_ln_w[i], cd_ln_b[i], bsz=bsz, seq=seq)
            a2 = _na(q, k, v, cd_rpb[i], bsz=bsz, seq=seq, lc=lc)
            w_out = cd_w_out[i].astype(BF16)
        xt = _outproj(xt, mod_l, a1, a2, w_out, n_tiles=n_out_tiles, mod_row=mod_row)
        xt = _ffn(xt, mod_l, wg[layer, 1], wu[layer, 1], wd[layer, 1], base=6, n_tiles=n_out_tiles, mod_row=mod_row,
                  final_norm=final_norm if last else None)
    return xt.reshape(bsz, seq, d)
```

```python
import functools
import math

import numpy as np
import jax
import jax.numpy as jnp
from jax import lax
from jax.experimental import pallas as pl
from jax.experimental.pallas import tpu as pltpu

F32 = jnp.float32
BF16 = jnp.bfloat16

GRID_W = 64
HEAD_DIM = 64
A_Q_HEADS = 12
A_KV_HEADS = 4
B_GROUPS = 4
B_GROUP_DIM = 64
C_CHANNELS = 512
C_KERNEL = 31
D_HEADS = 8
NA_WIN_ROWS = 8
NA_WIN_COLS = 16
ROPE_THETA = 10000.0
NORM_EPS = 1e-6
N_MOD = 9
FFN_RES_WEIGHT = 0.5

A_Q_W = A_Q_HEADS * HEAD_DIM
A_KV_W = A_KV_HEADS * HEAD_DIM
B_W = B_GROUPS * B_GROUP_DIM
D_W = D_HEADS * HEAD_DIM

LANES = 128
SUBLANES = 8
MXU_DIM = 256

TM = 512
FFN_CHUNK = 256
ATT_TQ = 1024
ATT_TK = 512
NA_QROWS = 4
NA_LOOKAHEAD = 2
CONV_HALO = 16
ATT_QCHUNK = 256
ATT_LOOKAHEAD = 4
NEG_BIG = -1e30
Q_SCALE_LOG2 = (HEAD_DIM ** -0.5) * math.log2(math.e)

_MiB = 1 << 20


def _cparams(sem, vmem_mib):
    return pltpu.CompilerParams(dimension_semantics=sem, vmem_limit_bytes=vmem_mib * _MiB)


def _dot(a, b):
    return jnp.dot(a, b, preferred_element_type=F32)


def _dot_nt(a, b):
    return lax.dot_general(a, b, (((1,), (1,)), ((), ())), preferred_element_type=F32)


def _sigmoid(x):
    return 1.0 / (1.0 + jnp.exp(-x))


def _cos_sin(n):
    idx = np.arange(n)
    ang = 2.0 * np.pi * (np.outer(idx, idx) % n) / n
    return np.cos(ang), np.sin(ang)


def _const_bf16(a):
    return jnp.asarray(np.asarray(a, np.float32)).astype(BF16)


def _modulate(x, shift, scale):
    ms = jnp.mean(x * x, axis=-1, keepdims=True)
    return (x * lax.rsqrt(ms + NORM_EPS)) * (1.0 + scale) + shift


def _mod_kernel(ct_ref, w_ref, b_ref, o_ref, *, n_rows):
    ct = ct_ref[...]
    a = ct * _sigmoid(ct)
    w = w_ref[...]
    rows = [jnp.sum(w * a[:, r:r + 1], axis=0, keepdims=True) for r in range(n_rows)]
    rows.append(jnp.zeros((SUBLANES - n_rows, w.shape[1]), F32))
    o_ref[...] = jnp.concatenate(rows, axis=0) + b_ref[...]


def _mod_params(c, c_ctx, w_mod, b_mod):
    depth, d, nmd = w_mod.shape
    bsz = c.shape[0]
    n_rows = bsz + 1
    assert n_rows <= SUBLANES
    ct = jnp.concatenate([c, c_ctx[None], jnp.zeros((SUBLANES - n_rows, d), F32)], axis=0).T
    tn = d
    out = pl.pallas_call(
        functools.partial(_mod_kernel, n_rows=n_rows),
        grid=(depth, nmd // tn),
        in_specs=[
            pl.BlockSpec((d, SUBLANES), lambda l, n: (0, 0)),
            pl.BlockSpec((None, d, tn), lambda l, n: (l, 0, n)),
            pl.BlockSpec((None, 1, tn), lambda l, n: (l, 0, n)),
        ],
        out_specs=pl.BlockSpec((None, SUBLANES, tn), lambda l, n: (l, 0, n)),
        out_shape=jax.ShapeDtypeStruct((depth, SUBLANES, nmd), F32),
        compiler_params=_cparams(("arbitrary", "arbitrary"), 32),
        name="adaln_params",
    )(ct, w_mod, b_mod.reshape(depth, 1, nmd))
    return out.reshape(depth, SUBLANES, N_MOD, d)


def _pick_rows(lat_ref, ctx_ref, n_lat_tiles):
    return jnp.where(pl.program_id(0) == n_lat_tiles, ctx_ref[...], lat_ref[...])


def _ffn_kernel(x_ref, *rest, base, final, n_chunks, ctx_tile):
    rest = list(rest)
    xc_ref = rest.pop(0) if ctx_tile is not None else None
    mod_ref, wg_ref, wu_ref, wd_ref = rest[:4]
    if final:
        fn_ref, o_ref, acc_ref = rest[4:]
    else:
        o_ref, acc_ref = rest[4:]
    x = x_ref[...] if xc_ref is None else _pick_rows(x_ref, xc_ref, ctx_tile)
    m = mod_ref[...]
    h = _modulate(x, m[base:base + 1], m[base + 1:base + 2]).astype(BF16)
    for c in range(n_chunks):
        sl = slice(c * FFN_CHUNK, (c + 1) * FFN_CHUNK)
        g = _dot(h, wg_ref[:, sl])
        u = _dot(h, wu_ref[:, sl])
        a = ((g * _sigmoid(g)) * u).astype(BF16)
        part = _dot(a, wd_ref[sl, :])
        if c == 0:
            acc_ref[...] = part
        else:
            acc_ref[...] += part
    y = x + (FFN_RES_WEIGHT * m[base + 2:base + 3]) * acc_ref[...]
    if final:
        ms = jnp.mean(y * y, axis=-1, keepdims=True)
        y = (y * lax.rsqrt(ms + NORM_EPS)) * fn_ref[...]
    o_ref[...] = y


def _ffn(x, mod_l, wg, wu, wd, *, base, n_tiles, mod_row, final_norm=None, x_ctx=None):
    d = x.shape[1]
    dff = wg.shape[1]
    assert dff % FFN_CHUNK == 0
    final = final_norm is not None
    resident = dict(pipeline_mode=pl.Buffered(1))
    ctx_tile = None
    in_specs = [pl.BlockSpec((TM, d), lambda i: (i, 0))]
    args = [x]
    if x_ctx is not None:
        ctx_tile = n_tiles - 1
        assert x.shape[0] == ctx_tile * TM and x_ctx.shape == (TM, d)
        in_specs = [pl.BlockSpec((TM, d), lambda i: (jnp.minimum(i, ctx_tile - 1), 0)),
                    pl.BlockSpec((TM, d), lambda i: (0, 0))]
        args.append(x_ctx)
    in_specs += [
        pl.BlockSpec((None, N_MOD, d), lambda i: (mod_row(i), 0, 0)),
        pl.BlockSpec((d, dff), lambda i: (0, 0), **resident),
        pl.BlockSpec((d, dff), lambda i: (0, 0), **resident),
        pl.BlockSpec((dff, d), lambda i: (0, 0), **resident),
    ]
    args += [mod_l, wg, wu, wd]
    if final:
        in_specs.append(pl.BlockSpec((1, d), lambda i: (0, 0)))
        args.append(final_norm.reshape(1, d))
    return pl.pallas_call(
        functools.partial(_ffn_kernel, base=base, final=final, n_chunks=dff // FFN_CHUNK, ctx_tile=ctx_tile),
        grid=(n_tiles,),
        in_specs=in_specs,
        out_specs=pl.BlockSpec((TM, d), lambda i: (i, 0)),
        out_shape=jax.ShapeDtypeStruct((n_tiles * TM, d), F32),
        scratch_shapes=[pltpu.VMEM((TM, d), F32)],
        compiler_params=_cparams(("arbitrary",), 48),
        name="ffn_final" if final else "ffn",
    )(*args)


def _outproj_kernel(x_ref, mod_ref, w1_ref, w2_ref, a1_ref, a2_ref, *rest, ctx_tile):
    if ctx_tile is None:
        (o_ref,) = rest
        a1, a2 = a1_ref[...], a2_ref[...]
    else:
        a1c_ref, a2c_ref, o_ref = rest
        a1 = _pick_rows(a1_ref, a1c_ref, ctx_tile)
        a2 = _pick_rows(a2_ref, a2c_ref, ctx_tile)
    y = _dot(a1.astype(BF16), w1_ref[...]) + _dot(a2.astype(BF16), w2_ref[...])
    o_ref[...] = x_ref[...] + mod_ref[...][5:6] * y


def _outproj(x, mod_l, a1, a2, w_out, *, n_tiles, mod_row, a1_ctx=None, a2_ctx=None):
    d = x.shape[1]
    d1, d2 = a1.shape[1], a2.shape[1]
    w1, w2 = w_out[:d1], w_out[d1:]
    ctx_tile = None
    lat = lambda i: (i, 0)
    args = [x, mod_l, w1, w2, a1, a2]
    if a1_ctx is not None:
        ctx_tile = n_tiles - 1
        assert a1.shape[0] == ctx_tile * TM and a1_ctx.shape[0] == TM and a2_ctx.shape[0] == TM
        lat = lambda i: (jnp.minimum(i, ctx_tile - 1), 0)
        args += [a1_ctx, a2_ctx]
    in_specs = [
        pl.BlockSpec((TM, d), lambda i: (i, 0)),
        pl.BlockSpec((None, N_MOD, d), lambda i: (mod_row(i), 0, 0)),
        pl.BlockSpec((d1, d), lambda i: (0, 0)),
        pl.BlockSpec((d2, d), lambda i: (0, 0)),
        pl.BlockSpec((TM, d1), lat),
        pl.BlockSpec((TM, d2), lat),
    ]
    if ctx_tile is not None:
        in_specs += [pl.BlockSpec((TM, d1), lambda i: (0, 0)), pl.BlockSpec((TM, d2), lambda i: (0, 0))]
    return pl.pallas_call(
        functools.partial(_outproj_kernel, ctx_tile=ctx_tile),
        grid=(n_tiles,),
        in_specs=in_specs,
        out_specs=pl.BlockSpec((TM, d), lambda i: (i, 0)),
        out_shape=jax.ShapeDtypeStruct((n_tiles * TM, d), F32),
        compiler_params=_cparams(("arbitrary",), 32),
        name="out_proj",
    )(*args)


def _swap_halves(z, low_half):
    n = z.shape[1]
    return jnp.where(low_half, pltpu.roll(z, n - HEAD_DIM // 2, 1), pltpu.roll(z, HEAD_DIM // 2, 1))


def _head_sumsq(z, ones_bd):
    zz = z * z
    hi = zz.astype(BF16)
    lo = (zz - hi.astype(F32)).astype(BF16)
    return _dot(hi, ones_bd) + _dot(lo, ones_bd)


def _inproj_ab_kernel(x_ref, mod_ref, w_ref, qn_ref, kn_ref, cos_ref, sin_ref, ones_ref, dft_ref,
                      q_ref, k_ref, vt_ref, y_ref):
    x = x_ref[...]
    m = mod_ref[...]
    h = _modulate(x, m[3:4], m[4:5]).astype(BF16)
    y = _dot(h, w_ref[...])
    ones_bd = ones_ref[...]
    cos = jnp.concatenate([cos_ref[...]] * 2, axis=1)
    sin = jnp.concatenate([sin_ref[...]] * 2, axis=1)
    lane = lax.broadcasted_iota(jnp.int32, (1, MXU_DIM), 1)
    low_half = (lane % HEAD_DIM) < (HEAD_DIM // 2)

    def norm_rope(z, nw):
        ss = _head_sumsq(z, ones_bd)
        zn = (z * lax.rsqrt(ss * (1.0 / HEAD_DIM) + NORM_EPS)) * nw
        return zn * cos + _swap_halves(zn, low_half) * sin

    qn = qn_ref[...]
    kn = kn_ref[...]
    for blk in range(A_Q_W // MXU_DIM):
        sl = slice(blk * MXU_DIM, (blk + 1) * MXU_DIM)
        q_ref[:, sl] = (norm_rope(y[:, sl], qn) * Q_SCALE_LOG2).astype(BF16)
    k_ref[...] = norm_rope(y[:, A_Q_W:A_Q_W + A_KV_W], kn).astype(BF16)
    vt_ref[...] = _values_with_ones_t(y[:, A_Q_W + A_KV_W:A_Q_W + 2 * A_KV_W], A_KV_HEADS).astype(BF16)
    f = y[:, A_Q_W + 2 * A_KV_W:].astype(BF16)
    y_ref[...] = _dot(f, dft_ref[...])


def _inproj_ab(x, mod_l, w_in, q_norm, k_norm, cos_t, sin_t, *, n_tiles, mod_row, rope_row):
    ntok, d = x.shape
    n_in = w_in.shape[1]
    ones_bd = _const_bf16(np.kron(np.eye(MXU_DIM // HEAD_DIM), np.ones((HEAD_DIM, HEAD_DIM))))
    gc, gs = _cos_sin(B_GROUP_DIM)
    eye = np.eye(B_GROUPS)
    dft = _const_bf16(np.concatenate([np.kron(eye, gc), np.kron(eye, gs)], axis=1))
    qn = jnp.tile(q_norm, MXU_DIM // HEAD_DIM).reshape(1, MXU_DIM)
    kn = jnp.tile(k_norm, MXU_DIM // HEAD_DIM).reshape(1, MXU_DIM)
    const = lambda i: (0, 0)
    row = lambda i: (i, 0)
    return pl.pallas_call(
        _inproj_ab_kernel,
        grid=(n_tiles,),
        in_specs=[
            pl.BlockSpec((TM, d), row),
            pl.BlockSpec((None, N_MOD, d), lambda i: (mod_row(i), 0, 0)),
            pl.BlockSpec((d, n_in), const),
            pl.BlockSpec((1, MXU_DIM), const),
            pl.BlockSpec((1, MXU_DIM), const),
            pl.BlockSpec((TM, LANES), lambda i: (rope_row(i), 0)),
            pl.BlockSpec((TM, LANES), lambda i: (rope_row(i), 0)),
            pl.BlockSpec((MXU_DIM, MXU_DIM), const),
            pl.BlockSpec((B_W, 2 * B_W), const),
        ],
        out_specs=[
            pl.BlockSpec((TM, A_Q_W), row),
            pl.BlockSpec((TM, A_KV_W), row),
            pl.BlockSpec((A_KV_HEADS * LANES, TM), lambda i: (0, i)),
            pl.BlockSpec((TM, 2 * B_W), row),
        ],
        out_shape=[
            jax.ShapeDtypeStruct((ntok, A_Q_W), BF16),
            jax.ShapeDtypeStruct((ntok, A_KV_W), BF16),
            jax.ShapeDtypeStruct((A_KV_HEADS * LANES, ntok), BF16),
            jax.ShapeDtypeStruct((ntok, 2 * B_W), F32),
        ],
        compiler_params=_cparams(("arbitrary",), 40),
        name="in_proj_ab",
    )(x, mod_l, w_in, qn, kn, cos_t, sin_t, ones_bd, dft)


def _rope_tables(seq):
    t = jnp.arange(seq)
    row = (t // GRID_W).astype(F32)
    col = (t % GRID_W).astype(F32)
    n_ax = HEAD_DIM // 4
    inv = ROPE_THETA ** (-jnp.arange(n_ax, dtype=F32) / n_ax)
    ang = jnp.concatenate([row[:, None] * inv, col[:, None] * inv], axis=-1)
    cos, sin = jnp.cos(ang), jnp.sin(ang)
    cos_h = jnp.concatenate([cos, cos], axis=-1)
    sin_h = jnp.concatenate([-sin, sin], axis=-1)
    cos_t = jnp.concatenate([jnp.tile(cos_h, (1, 2)), jnp.ones((TM, LANES), F32)], axis=0)
    sin_t = jnp.concatenate([jnp.tile(sin_h, (1, 2)), jnp.zeros((TM, LANES), F32)], axis=0)
    return cos_t, sin_t


def _gqa_kernel(q_ref, kc_ref, vtc_ref, *rest, has_lat):
    if has_lat:
        kl_ref, vtl_ref, o_ref, qm_ref, m_ref, acc_ref = rest
    else:
        o_ref, qm_ref, m_ref, acc_ref = rest
    j = pl.program_id(2)
    group = A_Q_HEADS // A_KV_HEADS
    lane = lax.broadcasted_iota(jnp.int32, (1, LANES), 1)
    hi_half = lane >= HEAD_DIM
    n_chunks = q_ref.shape[0] // ATT_QCHUNK

    def update(k_ref, vt_ref):
        bodies = [(h, c) for h in range(A_Q_HEADS) for c in range(n_chunks)]

        def scores(h, c):
            kvp = (h // group) // 2
            kp = k_ref[:, kvp * LANES:(kvp + 1) * LANES]
            return _dot_nt(kp, qm_ref[h, c * ATT_QCHUNK:(c + 1) * ATT_QCHUNK, :])

        def accumulate(h, c, alpha, pv):
            cs = slice(c * ATT_QCHUNK, (c + 1) * ATT_QCHUNK)
            acc_ref[h, :, cs] = alpha * acc_ref[h, :, cs] + pv

        queue = [scores(*bodies[t]) for t in range(min(ATT_LOOKAHEAD, len(bodies)))]
        pending = None
        for idx, (h, c) in enumerate(bodies):
            s = queue.pop(0)
            if idx + ATT_LOOKAHEAD < len(bodies):
                queue.append(scores(*bodies[idx + ATT_LOOKAHEAD]))
            cs = slice(c * ATT_QCHUNK, (c + 1) * ATT_QCHUNK)
            kvh = h // group
            vt = vt_ref[kvh * LANES:(kvh + 1) * LANES, :]
            m_prev = m_ref[h, :, cs]
            m_new = jnp.maximum(m_prev, jnp.max(s, axis=0, keepdims=True))
            alpha = jnp.exp2(m_prev - m_new)
            p = jnp.exp2(s - m_new).astype(BF16)
            m_ref[h, :, cs] = m_new
            pv = _dot(vt, p)
            if pending is not None:
                accumulate(*pending)
            pending = (h, c, alpha, pv)
        accumulate(*pending)

    @pl.when(j == 0)
    def _first():
        m_ref[...] = jnp.full(m_ref.shape, NEG_BIG, F32)
        acc_ref[...] = jnp.zeros(acc_ref.shape, F32)
        for h in range(A_Q_HEADS):
            qp = q_ref[:, (h // 2) * LANES:(h // 2 + 1) * LANES].astype(F32)
            kv_half = (h // group) % 2
            if kv_half != h % 2:
                qp = pltpu.roll(qp, HEAD_DIM, 1)
            keep = hi_half if kv_half else jnp.logical_not(hi_half)
            qm_ref[h] = jnp.where(keep, qp, jnp.zeros_like(qp)).astype(BF16)
        update(kc_ref, vtc_ref)

    if has_lat:
        @pl.when(j > 0)
        def _rest():
            update(kl_ref, vtl_ref)

    @pl.when(j == pl.num_programs(2) - 1)
    def _final():
        for pb in range(A_Q_HEADS // 2):
            for c in range(n_chunks):
                cs = slice(c * ATT_QCHUNK, (c + 1) * ATT_QCHUNK)
                parts = []
                for h in (2 * pb, 2 * pb + 1):
                    a = acc_ref[h, :, cs]
                    parts.append(a[:HEAD_DIM] / a[HEAD_DIM:HEAD_DIM + 1])
                o_ref[cs, pb * LANES:(pb + 1) * LANES] = jnp.concatenate(parts, axis=0).T.astype(BF16)


def _gqa_scratch(tq):
    return [
        pltpu.VMEM((A_Q_HEADS, tq, LANES), BF16),
        pltpu.VMEM((A_Q_HEADS, 1, tq), F32),
        pltpu.VMEM((A_Q_HEADS, LANES, tq), F32),
    ]


def _gqa(q, k, vt, *, bsz, seq, lc):
    n_lat = bsz * seq
    tq, tk = ATT_TQ, ATT_TK
    ctx_blk = lambda b: n_lat // lc + b
    o = pl.pallas_call(
        functools.partial(_gqa_kernel, has_lat=True),
        grid=(bsz, seq // tq, 1 + seq // tk),
        in_specs=[
            pl.BlockSpec((tq, A_Q_W), lambda b, i, j: (b * (seq // tq) + i, 0)),
            pl.BlockSpec((lc, A_KV_W), lambda b, i, j: (ctx_blk(b), 0)),
            pl.BlockSpec((A_KV_HEADS * LANES, lc), lambda b, i, j: (0, ctx_blk(b))),
            pl.BlockSpec((tk, A_KV_W), lambda b, i, j: (b * (seq // tk) + jnp.maximum(j - 1, 0), 0)),
            pl.BlockSpec((A_KV_HEADS * LANES, tk), lambda b, i, j: (0, b * (seq // tk) + jnp.maximum(j - 1, 0))),
        ],
        out_specs=pl.BlockSpec((tq, A_Q_W), lambda b, i, j: (b * (seq // tq) + i, 0)),
        out_shape=jax.ShapeDtypeStruct((n_lat, A_Q_W), BF16),
        scratch_shapes=_gqa_scratch(tq),
        compiler_params=_cparams(("arbitrary", "arbitrary", "arbitrary"), 48),
        name="gqa_latent",
    )(q, k, vt, k, vt)
    o_ctx = pl.pallas_call(
        functools.partial(_gqa_kernel, has_lat=False),
        grid=(bsz, 1, 1),
        in_specs=[
            pl.BlockSpec((lc, A_Q_W), lambda b, i, j: (ctx_blk(b), 0)),
            pl.BlockSpec((lc, A_KV_W), lambda b, i, j: (ctx_blk(b), 0)),
            pl.BlockSpec((A_KV_HEADS * LANES, lc), lambda b, i, j: (0, ctx_blk(b))),
        ],
        out_specs=pl.BlockSpec((lc, A_Q_W), lambda b, i, j: (b, 0)),
        out_shape=jax.ShapeDtypeStruct((bsz * lc, A_Q_W), BF16),
        scratch_shapes=_gqa_scratch(lc),
        compiler_params=_cparams(("arbitrary", "arbitrary", "arbitrary"), 32),
        name="gqa_context",
    )(q, k, vt)
    return o, o_ctx


def _dft_a_kernel(y_ref, ca_ref, tc_ref, ts_ref, d_ref, *, nbb):
    ca = ca_ref[...]
    ra = ca.shape[1]
    for j in range(nbb):
        yj = y_ref[:, j, :].astype(BF16)
        p = _dot(ca, yj)
        br = p[:ra, :B_W] - p[ra:, B_W:]
        bi = -p[:ra, B_W:] - p[ra:, :B_W]
        tc = jnp.concatenate([tc_ref[j]] * (B_W // LANES), axis=1)
        ts = jnp.concatenate([ts_ref[j]] * (B_W // LANES), axis=1)
        d_ref[:, 0, j, :] = br * tc + bi * ts
        d_ref[:, 1, j, :] = bi * tc - br * ts


def _dft_b_kernel(d_ref, cb_ref, o_ref, *, kb, scale):
    cb = cb_ref[...]
    for j in range(kb):
        o_ref[:, j, :] = _dot(cb, d_ref[j].astype(BF16)) * scale


def _dft_ctx_kernel(y_ref, cs_ref, o_ref, *, scale):
    y = y_ref[...]
    cs = cs_ref[...]
    lc = cs.shape[0]
    yc = y[:, :B_W].astype(BF16)
    ys = y[:, B_W:].astype(BF16)
    o_ref[...] = (_dot(cs[:, :lc], yc) - _dot(cs[:, lc:], ys)) * scale


def _fourier(y, *, bsz, seq, lc):
    ntok = y.shape[0]
    ra = seq // GRID_W
    nb = GRID_W
    nbb = SUBLANES
    kb = SUBLANES
    assert ntok % nb == 0 and ra % SUBLANES == 0 and (ntok // nb) % 1 == 0
    ca_c, ca_s = _cos_sin(ra)
    ca = _const_bf16(np.concatenate([ca_c, ca_s], axis=0))
    th = 2.0 * np.pi * np.outer(np.arange(nb), np.arange(ra)) / seq
    tc = jnp.asarray(np.broadcast_to(np.cos(th)[:, :, None], (nb, ra, LANES)).astype(np.float32))
    ts = jnp.asarray(np.broadcast_to(np.sin(th)[:, :, None], (nb, ra, LANES)).astype(np.float32))
    cb_c, cb_s = _cos_sin(nb)
    cb = _const_bf16(np.concatenate([cb_c, cb_s], axis=1))

    y3 = y.reshape(ntok // nb, nb, 2 * B_W)
    d = pl.pallas_call(
        functools.partial(_dft_a_kernel, nbb=nbb),
        grid=(bsz, nb // nbb),
        in_specs=[
            pl.BlockSpec((ra, nbb, 2 * B_W), lambda b, i: (b, i, 0)),
            pl.BlockSpec((2 * ra, ra), lambda b, i: (0, 0)),
            pl.BlockSpec((nbb, ra, LANES), lambda b, i: (i, 0, 0)),
            pl.BlockSpec((nbb, ra, LANES), lambda b, i: (i, 0, 0)),
        ],
        out_specs=pl.BlockSpec((None, ra, 2, nbb, B_W), lambda b, i: (b, 0, 0, i, 0)),
        out_shape=jax.ShapeDtypeStruct((bsz, ra, 2, nb, B_W), F32),
        compiler_params=_cparams(("arbitrary", "arbitrary"), 32),
        name="fourier_rows",
    )(y3, ca, tc, ts)
    d = d.reshape(bsz, ra, 2 * nb, B_W)

    scale = 1.0 / math.sqrt(seq * B_GROUP_DIM)
    n_out = bsz * nb
    z3 = pl.pallas_call(
        functools.partial(_dft_b_kernel, kb=kb, scale=scale),
        grid=(bsz, ra // kb),
        in_specs=[
            pl.BlockSpec((None, kb, 2 * nb, B_W), lambda b, i: (b, i, 0, 0)),
            pl.BlockSpec((nb, 2 * nb), lambda b, i: (0, 0)),
        ],
        out_specs=pl.BlockSpec((nb, kb, B_W), lambda b, i: (b, i, 0)),
        out_shape=jax.ShapeDtypeStruct((n_out, ra, B_W), F32),
        compiler_params=_cparams(("arbitrary", "arbitrary"), 32),
        name="fourier_cols",
    )(d, cb)
    z = z3.reshape(bsz * seq, B_W)

    cc, cs_ = _cos_sin(lc)
    csm = _const_bf16(np.concatenate([cc, cs_], axis=1))
    ctx_blk = lambda b: (bsz * seq) // lc + b
    z_ctx = pl.pallas_call(
        functools.partial(_dft_ctx_kernel, scale=1.0 / math.sqrt(lc * B_GROUP_DIM)),
        grid=(bsz,),
        in_specs=[
            pl.BlockSpec((lc, 2 * B_W), lambda b: (ctx_blk(b), 0)),
            pl.BlockSpec((lc, 2 * lc), lambda b: (0, 0)),
        ],
        out_specs=pl.BlockSpec((lc, B_W), lambda b: (b, 0)),
        out_shape=jax.ShapeDtypeStruct((bsz * lc, B_W), F32),
        compiler_params=_cparams(("arbitrary",), 32),
        name="fourier_context",
    )(y, csm)
    return z, z_ctx


def _values_with_ones_t(v, n_heads):
    vt = v.T
    ones = jnp.ones((HEAD_DIM, vt.shape[1]), F32)
    groups = []
    for h in range(n_heads):
        groups += [vt[h * HEAD_DIM:(h + 1) * HEAD_DIM], ones]
    return jnp.concatenate(groups, axis=0)


def _inproj_cd_kernel(x_ref, mod_ref, w_ref, u_ref, q_ref, k_ref, vt_ref):
    x = x_ref[...]
    m = mod_ref[...]
    h = _modulate(x, m[3:4], m[4:5]).astype(BF16)
    y = _dot(h, w_ref[...])
    c = C_CHANNELS
    u_ref[...] = y[:, :c] * _sigmoid(y[:, c:2 * c])
    q_ref[...] = (y[:, 2 * c:2 * c + D_W] * Q_SCALE_LOG2).astype(BF16)
    k_ref[...] = y[:, 2 * c + D_W:2 * c + 2 * D_W].astype(BF16)
    vt_ref[...] = _values_with_ones_t(y[:, 2 * c + 2 * D_W:], D_HEADS).astype(BF16)


def _inproj_cd(x, mod_l, w_in, *, n_tiles, mod_row):
    ntok, d = x.shape
    n_in = w_in.shape[1]
    row = lambda i: (i, 0)
    return pl.pallas_call(
        _inproj_cd_kernel,
        grid=(n_tiles,),
        in_specs=[
            pl.BlockSpec((TM, d), row),
            pl.BlockSpec((None, N_MOD, d), lambda i: (mod_row(i), 0, 0)),
            pl.BlockSpec((d, n_in), lambda i: (0, 0)),
        ],
        out_specs=[
            pl.BlockSpec((TM, C_CHANNELS), row),
            pl.BlockSpec((TM, D_W), row),
            pl.BlockSpec((TM, D_W), row),
            pl.BlockSpec((D_HEADS * LANES, TM), lambda i: (0, i)),
        ],
        out_shape=[
            jax.ShapeDtypeStruct((ntok, C_CHANNELS), F32),
            jax.ShapeDtypeStruct((ntok, D_W), BF16),
            jax.ShapeDtypeStruct((ntok, D_W), BF16),
            jax.ShapeDtypeStruct((D_HEADS * LANES, ntok), BF16),
        ],
        compiler_params=_cparams(("arbitrary",), 40),
        name="in_proj_cd",
    )(x, mod_l, w_in)


def _conv_kernel(prev_ref, cur_ref, next_ref, w_ref, b_ref, lnw_ref, lnb_ref, o_ref, buf_ref, shift_ref, *,
                 tiles_per_seq):
    i = pl.program_id(1)
    halo = CONV_HALO
    tm = cur_ref.shape[0]
    zeros = jnp.zeros((halo, cur_ref.shape[1]), F32)
    buf_ref[0:halo, :] = jnp.where(i > 0, prev_ref[...], zeros)
    buf_ref[halo:halo + tm, :] = cur_ref[...]
    buf_ref[halo + tm:, :] = jnp.where(i < tiles_per_seq - 1, next_ref[...], zeros)
    w = w_ref[...]
    off = halo - C_KERNEL // 2
    y = None
    for r in range(SUBLANES):
        taps = [t for t in range(C_KERNEL) if (t + off) % SUBLANES == r]
        part = None
        for t in taps:
            start = t + off - r
            term = buf_ref[start:start + tm + SUBLANES, :] * w[t:t + 1]
            part = term if part is None else part + term
        if r == 0:
            shifted = part[:tm]
        else:
            shift_ref[...] = part
            shifted = shift_ref[r:r + tm, :]
        y = shifted if y is None else y + shifted
    y = y + b_ref[...]
    mu = jnp.mean(y, axis=-1, keepdims=True)
    yc = y - mu
    var = jnp.mean(yc * yc, axis=-1, keepdims=True)
    z = (yc * lax.rsqrt(var + NORM_EPS)) * lnw_ref[...] + lnb_ref[...]
    o_ref[...] = (z * _sigmoid(z)).astype(BF16)


def _conv_module(u, dw_w, dw_b, ln_w, ln_b, *, bsz, seq):
    c = u.shape[1]
    tiles = seq // TM
    hpt = TM // CONV_HALO
    n_halo_blocks = u.shape[0] // CONV_HALO
    cur = lambda b, i: (b * tiles + i, 0)
    prev = lambda b, i: (jnp.maximum((b * tiles + i) * hpt - 1, 0), 0)
    nxt = lambda b, i: (jnp.minimum((b * tiles + i + 1) * hpt, n_halo_blocks - 1), 0)
    vec = lambda b, i: (0, 0)
    return pl.pallas_call(
        functools.partial(_conv_kernel, tiles_per_seq=tiles),
        grid=(bsz, tiles),
        in_specs=[
            pl.BlockSpec((CONV_HALO, c), prev),
            pl.BlockSpec((TM, c), cur),
            pl.BlockSpec((CONV_HALO, c), nxt),
            pl.BlockSpec((C_KERNEL, c), vec),
            pl.BlockSpec((1, c), vec),
            pl.BlockSpec((1, c), vec),
            pl.BlockSpec((1, c), vec),
        ],
        out_specs=pl.BlockSpec((TM, c), cur),
        out_shape=jax.ShapeDtypeStruct((bsz * seq, c), BF16),
        scratch_shapes=[pltpu.VMEM((TM + 2 * CONV_HALO, c), F32), pltpu.VMEM((TM + SUBLANES, c), F32)],
        compiler_params=_cparams(("arbitrary", "arbitrary"), 32),
        name="conv_module",
    )(u, u, u, dw_w, dw_b.reshape(1, c), ln_w.reshape(1, c), ln_b.reshape(1, c))


def _na_kernel(q_ref, k0_ref, k1_ref, k2_ref, vt0_ref, vt1_ref, vt2_ref, kc_ref, vtc_ref, bias_ref, o_ref):
    lane = lax.broadcasted_iota(jnp.int32, (1, LANES), 1)
    hi_half = lane >= HEAD_DIM

    def scores(h):
        sl = slice((h // 2) * LANES, (h // 2 + 1) * LANES)
        qp = q_ref[:, sl]
        keep = hi_half if h % 2 else jnp.logical_not(hi_half)
        qm = jnp.where(keep, qp, jnp.zeros_like(qp))
        kn = jnp.concatenate([k0_ref[:, sl], k1_ref[:, sl], k2_ref[:, sl]], axis=0)
        return _dot_nt(kn, qm) + bias_ref[h], _dot_nt(kc_ref[:, sl], qm)

    lookahead = min(NA_LOOKAHEAD, D_HEADS)
    queue = [scores(h) for h in range(lookahead)]
    outs = []
    for h in range(D_HEADS):
        s_nb, s_cx = queue.pop(0)
        if h + lookahead < D_HEADS:
            queue.append(scores(h + lookahead))
        mx = jnp.maximum(jnp.max(s_nb, axis=0, keepdims=True), jnp.max(s_cx, axis=0, keepdims=True))
        p = jnp.concatenate([jnp.exp2(s_nb - mx).astype(BF16), jnp.exp2(s_cx - mx).astype(BF16)], axis=0)
        rs = slice(h * LANES, (h + 1) * LANES)
        vt = jnp.concatenate([vt0_ref[rs, :], vt1_ref[rs, :], vt2_ref[rs, :], vtc_ref[rs, :]], axis=1)
        outs.append(_dot(vt, p))
    for pb in range(D_HEADS // 2):
        parts = [outs[h][:HEAD_DIM] / outs[h][HEAD_DIM:HEAD_DIM + 1] for h in (2 * pb, 2 * pb + 1)]
        o_ref[:, pb * LANES:(pb + 1) * LANES] = jnp.concatenate(parts, axis=0).T.astype(BF16)


def _na_bias_kernel(rpb_ref, o_ref, *, rows):
    h = pl.program_id(0)
    n_ro, n_co = 2 * NA_WIN_ROWS - 1, 2 * NA_WIN_COLS - 1
    kc = lax.broadcasted_iota(jnp.int32, (GRID_W, LANES), 0)
    lane = lax.broadcasted_iota(jnp.int32, (GRID_W, LANES), 1)
    qc = lane % GRID_W
    d = kc - qc + (NA_WIN_COLS - 1)
    cs = jnp.clip(qc - NA_WIN_COLS // 2, 0, GRID_W - NA_WIN_COLS)
    col_ok = (kc >= cs) & (kc < cs + NA_WIN_COLS)
    neg = jnp.full((GRID_W, LANES), NEG_BIG, F32)
    log2e = math.log2(math.e)
    band = []
    for ro in range(n_ro):
        acc = neg
        for co in range(n_co):
            acc = jnp.where(d == co, rpb_ref[(h * n_ro + ro) * n_co + co] * log2e, acc)
        band.append(jnp.where(col_ok, acc, neg))
    qr = NA_QROWS
    for kind, r0 in enumerate((0, qr, rows - qr)):
        for b in range(3 * qr):
            rk = r0 - qr + b
            for pair in range(qr // 2):
                halves = []
                for a in (2 * pair, 2 * pair + 1):
                    r = r0 + a
                    start = min(max(r - NA_WIN_ROWS // 2, 0), rows - NA_WIN_ROWS)
                    ok = start <= rk < start + NA_WIN_ROWS and 0 <= rk < rows
                    halves.append(band[rk - r + NA_WIN_ROWS - 1] if ok else neg)
                o_ref[kind, b * GRID_W:(b + 1) * GRID_W, pair * LANES:(pair + 1) * LANES] = jnp.where(
                    lane < GRID_W, halves[0], halves[1])


def _na_bias(rpb, rows):
    heads = rpb.shape[0]
    tq = NA_QROWS * GRID_W
    return pl.pallas_call(
        functools.partial(_na_bias_kernel, rows=rows),
        grid=(heads,),
        in_specs=[pl.BlockSpec(memory_space=pltpu.SMEM)],
        out_specs=pl.BlockSpec((3, None, 3 * tq, tq), lambda h: (0, h, 0, 0)),
        out_shape=jax.ShapeDtypeStruct((3, heads, 3 * tq, tq), F32),
        compiler_params=_cparams(("arbitrary",), 32),
        name="na_bias",
    )(rpb.reshape(-1))


def _na(q, k, vt, rpb, *, bsz, seq, lc):
    rows = seq // GRID_W
    tq = NA_QROWS * GRID_W
    nblk = seq // tq
    assert rows >= NA_WIN_ROWS and rows % NA_QROWS == 0 and NA_WIN_ROWS == 2 * NA_QROWS and nblk >= 3
    assert lc == tq
    bias = _na_bias(rpb, rows)
    n_lat = bsz * seq
    qmap = lambda b, i: (b * nblk + i, 0)

    def near(off, transposed):
        def index(b, i):
            blk = b * nblk + jnp.clip(i + off, 0, nblk - 1)
            return (0, blk) if transposed else (blk, 0)
        return index

    kind = lambda b, i: (jnp.where(i == 0, 0, jnp.where(i == nblk - 1, 2, 1)), 0, 0, 0)
    rows_blk = lambda m: pl.BlockSpec((tq, D_W), m)
    cols_blk = lambda m: pl.BlockSpec((D_HEADS * LANES, tq), m)
    return pl.pallas_call(
        _na_kernel,
        grid=(bsz, nblk),
        in_specs=[
            rows_blk(qmap),
            rows_blk(near(-1, False)), rows_blk(near(0, False)), rows_blk(near(1, False)),
            cols_blk(near(-1, True)), cols_blk(near(0, True)), cols_blk(near(1, True)),
            rows_blk(lambda b, i: (n_lat // lc + b, 0)),
            cols_blk(lambda b, i: (0, n_lat // lc + b)),
            pl.BlockSpec((None, D_HEADS, 3 * tq, tq), kind),
        ],
        out_specs=rows_blk(qmap),
        out_shape=jax.ShapeDtypeStruct((n_lat, D_W), BF16),
        compiler_params=_cparams(("arbitrary", "arbitrary"), 48),
        name="neighbourhood_attention",
    )(q, k, k, k, vt, vt, vt, k, vt, bias)


def kernel(x, c, ctx, c_ctx, w_mod, b_mod, ffn_w_gate, ffn_w_up, ffn_w_down, ab_w_in, ab_w_out, ab_q_norm,
           ab_k_norm, cd_w_in, cd_w_out, cd_dw_w, cd_dw_b, cd_ln_w, cd_ln_b, cd_rpb, final_norm):
    bsz, seq, d = x.shape
    lc = ctx.shape[1]
    depth = w_mod.shape[0]
    assert depth == 2, "layer 0 (A/B mixer) updates the context, layer 1 (C/D mixer) is the last layer"
    assert bsz * lc == TM and seq % TM == 0 and seq % ATT_TQ == 0 and seq % ATT_TK == 0
    n_lat_tiles = bsz * seq // TM
    n_all_tiles = n_lat_tiles + 1
    tiles_per_seq = seq // TM
    mod_row = lambda i: jnp.where(i == n_lat_tiles, bsz, i // tiles_per_seq)
    rope_row = lambda i: jnp.where(i == n_lat_tiles, tiles_per_seq, i % tiles_per_seq)

    mod = _mod_params(c, c_ctx, w_mod, b_mod)
    cos_t, sin_t = _rope_tables(seq)

    def ffn_weights(layer, half):
        return (ffn_w_gate[layer, half].astype(BF16), ffn_w_up[layer, half].astype(BF16),
                ffn_w_down[layer, half].astype(BF16))

    xt = x.reshape(bsz * seq, d)
    for layer in range(depth):
        last = layer == depth - 1
        i = layer // 2
        mod_l = mod[layer]
        xt = _ffn(xt, mod_l, *ffn_weights(layer, 0), base=0, n_tiles=n_all_tiles, mod_row=mod_row,
                  x_ctx=ctx.reshape(bsz * lc, d) if layer == 0 else None)
        n_out_tiles = n_lat_tiles if last else n_all_tiles
        if layer % 2 == 0:
            q, k, vt, y = _inproj_ab(xt, mod_l, ab_w_in[i].astype(BF16), ab_q_norm[i], ab_k_norm[i], cos_t, sin_t,
                                     n_tiles=n_all_tiles, mod_row=mod_row, rope_row=rope_row)
            a1, a1_ctx = _gqa(q, k, vt, bsz=bsz, seq=seq, lc=lc)
            a2, a2_ctx = _fourier(y, bsz=bsz, seq=seq, lc=lc)
            xt = _outproj(xt, mod_l, a1, a2, ab_w_out[i].astype(BF16), n_tiles=n_out_tiles, mod_row=mod_row,
                          a1_ctx=a1_ctx, a2_ctx=a2_ctx)
        else:
            u, q, k, vt = _inproj_cd(xt, mod_l, cd_w_in[i].astype(BF16), n_tiles=n_all_tiles, mod_row=mod_row)
            a1 = _conv_module(u, cd_dw_w[i], cd_dw_b[i], cd_ln_w[i], cd_ln_b[i], bsz=bsz, seq=seq)
            a2 = _na(q, k, vt, cd_rpb[i], bsz=bsz, seq=seq, lc=lc)
            xt = _outproj(xt, mod_l, a1, a2, cd_w_out[i].astype(BF16), n_tiles=n_out_tiles, mod_row=mod_row)
        xt = _ffn(xt, mod_l, *ffn_weights(layer, 1), base=6, n_tiles=n_out_tiles, mod_row=mod_row,
                  final_norm=final_norm if last else None)
    return xt.reshape(bsz, seq, d)
```

```python
import functools
import math

import numpy as np
import jax
import jax.numpy as jnp
from jax import lax
from jax.experimental import pallas as pl
from jax.experimental.pallas import tpu as pltpu

F32 = jnp.float32
BF16 = jnp.bfloat16

GRID_W = 64
HEAD_DIM = 64
A_Q_HEADS = 12
A_KV_HEADS = 4
B_GROUPS = 4
B_GROUP_DIM = 64
C_CHANNELS = 512
C_KERNEL = 31
D_HEADS = 8
NA_WIN_ROWS = 8
NA_WIN_COLS = 16
ROPE_THETA = 10000.0
NORM_EPS = 1e-6
N_MOD = 9
FFN_RES_WEIGHT = 0.5

A_Q_W = A_Q_HEADS * HEAD_DIM
A_KV_W = A_KV_HEADS * HEAD_DIM
B_W = B_GROUPS * B_GROUP_DIM
D_W = D_HEADS * HEAD_DIM

LANES = 128
SUBLANES = 8
MXU_DIM = 256

TM = 512
FFN_CHUNK = 256
ATT_TQ = 1024
ATT_TK = 512
NA_QROWS = 4
NA_LOOKAHEAD = 3
CONV_HALO = 16
ATT_QCHUNK = 256
ATT_LOOKAHEAD = 4
SAFE_SCORE_BOUND_LOG2 = 100.0
ONES_ROWS = 16
V_GROUP = HEAD_DIM + ONES_ROWS
NEG_BIG = -1e30
Q_SCALE_LOG2 = (HEAD_DIM ** -0.5) * math.log2(math.e)

_MiB = 1 << 20


def _cparams(sem, vmem_mib):
    return pltpu.CompilerParams(dimension_semantics=sem, vmem_limit_bytes=vmem_mib * _MiB)


def _dot(a, b):
    return jnp.dot(a, b, preferred_element_type=F32)


def _dot_nt(a, b):
    return lax.dot_general(a, b, (((1,), (1,)), ((), ())), preferred_element_type=F32)


def _sigmoid(x):
    return 1.0 / (1.0 + jnp.exp(-x))


def _cos_sin(n):
    idx = np.arange(n)
    ang = 2.0 * np.pi * (np.outer(idx, idx) % n) / n
    return np.cos(ang), np.sin(ang)


def _const_bf16(a):
    return jnp.asarray(np.asarray(a, np.float32)).astype(BF16)


def _modulate(x, shift, scale):
    ms = jnp.mean(x * x, axis=-1, keepdims=True)
    return (x * lax.rsqrt(ms + NORM_EPS)) * (1.0 + scale) + shift


def _mod_kernel(ct_ref, w_ref, b_ref, o_ref, *, n_rows):
    ct = ct_ref[...]
    a = ct * _sigmoid(ct)
    w = w_ref[...]
    rows = [jnp.sum(w * a[:, r:r + 1], axis=0, keepdims=True) for r in range(n_rows)]
    rows.append(jnp.zeros((SUBLANES - n_rows, w.shape[1]), F32))
    o_ref[...] = jnp.concatenate(rows, axis=0) + b_ref[...]


def _mod_params(c, c_ctx, w_mod, b_mod):
    depth, d, nmd = w_mod.shape
    bsz = c.shape[0]
    n_rows = bsz + 1
    assert n_rows <= SUBLANES
    ct = jnp.concatenate([c, c_ctx[None], jnp.zeros((SUBLANES - n_rows, d), F32)], axis=0).T
    tn = d
    out = pl.pallas_call(
        functools.partial(_mod_kernel, n_rows=n_rows),
        grid=(depth, nmd // tn),
        in_specs=[
            pl.BlockSpec((d, SUBLANES), lambda l, n: (0, 0)),
            pl.BlockSpec((None, d, tn), lambda l, n: (l, 0, n)),
            pl.BlockSpec((None, 1, tn), lambda l, n: (l, 0, n)),
        ],
        out_specs=pl.BlockSpec((None, SUBLANES, tn), lambda l, n: (l, 0, n)),
        out_shape=jax.ShapeDtypeStruct((depth, SUBLANES, nmd), F32),
        compiler_params=_cparams(("arbitrary", "arbitrary"), 32),
        name="adaln_params",
    )(ct, w_mod, b_mod.reshape(depth, 1, nmd))
    return out.reshape(depth, SUBLANES, N_MOD, d)


def _pick_rows(lat_ref, ctx_ref, n_lat_tiles):
    return jnp.where(pl.program_id(0) == n_lat_tiles, ctx_ref[...], lat_ref[...])


def _ffn_kernel(*refs, base, final, n_chunks, ctx_tile, x_has_ctx, mixer):
    refs = list(refs)
    x_ref = refs.pop(0)
    x = x_ref[...] if not x_has_ctx else _pick_rows(x_ref, refs.pop(0), ctx_tile)
    if mixer:
        a1_ref, a2_ref = refs.pop(0), refs.pop(0)
        if ctx_tile is None:
            a1, a2 = a1_ref[...], a2_ref[...]
        else:
            a1 = _pick_rows(a1_ref, refs.pop(0), ctx_tile)
            a2 = _pick_rows(a2_ref, refs.pop(0), ctx_tile)
        w1_ref, w2_ref = refs.pop(0), refs.pop(0)
    mod_ref, wg_ref, wu_ref, wd_ref = refs[:4]
    if final:
        fn_ref, o_ref, acc_ref = refs[4:]
    else:
        o_ref, acc_ref = refs[4:]
    m = mod_ref[...]
    if mixer:
        x = x + m[5:6] * (_dot(a1.astype(BF16), w1_ref[...]) + _dot(a2.astype(BF16), w2_ref[...]))
    h = _modulate(x, m[base:base + 1], m[base + 1:base + 2]).astype(BF16)
    for c in range(n_chunks):
        sl = slice(c * FFN_CHUNK, (c + 1) * FFN_CHUNK)
        g = _dot(h, wg_ref[:, sl])
        u = _dot(h, wu_ref[:, sl])
        a = ((g * _sigmoid(g)) * u).astype(BF16)
        part = _dot(a, wd_ref[sl, :])
        if c == 0:
            acc_ref[...] = part
        else:
            acc_ref[...] += part
    y = x + (FFN_RES_WEIGHT * m[base + 2:base + 3]) * acc_ref[...]
    if final:
        ms = jnp.mean(y * y, axis=-1, keepdims=True)
        y = (y * lax.rsqrt(ms + NORM_EPS)) * fn_ref[...]
    o_ref[...] = y


def _ffn(x, mod_l, ffn_w, layer, half, *, base, n_tiles, mod_row, final_norm=None, x_ctx=None, mixer=None):
    d = x.shape[1]
    wg, wu, wd = ffn_w
    dff = wg.shape[-1]
    assert dff % FFN_CHUNK == 0
    final = final_norm is not None
    resident = dict(pipeline_mode=pl.Buffered(1))
    const = lambda i: (0, 0)
    last_lat = n_tiles - 2
    lat_only = lambda i: (jnp.minimum(i, last_lat), 0)
    ctx_tile = None
    in_specs = [pl.BlockSpec((TM, d), lambda i: (i, 0))]
    args = [x]
    if x_ctx is not None:
        ctx_tile = n_tiles - 1
        assert x.shape[0] == ctx_tile * TM and x_ctx.shape == (TM, d)
        in_specs = [pl.BlockSpec((TM, d), lat_only), pl.BlockSpec((TM, d), const)]
        args.append(x_ctx)
    if mixer is not None:
        a1, a2, w_out, a1_ctx, a2_ctx = mixer
        d1, d2 = a1.shape[1], a2.shape[1]
        if a1_ctx is None:
            in_specs += [pl.BlockSpec((TM, d1), lambda i: (i, 0)), pl.BlockSpec((TM, d2), lambda i: (i, 0))]
            args += [a1, a2]
        else:
            ctx_tile = n_tiles - 1
            assert a1.shape[0] == ctx_tile * TM and a1_ctx.shape[0] == TM and a2_ctx.shape[0] == TM
            in_specs += [pl.BlockSpec((TM, d1), lat_only), pl.BlockSpec((TM, d2), lat_only),
                         pl.BlockSpec((TM, d1), const), pl.BlockSpec((TM, d2), const)]
            args += [a1, a2, a1_ctx, a2_ctx]
        in_specs += [pl.BlockSpec((d1, d), const, **resident), pl.BlockSpec((d2, d), lambda i: (d1 // d2, 0), **resident)]
        assert d1 % d2 == 0
        args += [w_out, w_out]
    sel = lambda i: (layer, half, 0, 0)
    in_specs += [
        pl.BlockSpec((None, N_MOD, d), lambda i: (mod_row(i), 0, 0)),
        pl.BlockSpec((None, None, d, dff), sel, **resident),
        pl.BlockSpec((None, None, d, dff), sel, **resident),
        pl.BlockSpec((None, None, dff, d), sel, **resident),
    ]
    args += [mod_l, wg, wu, wd]
    if final:
        in_specs.append(pl.BlockSpec((1, d), const))
        args.append(final_norm.reshape(1, d))
    return pl.pallas_call(
        functools.partial(_ffn_kernel, base=base, final=final, n_chunks=dff // FFN_CHUNK, ctx_tile=ctx_tile,
                          x_has_ctx=x_ctx is not None, mixer=mixer is not None),
        grid=(n_tiles,),
        in_specs=in_specs,
        out_specs=pl.BlockSpec((TM, d), lambda i: (i, 0)),
        out_shape=jax.ShapeDtypeStruct((n_tiles * TM, d), F32),
        scratch_shapes=[pltpu.VMEM((TM, d), F32)],
        compiler_params=_cparams(("arbitrary",), 52),
        name=("mix_ffn_final" if final else "mix_ffn") if mixer is not None else "ffn",
    )(*args)


def _swap_halves(z, low_half):
    n = z.shape[1]
    return jnp.where(low_half, pltpu.roll(z, n - HEAD_DIM // 2, 1), pltpu.roll(z, HEAD_DIM // 2, 1))


def _head_sumsq(z, ones_bd):
    zz = z * z
    hi = zz.astype(BF16)
    lo = (zz - hi.astype(F32)).astype(BF16)
    return _dot(hi, ones_bd) + _dot(lo, ones_bd)


def _inproj_ab_kernel(x_ref, mod_ref, w_ref, qn_ref, kn_ref, cos_ref, sin_ref, ones_ref, dft_ref,
                      q_ref, k_ref, vt_ref, y_ref):
    x = x_ref[...]
    m = mod_ref[...]
    h = _modulate(x, m[3:4], m[4:5]).astype(BF16)
    y = _dot(h, w_ref[...])
    ones_bd = ones_ref[...]
    cos = jnp.concatenate([cos_ref[...]] * 2, axis=1)
    sin = jnp.concatenate([sin_ref[...]] * 2, axis=1)
    lane = lax.broadcasted_iota(jnp.int32, (1, MXU_DIM), 1)
    low_half = (lane % HEAD_DIM) < (HEAD_DIM // 2)

    def norm_rope(z, nw):
        ss = _head_sumsq(z, ones_bd)
        zn = (z * lax.rsqrt(ss * (1.0 / HEAD_DIM) + NORM_EPS)) * nw
        return zn * cos + _swap_halves(zn, low_half) * sin

    qn = qn_ref[...]
    kn = kn_ref[...]
    for blk in range(A_Q_W // MXU_DIM):
        sl = slice(blk * MXU_DIM, (blk + 1) * MXU_DIM)
        q_ref[:, sl] = (norm_rope(y[:, sl], qn) * Q_SCALE_LOG2).astype(BF16)
    k_ref[...] = norm_rope(y[:, A_Q_W:A_Q_W + A_KV_W], kn).astype(BF16)
    vt_ref[...] = _values_with_ones_t(y[:, A_Q_W + A_KV_W:A_Q_W + 2 * A_KV_W], A_KV_HEADS).astype(BF16)
    f = y[:, A_Q_W + 2 * A_KV_W:].astype(BF16)
    y_ref[...] = _dot(f, dft_ref[...])


def _inproj_ab(x, mod_l, w_in, q_norm, k_norm, cos_t, sin_t, *, n_tiles, mod_row, rope_row):
    ntok, d = x.shape
    n_in = w_in.shape[1]
    ones_bd = _const_bf16(np.kron(np.eye(MXU_DIM // HEAD_DIM), np.ones((HEAD_DIM, HEAD_DIM))))
    gc, gs = _cos_sin(B_GROUP_DIM)
    eye = np.eye(B_GROUPS)
    dft = _const_bf16(np.concatenate([np.kron(eye, gc), np.kron(eye, gs)], axis=1))
    qn = jnp.tile(q_norm, MXU_DIM // HEAD_DIM).reshape(1, MXU_DIM)
    kn = jnp.tile(k_norm, MXU_DIM // HEAD_DIM).reshape(1, MXU_DIM)
    const = lambda i: (0, 0)
    row = lambda i: (i, 0)
    return pl.pallas_call(
        _inproj_ab_kernel,
        grid=(n_tiles,),
        in_specs=[
            pl.BlockSpec((TM, d), row),
            pl.BlockSpec((None, N_MOD, d), lambda i: (mod_row(i), 0, 0)),
            pl.BlockSpec((d, n_in), const),
            pl.BlockSpec((1, MXU_DIM), const),
            pl.BlockSpec((1, MXU_DIM), const),
            pl.BlockSpec((TM, LANES), lambda i: (rope_row(i), 0)),
            pl.BlockSpec((TM, LANES), lambda i: (rope_row(i), 0)),
            pl.BlockSpec((MXU_DIM, MXU_DIM), const),
            pl.BlockSpec((B_W, 2 * B_W), const),
        ],
        out_specs=[
            pl.BlockSpec((TM, A_Q_W), row),
            pl.BlockSpec((TM, A_KV_W), row),
            pl.BlockSpec((A_KV_HEADS * V_GROUP, TM), lambda i: (0, i)),
            pl.BlockSpec((TM, 2 * B_W), row),
        ],
        out_shape=[
            jax.ShapeDtypeStruct((ntok, A_Q_W), BF16),
            jax.ShapeDtypeStruct((ntok, A_KV_W), BF16),
            jax.ShapeDtypeStruct((A_KV_HEADS * V_GROUP, ntok), BF16),
            jax.ShapeDtypeStruct((ntok, 2 * B_W), F32),
        ],
        compiler_params=_cparams(("arbitrary",), 40),
        name="in_proj_ab",
    )(x, mod_l, w_in, qn, kn, cos_t, sin_t, ones_bd, dft)


def _rope_tables(seq):
    t = jnp.arange(seq)
    row = (t // GRID_W).astype(F32)
    col = (t % GRID_W).astype(F32)
    n_ax = HEAD_DIM // 4
    inv = ROPE_THETA ** (-jnp.arange(n_ax, dtype=F32) / n_ax)
    ang = jnp.concatenate([row[:, None] * inv, col[:, None] * inv], axis=-1)
    cos, sin = jnp.cos(ang), jnp.sin(ang)
    cos_h = jnp.concatenate([cos, cos], axis=-1)
    sin_h = jnp.concatenate([-sin, sin], axis=-1)
    cos_t = jnp.concatenate([jnp.tile(cos_h, (1, 2)), jnp.ones((TM, LANES), F32)], axis=0)
    sin_t = jnp.concatenate([jnp.tile(sin_h, (1, 2)), jnp.zeros((TM, LANES), F32)], axis=0)
    return cos_t, sin_t


def _gqa_kernel(bound_ref, q_ref, kc_ref, vtc_ref, *rest, has_lat):
    if has_lat:
        kl_ref, vtl_ref, o_ref, qm_ref, m_ref, acc_ref = rest
    else:
        o_ref, qm_ref, m_ref, acc_ref = rest
    j = pl.program_id(2)
    group = A_Q_HEADS // A_KV_HEADS
    lane = lax.broadcasted_iota(jnp.int32, (1, LANES), 1)
    hi_half = lane >= HEAD_DIM
    n_chunks = q_ref.shape[0] // ATT_QCHUNK
    no_shift = bound_ref[0] <= SAFE_SCORE_BOUND_LOG2

    def update(k_ref, vt_ref, shifted):
        bodies = [(h, c) for h in range(A_Q_HEADS) for c in range(n_chunks)]

        def scores(h, c):
            kvp = (h // group) // 2
            kp = k_ref[:, kvp * LANES:(kvp + 1) * LANES]
            return _dot_nt(kp, qm_ref[h, c * ATT_QCHUNK:(c + 1) * ATT_QCHUNK, :])

        def accumulate(h, c, alpha, pv):
            cs = slice(c * ATT_QCHUNK, (c + 1) * ATT_QCHUNK)
            prev = acc_ref[h, :, cs] if alpha is None else alpha * acc_ref[h, :, cs]
            acc_ref[h, :, cs] = prev + pv

        queue = [scores(*bodies[t]) for t in range(min(ATT_LOOKAHEAD, len(bodies)))]
        pending = None
        for idx, (h, c) in enumerate(bodies):
            s = queue.pop(0)
            if idx + ATT_LOOKAHEAD < len(bodies):
                queue.append(scores(*bodies[idx + ATT_LOOKAHEAD]))
            cs = slice(c * ATT_QCHUNK, (c + 1) * ATT_QCHUNK)
            kvh = h // group
            vt = vt_ref[kvh * V_GROUP:(kvh + 1) * V_GROUP, :]
            if shifted:
                m_prev = m_ref[h, :, cs]
                m_new = jnp.maximum(m_prev, jnp.max(s, axis=0, keepdims=True))
                alpha = jnp.exp2(m_prev - m_new)
                p = jnp.exp2(s - m_new).astype(BF16)
                m_ref[h, :, cs] = m_new
            else:
                alpha = None
                p = jnp.exp2(s).astype(BF16)
            pv = _dot(vt, p)
            if pending is not None:
                accumulate(*pending)
            pending = (h, c, alpha, pv)
        accumulate(*pending)

    def update_either(k_ref, vt_ref):
        @pl.when(no_shift)
        def _plain():
            update(k_ref, vt_ref, shifted=False)

        @pl.when(jnp.logical_not(no_shift))
        def _running_max():
            update(k_ref, vt_ref, shifted=True)

    @pl.when(j == 0)
    def _first():
        m_ref[...] = jnp.full(m_ref.shape, NEG_BIG, F32)
        acc_ref[...] = jnp.zeros(acc_ref.shape, F32)
        for h in range(A_Q_HEADS):
            qp = q_ref[:, (h // 2) * LANES:(h // 2 + 1) * LANES].astype(F32)
            kv_half = (h // group) % 2
            if kv_half != h % 2:
                qp = pltpu.roll(qp, HEAD_DIM, 1)
            keep = hi_half if kv_half else jnp.logical_not(hi_half)
            qm_ref[h] = jnp.where(keep, qp, jnp.zeros_like(qp)).astype(BF16)
        update_either(kc_ref, vtc_ref)

    if has_lat:
        @pl.when(j > 0)
        def _rest():
            update_either(kl_ref, vtl_ref)

    @pl.when(j == pl.num_programs(2) - 1)
    def _final():
        for pb in range(A_Q_HEADS // 2):
            for c in range(n_chunks):
                cs = slice(c * ATT_QCHUNK, (c + 1) * ATT_QCHUNK)
                parts = []
                for h in (2 * pb, 2 * pb + 1):
                    a = acc_ref[h, :, cs]
                    parts.append(a[:HEAD_DIM] / a[HEAD_DIM:HEAD_DIM + 1])
                o_ref[cs, pb * LANES:(pb + 1) * LANES] = jnp.concatenate(parts, axis=0).T.astype(BF16)


def _gqa_scratch(tq):
    return [
        pltpu.VMEM((A_Q_HEADS, tq, LANES), BF16),
        pltpu.VMEM((A_Q_HEADS, 1, tq), F32),
        pltpu.VMEM((A_Q_HEADS, V_GROUP, tq), F32),
    ]


def _gqa(q, k, vt, q_norm, k_norm, *, bsz, seq, lc):
    n_lat = bsz * seq
    tq, tk = ATT_TQ, ATT_TK
    ctx_blk = lambda b: n_lat // lc + b
    bound = (1.05 * HEAD_DIM * Q_SCALE_LOG2 * jnp.max(jnp.abs(q_norm)) * jnp.max(jnp.abs(k_norm))).reshape(1)
    smem = pl.BlockSpec(memory_space=pltpu.SMEM)
    o = pl.pallas_call(
        functools.partial(_gqa_kernel, has_lat=True),
        grid=(bsz, seq // tq, 1 + seq // tk),
        in_specs=[
            smem,
            pl.BlockSpec((tq, A_Q_W), lambda b, i, j: (b * (seq // tq) + i, 0)),
            pl.BlockSpec((lc, A_KV_W), lambda b, i, j: (ctx_blk(b), 0)),
            pl.BlockSpec((A_KV_HEADS * V_GROUP, lc), lambda b, i, j: (0, ctx_blk(b))),
            pl.BlockSpec((tk, A_KV_W), lambda b, i, j: (b * (seq // tk) + jnp.maximum(j - 1, 0), 0)),
            pl.BlockSpec((A_KV_HEADS * V_GROUP, tk), lambda b, i, j: (0, b * (seq // tk) + jnp.maximum(j - 1, 0))),
        ],
        out_specs=pl.BlockSpec((tq, A_Q_W), lambda b, i, j: (b * (seq // tq) + i, 0)),
        out_shape=jax.ShapeDtypeStruct((n_lat, A_Q_W), BF16),
        scratch_shapes=_gqa_scratch(tq),
        compiler_params=_cparams(("arbitrary", "arbitrary", "arbitrary"), 48),
        name="gqa_latent",
    )(bound, q, k, vt, k, vt)
    o_ctx = pl.pallas_call(
        functools.partial(_gqa_kernel, has_lat=False),
        grid=(bsz, 1, 1),
        in_specs=[
            smem,
            pl.BlockSpec((lc, A_Q_W), lambda b, i, j: (ctx_blk(b), 0)),
            pl.BlockSpec((lc, A_KV_W), lambda b, i, j: (ctx_blk(b), 0)),
            pl.BlockSpec((A_KV_HEADS * V_GROUP, lc), lambda b, i, j: (0, ctx_blk(b))),
        ],
        out_specs=pl.BlockSpec((lc, A_Q_W), lambda b, i, j: (b, 0)),
        out_shape=jax.ShapeDtypeStruct((bsz * lc, A_Q_W), BF16),
        scratch_shapes=_gqa_scratch(lc),
        compiler_params=_cparams(("arbitrary", "arbitrary", "arbitrary"), 32),
        name="gqa_context",
    )(bound, q, k, vt)
    return o, o_ctx


def _dft_a_kernel(y_ref, ca_ref, tc_ref, ts_ref, d_ref, *, nbb):
    ca = ca_ref[...]
    ra = ca.shape[1]
    for j in range(nbb):
        yj = y_ref[:, j, :].astype(BF16)
        p = _dot(ca, yj)
        br = p[:ra, :B_W] - p[ra:, B_W:]
        bi = -p[:ra, B_W:] - p[ra:, :B_W]
        tc = jnp.concatenate([tc_ref[j]] * (B_W // LANES), axis=1)
        ts = jnp.concatenate([ts_ref[j]] * (B_W // LANES), axis=1)
        d_ref[:, 0, j, :] = br * tc + bi * ts
        d_ref[:, 1, j, :] = bi * tc - br * ts


def _dft_b_kernel(d_ref, cb_ref, o_ref, *, kb, scale):
    cb = cb_ref[...]
    for j in range(kb):
        o_ref[:, j, :] = _dot(cb, d_ref[j].astype(BF16)) * scale


def _dft_ctx_kernel(y_ref, cs_ref, o_ref, *, scale):
    y = y_ref[...]
    cs = cs_ref[...]
    lc = cs.shape[0]
    yc = y[:, :B_W].astype(BF16)
    ys = y[:, B_W:].astype(BF16)
    o_ref[...] = (_dot(cs[:, :lc], yc) - _dot(cs[:, lc:], ys)) * scale


def _fourier(y, *, bsz, seq, lc):
    ntok = y.shape[0]
    ra = seq // GRID_W
    nb = GRID_W
    nbb = SUBLANES
    kb = SUBLANES
    assert ntok % nb == 0 and ra % SUBLANES == 0 and (ntok // nb) % 1 == 0
    ca_c, ca_s = _cos_sin(ra)
    ca = _const_bf16(np.concatenate([ca_c, ca_s], axis=0))
    th = 2.0 * np.pi * np.outer(np.arange(nb), np.arange(ra)) / seq
    tc = jnp.asarray(np.broadcast_to(np.cos(th)[:, :, None], (nb, ra, LANES)).astype(np.float32))
    ts = jnp.asarray(np.broadcast_to(np.sin(th)[:, :, None], (nb, ra, LANES)).astype(np.float32))
    cb_c, cb_s = _cos_sin(nb)
    cb = _const_bf16(np.concatenate([cb_c, cb_s], axis=1))

    y3 = y.reshape(ntok // nb, nb, 2 * B_W)
    d = pl.pallas_call(
        functools.partial(_dft_a_kernel, nbb=nbb),
        grid=(bsz, nb // nbb),
        in_specs=[
            pl.BlockSpec((ra, nbb, 2 * B_W), lambda b, i: (b, i, 0)),
            pl.BlockSpec((2 * ra, ra), lambda b, i: (0, 0)),
            pl.BlockSpec((nbb, ra, LANES), lambda b, i: (i, 0, 0)),
            pl.BlockSpec((nbb, ra, LANES), lambda b, i: (i, 0, 0)),
        ],
        out_specs=pl.BlockSpec((None, ra, 2, nbb, B_W), lambda b, i: (b, 0, 0, i, 0)),
        out_shape=jax.ShapeDtypeStruct((bsz, ra, 2, nb, B_W), F32),
        compiler_params=_cparams(("arbitrary", "arbitrary"), 32),
        name="fourier_rows",
    )(y3, ca, tc, ts)
    d = d.reshape(bsz, ra, 2 * nb, B_W)

    scale = 1.0 / math.sqrt(seq * B_GROUP_DIM)
    n_out = bsz * nb
    z3 = pl.pallas_call(
        functools.partial(_dft_b_kernel, kb=kb, scale=scale),
        grid=(bsz, ra // kb),
        in_specs=[
            pl.BlockSpec((None, kb, 2 * nb, B_W), lambda b, i: (b, i, 0, 0)),
            pl.BlockSpec((nb, 2 * nb), lambda b, i: (0, 0)),
        ],
        out_specs=pl.BlockSpec((nb, kb, B_W), lambda b, i: (b, i, 0)),
        out_shape=jax.ShapeDtypeStruct((n_out, ra, B_W), F32),
        compiler_params=_cparams(("arbitrary", "arbitrary"), 32),
        name="fourier_cols",
    )(d, cb)
    z = z3.reshape(bsz * seq, B_W)

    cc, cs_ = _cos_sin(lc)
    csm = _const_bf16(np.concatenate([cc, cs_], axis=1))
    ctx_blk = lambda b: (bsz * seq) // lc + b
    z_ctx = pl.pallas_call(
        functools.partial(_dft_ctx_kernel, scale=1.0 / math.sqrt(lc * B_GROUP_DIM)),
        grid=(bsz,),
        in_specs=[
            pl.BlockSpec((lc, 2 * B_W), lambda b: (ctx_blk(b), 0)),
            pl.BlockSpec((lc, 2 * lc), lambda b: (0, 0)),
        ],
        out_specs=pl.BlockSpec((lc, B_W), lambda b: (b, 0)),
        out_shape=jax.ShapeDtypeStruct((bsz * lc, B_W), F32),
        compiler_params=_cparams(("arbitrary",), 32),
        name="fourier_context",
    )(y, csm)
    return z, z_ctx


def _values_with_ones_t(v, n_heads):
    vt = v.T
    ones = jnp.ones((ONES_ROWS, vt.shape[1]), F32)
    groups = []
    for h in range(n_heads):
        groups += [vt[h * HEAD_DIM:(h + 1) * HEAD_DIM], ones]
    return jnp.concatenate(groups, axis=0)


def _inproj_cd_kernel(x_ref, mod_ref, w_ref, u_ref, q_ref, k_ref, vt_ref):
    x = x_ref[...]
    m = mod_ref[...]
    h = _modulate(x, m[3:4], m[4:5]).astype(BF16)
    y = _dot(h, w_ref[...])
    c = C_CHANNELS
    u_ref[...] = y[:, :c] * _sigmoid(y[:, c:2 * c])
    q_ref[...] = (y[:, 2 * c:2 * c + D_W] * Q_SCALE_LOG2).astype(BF16)
    k_ref[...] = y[:, 2 * c + D_W:2 * c + 2 * D_W].astype(BF16)
    vt_ref[...] = _values_with_ones_t(y[:, 2 * c + 2 * D_W:], D_HEADS).astype(BF16)


def _inproj_cd(x, mod_l, w_in, *, n_tiles, mod_row):
    ntok, d = x.shape
    n_in = w_in.shape[1]
    row = lambda i: (i, 0)
    return pl.pallas_call(
        _inproj_cd_kernel,
        grid=(n_tiles,),
        in_specs=[
            pl.BlockSpec((TM, d), row),
            pl.BlockSpec((None, N_MOD, d), lambda i: (mod_row(i), 0, 0)),
            pl.BlockSpec((d, n_in), lambda i: (0, 0)),
        ],
        out_specs=[
            pl.BlockSpec((TM, C_CHANNELS), row),
            pl.BlockSpec((TM, D_W), row),
            pl.BlockSpec((TM, D_W), row),
            pl.BlockSpec((D_HEADS * V_GROUP, TM), lambda i: (0, i)),
        ],
        out_shape=[
            jax.ShapeDtypeStruct((ntok, C_CHANNELS), F32),
            jax.ShapeDtypeStruct((ntok, D_W), BF16),
            jax.ShapeDtypeStruct((ntok, D_W), BF16),
            jax.ShapeDtypeStruct((D_HEADS * V_GROUP, ntok), BF16),
        ],
        compiler_params=_cparams(("arbitrary",), 40),
        name="in_proj_cd",
    )(x, mod_l, w_in)


def _conv_kernel(prev_ref, cur_ref, next_ref, w_ref, b_ref, lnw_ref, lnb_ref, o_ref, buf_ref, shift_ref, *,
                 tiles_per_seq):
    i = pl.program_id(1)
    halo = CONV_HALO
    tm = cur_ref.shape[0]
    zeros = jnp.zeros((halo, cur_ref.shape[1]), F32)
    buf_ref[0:halo, :] = jnp.where(i > 0, prev_ref[...], zeros)
    buf_ref[halo:halo + tm, :] = cur_ref[...]
    buf_ref[halo + tm:, :] = jnp.where(i < tiles_per_seq - 1, next_ref[...], zeros)
    w = w_ref[...]
    off = halo - C_KERNEL // 2
    y = None
    for r in range(SUBLANES):
        taps = [t for t in range(C_KERNEL) if (t + off) % SUBLANES == r]
        part = None
        for t in taps:
            start = t + off - r
            term = buf_ref[start:start + tm + SUBLANES, :] * w[t:t + 1]
            part = term if part is None else part + term
        if r == 0:
            shifted = part[:tm]
        else:
            shift_ref[...] = part
            shifted = shift_ref[r:r + tm, :]
        y = shifted if y is None else y + shifted
    y = y + b_ref[...]
    mu = jnp.mean(y, axis=-1, keepdims=True)
    yc = y - mu
    var = jnp.mean(yc * yc, axis=-1, keepdims=True)
    z = (yc * lax.rsqrt(var + NORM_EPS)) * lnw_ref[...] + lnb_ref[...]
    o_ref[...] = (z * _sigmoid(z)).astype(BF16)


def _conv_module(u, dw_w, dw_b, ln_w, ln_b, *, bsz, seq):
    c = u.shape[1]
    tiles = seq // TM
    hpt = TM // CONV_HALO
    n_halo_blocks = u.shape[0] // CONV_HALO
    cur = lambda b, i: (b * tiles + i, 0)
    prev = lambda b, i: (jnp.maximum((b * tiles + i) * hpt - 1, 0), 0)
    nxt = lambda b, i: (jnp.minimum((b * tiles + i + 1) * hpt, n_halo_blocks - 1), 0)
    vec = lambda b, i: (0, 0)
    return pl.pallas_call(
        functools.partial(_conv_kernel, tiles_per_seq=tiles),
        grid=(bsz, tiles),
        in_specs=[
            pl.BlockSpec((CONV_HALO, c), prev),
            pl.BlockSpec((TM, c), cur),
            pl.BlockSpec((CONV_HALO, c), nxt),
            pl.BlockSpec((C_KERNEL, c), vec),
            pl.BlockSpec((1, c), vec),
            pl.BlockSpec((1, c), vec),
            pl.BlockSpec((1, c), vec),
        ],
        out_specs=pl.BlockSpec((TM, c), cur),
        out_shape=jax.ShapeDtypeStruct((bsz * seq, c), BF16),
        scratch_shapes=[pltpu.VMEM((TM + 2 * CONV_HALO, c), F32), pltpu.VMEM((TM + SUBLANES, c), F32)],
        compiler_params=_cparams(("arbitrary", "arbitrary"), 32),
        name="conv_module",
    )(u, u, u, dw_w, dw_b.reshape(1, c), ln_w.reshape(1, c), ln_b.reshape(1, c))


def _na_kernel(q_ref, k0_ref, k1_ref, k2_ref, vt0_ref, vt1_ref, vt2_ref, kc_ref, vtc_ref, bias_ref, o_ref):
    lane = lax.broadcasted_iota(jnp.int32, (1, LANES), 1)
    hi_half = lane >= HEAD_DIM

    def scores(h):
        sl = slice((h // 2) * LANES, (h // 2 + 1) * LANES)
        qp = q_ref[:, sl]
        keep = hi_half if h % 2 else jnp.logical_not(hi_half)
        qm = jnp.where(keep, qp, jnp.zeros_like(qp))
        kn = jnp.concatenate([k0_ref[:, sl], k1_ref[:, sl], k2_ref[:, sl]], axis=0)
        return _dot_nt(kn, qm) + bias_ref[h], _dot_nt(kc_ref[:, sl], qm)

    lookahead = min(NA_LOOKAHEAD, D_HEADS)
    queue = [scores(h) for h in range(lookahead)]
    outs = []
    for h in range(D_HEADS):
        s_nb, s_cx = queue.pop(0)
        if h + lookahead < D_HEADS:
            queue.append(scores(h + lookahead))
        mx = jnp.maximum(jnp.max(s_nb, axis=0, keepdims=True), jnp.max(s_cx, axis=0, keepdims=True))
        p = jnp.concatenate([jnp.exp2(s_nb - mx).astype(BF16), jnp.exp2(s_cx - mx).astype(BF16)], axis=0)
        rs = slice(h * V_GROUP, (h + 1) * V_GROUP)
        vt = jnp.concatenate([vt0_ref[rs, :], vt1_ref[rs, :], vt2_ref[rs, :], vtc_ref[rs, :]], axis=1)
        outs.append(_dot(vt, p))
    for pb in range(D_HEADS // 2):
        parts = [outs[h][:HEAD_DIM] / outs[h][HEAD_DIM:HEAD_DIM + 1] for h in (2 * pb, 2 * pb + 1)]
        o_ref[:, pb * LANES:(pb + 1) * LANES] = jnp.concatenate(parts, axis=0).T.astype(BF16)


def _na_bias_kernel(rpb_ref, o_ref, *, rows):
    h = pl.program_id(0)
    n_ro, n_co = 2 * NA_WIN_ROWS - 1, 2 * NA_WIN_COLS - 1
    kc = lax.broadcasted_iota(jnp.int32, (GRID_W, LANES), 0)
    lane = lax.broadcasted_iota(jnp.int32, (GRID_W, LANES), 1)
    qc = lane % GRID_W
    d = kc - qc + (NA_WIN_COLS - 1)
    cs = jnp.clip(qc - NA_WIN_COLS // 2, 0, GRID_W - NA_WIN_COLS)
    col_ok = (kc >= cs) & (kc < cs + NA_WIN_COLS)
    neg = jnp.full((GRID_W, LANES), NEG_BIG, F32)
    log2e = math.log2(math.e)
    band = []
    for ro in range(n_ro):
        acc = neg
        for co in range(n_co):
            acc = jnp.where(d == co, rpb_ref[(h * n_ro + ro) * n_co + co] * log2e, acc)
        band.append(jnp.where(col_ok, acc, neg))
    qr = NA_QROWS
    for kind, r0 in enumerate((0, qr, rows - qr)):
        for b in range(3 * qr):
            rk = r0 - qr + b
            for pair in range(qr // 2):
                halves = []
                for a in (2 * pair, 2 * pair + 1):
                    r = r0 + a
                    start = min(max(r - NA_WIN_ROWS // 2, 0), rows - NA_WIN_ROWS)
                    ok = start <= rk < start + NA_WIN_ROWS and 0 <= rk < rows
                    halves.append(band[rk - r + NA_WIN_ROWS - 1] if ok else neg)
                o_ref[kind, b * GRID_W:(b + 1) * GRID_W, pair * LANES:(pair + 1) * LANES] = jnp.where(
                    lane < GRID_W, halves[0], halves[1])


def _na_bias(rpb, rows):
    heads = rpb.shape[0]
    tq = NA_QROWS * GRID_W
    return pl.pallas_call(
        functools.partial(_na_bias_kernel, rows=rows),
        grid=(heads,),
        in_specs=[pl.BlockSpec(memory_space=pltpu.SMEM)],
        out_specs=pl.BlockSpec((3, None, 3 * tq, tq), lambda h: (0, h, 0, 0)),
        out_shape=jax.ShapeDtypeStruct((3, heads, 3 * tq, tq), F32),
        compiler_params=_cparams(("arbitrary",), 32),
        name="na_bias",
    )(rpb.reshape(-1))


def _na(q, k, vt, rpb, *, bsz, seq, lc):
    rows = seq // GRID_W
    tq = NA_QROWS * GRID_W
    nblk = seq // tq
    assert rows >= NA_WIN_ROWS and rows % NA_QROWS == 0 and NA_WIN_ROWS == 2 * NA_QROWS and nblk >= 3
    assert lc == tq
    bias = _na_bias(rpb, rows)
    n_lat = bsz * seq
    qmap = lambda b, i: (b * nblk + i, 0)

    def near(off, transposed):
        def index(b, i):
            blk = b * nblk + jnp.clip(i + off, 0, nblk - 1)
            return (0, blk) if transposed else (blk, 0)
        return index

    kind = lambda b, i: (jnp.where(i == 0, 0, jnp.where(i == nblk - 1, 2, 1)), 0, 0, 0)
    rows_blk = lambda m: pl.BlockSpec((tq, D_W), m)
    cols_blk = lambda m: pl.BlockSpec((D_HEADS * V_GROUP, tq), m)
    return pl.pallas_call(
        _na_kernel,
        grid=(bsz, nblk),
        in_specs=[
            rows_blk(qmap),
            rows_blk(near(-1, False)), rows_blk(near(0, False)), rows_blk(near(1, False)),
            cols_blk(near(-1, True)), cols_blk(near(0, True)), cols_blk(near(1, True)),
            rows_blk(lambda b, i: (n_lat // lc + b, 0)),
            cols_blk(lambda b, i: (0, n_lat // lc + b)),
            pl.BlockSpec((None, D_HEADS, 3 * tq, tq), kind),
        ],
        out_specs=rows_blk(qmap),
        out_shape=jax.ShapeDtypeStruct((n_lat, D_W), BF16),
        compiler_params=_cparams(("arbitrary", "arbitrary"), 48),
        name="neighbourhood_attention",
    )(q, k, k, k, vt, vt, vt, k, vt, bias)


def kernel(x, c, ctx, c_ctx, w_mod, b_mod, ffn_w_gate, ffn_w_up, ffn_w_down, ab_w_in, ab_w_out, ab_q_norm,
           ab_k_norm, cd_w_in, cd_w_out, cd_dw_w, cd_dw_b, cd_ln_w, cd_ln_b, cd_rpb, final_norm):
    bsz, seq, d = x.shape
    lc = ctx.shape[1]
    depth = w_mod.shape[0]
    assert depth == 2, "layer 0 (A/B mixer) updates the context, layer 1 (C/D mixer) is the last layer"
    assert bsz * lc == TM and seq % TM == 0 and seq % ATT_TQ == 0 and seq % ATT_TK == 0
    n_lat_tiles = bsz * seq // TM
    n_all_tiles = n_lat_tiles + 1
    tiles_per_seq = seq // TM
    mod_row = lambda i: jnp.where(i == n_lat_tiles, bsz, i // tiles_per_seq)
    rope_row = lambda i: jnp.where(i == n_lat_tiles, tiles_per_seq, i % tiles_per_seq)

    mod = _mod_params(c, c_ctx, w_mod, b_mod)
    cos_t, sin_t = _rope_tables(seq)

    ffn_w = (ffn_w_gate.astype(BF16), ffn_w_up.astype(BF16), ffn_w_down.astype(BF16))

    xt = x.reshape(bsz * seq, d)
    for layer in range(depth):
        last = layer == depth - 1
        i = layer // 2
        mod_l = mod[layer]
        xt = _ffn(xt, mod_l, ffn_w, layer, 0, base=0, n_tiles=n_all_tiles, mod_row=mod_row,
                  x_ctx=ctx.reshape(bsz * lc, d) if layer == 0 else None)
        if layer % 2 == 0:
            q, k, vt, y = _inproj_ab(xt, mod_l, ab_w_in[i].astype(BF16), ab_q_norm[i], ab_k_norm[i], cos_t, sin_t,
                                     n_tiles=n_all_tiles, mod_row=mod_row, rope_row=rope_row)
            a1, a1_ctx = _gqa(q, k, vt, ab_q_norm[i], ab_k_norm[i], bsz=bsz, seq=seq, lc=lc)
            a2, a2_ctx = _fourier(y, bsz=bsz, seq=seq, lc=lc)
            mixer = (a1, a2, ab_w_out[i].astype(BF16), a1_ctx, a2_ctx)
        else:
            u, q, k, vt = _inproj_cd(xt, mod_l, cd_w_in[i].astype(BF16), n_tiles=n_all_tiles, mod_row=mod_row)
            a1 = _conv_module(u, cd_dw_w[i], cd_dw_b[i], cd_ln_w[i], cd_ln_b[i], bsz=bsz, seq=seq)
            a2 = _na(q, k, vt, cd_rpb[i], bsz=bsz, seq=seq, lc=lc)
            mixer = (a1, a2, cd_w_out[i].astype(BF16), None, None)
        assert last == (mixer[3] is None)
        xt = _ffn(xt, mod_l, ffn_w, layer, 1, base=6, n_tiles=n_lat_tiles if last else n_all_tiles, mod_row=mod_row,
                  final_norm=final_norm if last else None, mixer=mixer)
    return xt.reshape(bsz, seq, d)
```

```python
import functools
import math

import numpy as np
import jax
import jax.numpy as jnp
from jax import lax
from jax.experimental import pallas as pl
from jax.experimental.pallas import tpu as pltpu

F32 = jnp.float32
BF16 = jnp.bfloat16

GRID_W = 64
HEAD_DIM = 64
A_Q_HEADS = 12
A_KV_HEADS = 4
B_GROUPS = 4
B_GROUP_DIM = 64
C_CHANNELS = 512
C_KERNEL = 31
D_HEADS = 8
NA_WIN_ROWS = 8
NA_WIN_COLS = 16
ROPE_THETA = 10000.0
NORM_EPS = 1e-6
N_MOD = 9
FFN_RES_WEIGHT = 0.5

A_Q_W = A_Q_HEADS * HEAD_DIM
A_KV_W = A_KV_HEADS * HEAD_DIM
B_W = B_GROUPS * B_GROUP_DIM
D_W = D_HEADS * HEAD_DIM

LANES = 128
SUBLANES = 8
MXU_DIM = 256

TM = 512
FFN_CHUNK = 256
ATT_TQ = 1024
ATT_TK = 1024
ATT_KBODY = 512
NA_QROWS = 4
NA_LOOKAHEAD = 3
CONV_HALO = 16
ATT_QCHUNK = 256
ATT_LOOKAHEAD = 4
SAFE_SCORE_BOUND_LOG2 = 100.0
ONES_ROWS = 16
V_GROUP = HEAD_DIM + ONES_ROWS
NEG_BIG = -1e30
Q_SCALE_LOG2 = (HEAD_DIM ** -0.5) * math.log2(math.e)

_MiB = 1 << 20


def _cparams(sem, vmem_mib):
    return pltpu.CompilerParams(dimension_semantics=sem, vmem_limit_bytes=vmem_mib * _MiB)


def _dot(a, b):
    return jnp.dot(a, b, preferred_element_type=F32)


def _dot_nt(a, b):
    return lax.dot_general(a, b, (((1,), (1,)), ((), ())), preferred_element_type=F32)


def _sigmoid(x):
    return 1.0 / (1.0 + jnp.exp(-x))


def _cos_sin(n):
    idx = np.arange(n)
    ang = 2.0 * np.pi * (np.outer(idx, idx) % n) / n
    return np.cos(ang), np.sin(ang)


def _const_bf16(a):
    return jnp.asarray(np.asarray(a, np.float32)).astype(BF16)


def _modulate(x, shift, scale):
    ms = jnp.mean(x * x, axis=-1, keepdims=True)
    return (x * lax.rsqrt(ms + NORM_EPS)) * (1.0 + scale) + shift


def _mod_kernel(ct_ref, w_ref, b_ref, o_ref, *, n_rows):
    ct = ct_ref[...]
    a = ct * _sigmoid(ct)
    w = w_ref[...]
    rows = [jnp.sum(w * a[:, r:r + 1], axis=0, keepdims=True) for r in range(n_rows)]
    rows.append(jnp.zeros((SUBLANES - n_rows, w.shape[1]), F32))
    o_ref[...] = jnp.concatenate(rows, axis=0) + b_ref[...]


def _mod_params(c, c_ctx, w_mod, b_mod):
    depth, d, nmd = w_mod.shape
    bsz = c.shape[0]
    n_rows = bsz + 1
    assert n_rows <= SUBLANES
    ct = jnp.concatenate([c, c_ctx[None], jnp.zeros((SUBLANES - n_rows, d), F32)], axis=0).T
    tn = d
    out = pl.pallas_call(
        functools.partial(_mod_kernel, n_rows=n_rows),
        grid=(depth, nmd // tn),
        in_specs=[
            pl.BlockSpec((d, SUBLANES), lambda l, n: (0, 0)),
            pl.BlockSpec((None, d, tn), lambda l, n: (l, 0, n)),
            pl.BlockSpec((None, 1, tn), lambda l, n: (l, 0, n)),
        ],
        out_specs=pl.BlockSpec((None, SUBLANES, tn), lambda l, n: (l, 0, n)),
        out_shape=jax.ShapeDtypeStruct((depth, SUBLANES, nmd), F32),
        compiler_params=_cparams(("arbitrary", "arbitrary"), 32),
        name="adaln_params",
    )(ct, w_mod, b_mod.reshape(depth, 1, nmd))
    return out.reshape(depth, SUBLANES, N_MOD, d)


def _pick_rows(lat_ref, ctx_ref, n_lat_tiles):
    return jnp.where(pl.program_id(0) == n_lat_tiles, ctx_ref[...], lat_ref[...])


def _ffn_kernel(*refs, base, final, n_chunks, ctx_tile, x_has_ctx, mixer):
    refs = list(refs)
    x_ref = refs.pop(0)
    x = x_ref[...] if not x_has_ctx else _pick_rows(x_ref, refs.pop(0), ctx_tile)
    if mixer:
        a1_ref, a2_ref = refs.pop(0), refs.pop(0)
        if ctx_tile is None:
            a1, a2 = a1_ref[...], a2_ref[...]
        else:
            a1 = _pick_rows(a1_ref, refs.pop(0), ctx_tile)
            a2 = _pick_rows(a2_ref, refs.pop(0), ctx_tile)
        w1_ref, w2_ref = refs.pop(0), refs.pop(0)
    mod_ref, wg_ref, wu_ref, wd_ref = refs[:4]
    if final:
        fn_ref, o_ref, acc_ref = refs[4:]
    else:
        o_ref, acc_ref = refs[4:]
    m = mod_ref[...]
    if mixer:
        x = x + m[5:6] * (_dot(a1.astype(BF16), w1_ref[...]) + _dot(a2.astype(BF16), w2_ref[...]))
    h = _modulate(x, m[base:base + 1], m[base + 1:base + 2]).astype(BF16)
    for c in range(n_chunks):
        sl = slice(c * FFN_CHUNK, (c + 1) * FFN_CHUNK)
        g = _dot(h, wg_ref[:, sl])
        u = _dot(h, wu_ref[:, sl])
        a = ((g * _sigmoid(g)) * u).astype(BF16)
        part = _dot(a, wd_ref[sl, :])
        if c == 0:
            acc_ref[...] = part
        else:
            acc_ref[...] += part
    y = x + (FFN_RES_WEIGHT * m[base + 2:base + 3]) * acc_ref[...]
    if final:
        ms = jnp.mean(y * y, axis=-1, keepdims=True)
        y = (y * lax.rsqrt(ms + NORM_EPS)) * fn_ref[...]
    o_ref[...] = y


def _ffn(x, mod_l, ffn_w, layer, half, *, base, n_tiles, mod_row, final_norm=None, x_ctx=None, mixer=None):
    d = x.shape[1]
    wg, wu, wd = ffn_w
    dff = wg.shape[-1]
    assert dff % FFN_CHUNK == 0
    final = final_norm is not None
    resident = dict(pipeline_mode=pl.Buffered(1))
    const = lambda i: (0, 0)
    last_lat = n_tiles - 2
    lat_only = lambda i: (jnp.minimum(i, last_lat), 0)
    ctx_tile = None
    in_specs = [pl.BlockSpec((TM, d), lambda i: (i, 0))]
    args = [x]
    if x_ctx is not None:
        ctx_tile = n_tiles - 1
        assert x.shape[0] == ctx_tile * TM and x_ctx.shape == (TM, d)
        in_specs = [pl.BlockSpec((TM, d), lat_only), pl.BlockSpec((TM, d), const)]
        args.append(x_ctx)
    if mixer is not None:
        a1, a2, w_out, a1_ctx, a2_ctx = mixer
        d1, d2 = a1.shape[1], a2.shape[1]
        if a1_ctx is None:
            in_specs += [pl.BlockSpec((TM, d1), lambda i: (i, 0)), pl.BlockSpec((TM, d2), lambda i: (i, 0))]
            args += [a1, a2]
        else:
            ctx_tile = n_tiles - 1
            assert a1.shape[0] == ctx_tile * TM and a1_ctx.shape[0] == TM and a2_ctx.shape[0] == TM
            in_specs += [pl.BlockSpec((TM, d1), lat_only), pl.BlockSpec((TM, d2), lat_only),
                         pl.BlockSpec((TM, d1), const), pl.BlockSpec((TM, d2), const)]
            args += [a1, a2, a1_ctx, a2_ctx]
        in_specs += [pl.BlockSpec((d1, d), const, **resident), pl.BlockSpec((d2, d), lambda i: (d1 // d2, 0), **resident)]
        assert d1 % d2 == 0
        args += [w_out, w_out]
    sel = lambda i: (layer, half, 0, 0)
    in_specs += [
        pl.BlockSpec((None, N_MOD, d), lambda i: (mod_row(i), 0, 0)),
        pl.BlockSpec((None, None, d, dff), sel, **resident),
        pl.BlockSpec((None, None, d, dff), sel, **resident),
        pl.BlockSpec((None, None, dff, d), sel, **resident),
    ]
    args += [mod_l, wg, wu, wd]
    if final:
        in_specs.append(pl.BlockSpec((1, d), const))
        args.append(final_norm.reshape(1, d))
    return pl.pallas_call(
        functools.partial(_ffn_kernel, base=base, final=final, n_chunks=dff // FFN_CHUNK, ctx_tile=ctx_tile,
                          x_has_ctx=x_ctx is not None, mixer=mixer is not None),
        grid=(n_tiles,),
        in_specs=in_specs,
        out_specs=pl.BlockSpec((TM, d), lambda i: (i, 0)),
        out_shape=jax.ShapeDtypeStruct((n_tiles * TM, d), F32),
        scratch_shapes=[pltpu.VMEM((TM, d), F32)],
        compiler_params=_cparams(("arbitrary",), 52),
        name=("mix_ffn_final" if final else "mix_ffn") if mixer is not None else "ffn",
    )(*args)


def _swap_halves(z, low_half):
    n = z.shape[1]
    return jnp.where(low_half, pltpu.roll(z, n - HEAD_DIM // 2, 1), pltpu.roll(z, HEAD_DIM // 2, 1))


def _head_sumsq(z, ones_bd):
    zz = z * z
    hi = zz.astype(BF16)
    lo = (zz - hi.astype(F32)).astype(BF16)
    return _dot(hi, ones_bd) + _dot(lo, ones_bd)


def _inproj_ab_kernel(x_ref, mod_ref, w_ref, qn_ref, kn_ref, cos_ref, sin_ref, ones_ref, dft_ref,
                      q_ref, k_ref, vt_ref, y_ref):
    x = x_ref[...]
    m = mod_ref[...]
    h = _modulate(x, m[3:4], m[4:5]).astype(BF16)
    y = _dot(h, w_ref[...])
    ones_bd = ones_ref[...]
    cos = jnp.concatenate([cos_ref[...]] * 2, axis=1)
    sin = jnp.concatenate([sin_ref[...]] * 2, axis=1)
    lane = lax.broadcasted_iota(jnp.int32, (1, MXU_DIM), 1)
    low_half = (lane % HEAD_DIM) < (HEAD_DIM // 2)

    def norm_rope(z, nw):
        ss = _head_sumsq(z, ones_bd)
        zn = (z * lax.rsqrt(ss * (1.0 / HEAD_DIM) + NORM_EPS)) * nw
        return zn * cos + _swap_halves(zn, low_half) * sin

    qn = qn_ref[...]
    kn = kn_ref[...]
    for blk in range(A_Q_W // MXU_DIM):
        sl = slice(blk * MXU_DIM, (blk + 1) * MXU_DIM)
        q_ref[:, sl] = (norm_rope(y[:, sl], qn) * Q_SCALE_LOG2).astype(BF16)
    k_ref[...] = norm_rope(y[:, A_Q_W:A_Q_W + A_KV_W], kn).astype(BF16)
    vt_ref[...] = _values_with_ones_t(y[:, A_Q_W + A_KV_W:A_Q_W + 2 * A_KV_W], A_KV_HEADS).astype(BF16)
    f = y[:, A_Q_W + 2 * A_KV_W:].astype(BF16)
    y_ref[...] = _dot(f, dft_ref[...])


def _inproj_ab(x, mod_l, w_in, q_norm, k_norm, cos_t, sin_t, *, n_tiles, mod_row, rope_row):
    ntok, d = x.shape
    n_in = w_in.shape[1]
    ones_bd = _const_bf16(np.kron(np.eye(MXU_DIM // HEAD_DIM), np.ones((HEAD_DIM, HEAD_DIM))))
    gc, gs = _cos_sin(B_GROUP_DIM)
    eye = np.eye(B_GROUPS)
    dft = _const_bf16(np.concatenate([np.kron(eye, gc), np.kron(eye, gs)], axis=1))
    qn = jnp.tile(q_norm, MXU_DIM // HEAD_DIM).reshape(1, MXU_DIM)
    kn = jnp.tile(k_norm, MXU_DIM // HEAD_DIM).reshape(1, MXU_DIM)
    const = lambda i: (0, 0)
    row = lambda i: (i, 0)
    return pl.pallas_call(
        _inproj_ab_kernel,
        grid=(n_tiles,),
        in_specs=[
            pl.BlockSpec((TM, d), row),
            pl.BlockSpec((None, N_MOD, d), lambda i: (mod_row(i), 0, 0)),
            pl.BlockSpec((d, n_in), const),
            pl.BlockSpec((1, MXU_DIM), const),
            pl.BlockSpec((1, MXU_DIM), const),
            pl.BlockSpec((TM, LANES), lambda i: (rope_row(i), 0)),
            pl.BlockSpec((TM, LANES), lambda i: (rope_row(i), 0)),
            pl.BlockSpec((MXU_DIM, MXU_DIM), const),
            pl.BlockSpec((B_W, 2 * B_W), const),
        ],
        out_specs=[
            pl.BlockSpec((TM, A_Q_W), row),
            pl.BlockSpec((TM, A_KV_W), row),
            pl.BlockSpec((A_KV_HEADS * V_GROUP, TM), lambda i: (0, i)),
            pl.BlockSpec((TM, 2 * B_W), row),
        ],
        out_shape=[
            jax.ShapeDtypeStruct((ntok, A_Q_W), BF16),
            jax.ShapeDtypeStruct((ntok, A_KV_W), BF16),
            jax.ShapeDtypeStruct((A_KV_HEADS * V_GROUP, ntok), BF16),
            jax.ShapeDtypeStruct((ntok, 2 * B_W), F32),
        ],
        compiler_params=_cparams(("arbitrary",), 40),
        name="in_proj_ab",
    )(x, mod_l, w_in, qn, kn, cos_t, sin_t, ones_bd, dft)


def _rope_tables(seq):
    t = jnp.arange(seq)
    row = (t // GRID_W).astype(F32)
    col = (t % GRID_W).astype(F32)
    n_ax = HEAD_DIM // 4
    inv = ROPE_THETA ** (-jnp.arange(n_ax, dtype=F32) / n_ax)
    ang = jnp.concatenate([row[:, None] * inv, col[:, None] * inv], axis=-1)
    cos, sin = jnp.cos(ang), jnp.sin(ang)
    cos_h = jnp.concatenate([cos, cos], axis=-1)
    sin_h = jnp.concatenate([-sin, sin], axis=-1)
    cos_t = jnp.concatenate([jnp.tile(cos_h, (1, 2)), jnp.ones((TM, LANES), F32)], axis=0)
    sin_t = jnp.concatenate([jnp.tile(sin_h, (1, 2)), jnp.zeros((TM, LANES), F32)], axis=0)
    return cos_t, sin_t


def _gqa_kernel(bound_ref, q_ref, kc_ref, vtc_ref, *rest, has_lat):
    if has_lat:
        kl_ref, vtl_ref, o_ref, qm_ref, m_ref, acc_ref = rest
    else:
        o_ref, qm_ref, m_ref, acc_ref = rest
    j = pl.program_id(2)
    group = A_Q_HEADS // A_KV_HEADS
    lane = lax.broadcasted_iota(jnp.int32, (1, LANES), 1)
    hi_half = lane >= HEAD_DIM
    n_chunks = q_ref.shape[0] // ATT_QCHUNK
    no_shift = bound_ref[0] <= SAFE_SCORE_BOUND_LOG2

    def update(k_ref, vt_ref, shifted):
        kbody = min(ATT_KBODY, k_ref.shape[0])
        bodies = [(kb, h, c) for kb in range(k_ref.shape[0] // kbody) for h in range(A_Q_HEADS) for c in range(n_chunks)]

        def scores(kb, h, c):
            kvp = (h // group) // 2
            kp = k_ref[kb * kbody:(kb + 1) * kbody, kvp * LANES:(kvp + 1) * LANES]
            return _dot_nt(kp, qm_ref[h, c * ATT_QCHUNK:(c + 1) * ATT_QCHUNK, :])

        def accumulate(h, c, alpha, pv):
            cs = slice(c * ATT_QCHUNK, (c + 1) * ATT_QCHUNK)
            prev = acc_ref[h, :, cs] if alpha is None else alpha * acc_ref[h, :, cs]
            acc_ref[h, :, cs] = prev + pv

        queue = [scores(*bodies[t]) for t in range(min(ATT_LOOKAHEAD, len(bodies)))]
        pending = None
        for idx, (kb, h, c) in enumerate(bodies):
            s = queue.pop(0)
            if idx + ATT_LOOKAHEAD < len(bodies):
                queue.append(scores(*bodies[idx + ATT_LOOKAHEAD]))
            cs = slice(c * ATT_QCHUNK, (c + 1) * ATT_QCHUNK)
            kvh = h // group
            vt = vt_ref[kvh * V_GROUP:(kvh + 1) * V_GROUP, kb * kbody:(kb + 1) * kbody]
            if shifted:
                m_prev = m_ref[h, :, cs]
                m_new = jnp.maximum(m_prev, jnp.max(s, axis=0, keepdims=True))
                alpha = jnp.exp2(m_prev - m_new)
                p = jnp.exp2(s - m_new).astype(BF16)
                m_ref[h, :, cs] = m_new
            else:
                alpha = None
                p = jnp.exp2(s).astype(BF16)
            pv = _dot(vt, p)
            if pending is not None:
                accumulate(*pending)
            pending = (h, c, alpha, pv)
        accumulate(*pending)

    def update_either(k_ref, vt_ref):
        @pl.when(no_shift)
        def _plain():
            update(k_ref, vt_ref, shifted=False)

        @pl.when(jnp.logical_not(no_shift))
        def _running_max():
            update(k_ref, vt_ref, shifted=True)

    @pl.when(j == 0)
    def _first():
        m_ref[...] = jnp.full(m_ref.shape, NEG_BIG, F32)
        acc_ref[...] = jnp.zeros(acc_ref.shape, F32)
        for h in range(A_Q_HEADS):
            qp = q_ref[:, (h // 2) * LANES:(h // 2 + 1) * LANES].astype(F32)
            kv_half = (h // group) % 2
            if kv_half != h % 2:
                qp = pltpu.roll(qp, HEAD_DIM, 1)
            keep = hi_half if kv_half else jnp.logical_not(hi_half)
            qm_ref[h] = jnp.where(keep, qp, jnp.zeros_like(qp)).astype(BF16)
        update_either(kc_ref, vtc_ref)

    if has_lat:
        @pl.when(j > 0)
        def _rest():
            update_either(kl_ref, vtl_ref)

    @pl.when(j == pl.num_programs(2) - 1)
    def _final():
        for pb in range(A_Q_HEADS // 2):
            for c in range(n_chunks):
                cs = slice(c * ATT_QCHUNK, (c + 1) * ATT_QCHUNK)
                parts = []
                for h in (2 * pb, 2 * pb + 1):
                    a = acc_ref[h, :, cs]
                    parts.append(a[:HEAD_DIM] / a[HEAD_DIM:HEAD_DIM + 1])
                o_ref[cs, pb * LANES:(pb + 1) * LANES] = jnp.concatenate(parts, axis=0).T.astype(BF16)


def _gqa_scratch(tq):
    return [
        pltpu.VMEM((A_Q_HEADS, tq, LANES), BF16),
        pltpu.VMEM((A_Q_HEADS, 1, tq), F32),
        pltpu.VMEM((A_Q_HEADS, V_GROUP, tq), F32),
    ]


def _gqa(q, k, vt, q_norm, k_norm, *, bsz, seq, lc):
    n_lat = bsz * seq
    tq, tk = ATT_TQ, ATT_TK
    ctx_blk = lambda b: n_lat // lc + b
    bound = (1.05 * HEAD_DIM * Q_SCALE_LOG2 * jnp.max(jnp.abs(q_norm)) * jnp.max(jnp.abs(k_norm))).reshape(1)
    smem = pl.BlockSpec(memory_space=pltpu.SMEM)
    o = pl.pallas_call(
        functools.partial(_gqa_kernel, has_lat=True),
        grid=(bsz, seq // tq, 1 + seq // tk),
        in_specs=[
            smem,
            pl.BlockSpec((tq, A_Q_W), lambda b, i, j: (b * (seq // tq) + i, 0)),
            pl.BlockSpec((lc, A_KV_W), lambda b, i, j: (ctx_blk(b), 0)),
            pl.BlockSpec((A_KV_HEADS * V_GROUP, lc), lambda b, i, j: (0, ctx_blk(b))),
            pl.BlockSpec((tk, A_KV_W), lambda b, i, j: (b * (seq // tk) + jnp.maximum(j - 1, 0), 0)),
            pl.BlockSpec((A_KV_HEADS * V_GROUP, tk), lambda b, i, j: (0, b * (seq // tk) + jnp.maximum(j - 1, 0))),
        ],
        out_specs=pl.BlockSpec((tq, A_Q_W), lambda b, i, j: (b * (seq // tq) + i, 0)),
        out_shape=jax.ShapeDtypeStruct((n_lat, A_Q_W), BF16),
        scratch_shapes=_gqa_scratch(tq),
        compiler_params=_cparams(("arbitrary", "arbitrary", "arbitrary"), 48),
        name="gqa_latent",
    )(bound, q, k, vt, k, vt)
    o_ctx = pl.pallas_call(
        functools.partial(_gqa_kernel, has_lat=False),
        grid=(bsz, 1, 1),
        in_specs=[
            smem,
            pl.BlockSpec((lc, A_Q_W), lambda b, i, j: (ctx_blk(b), 0)),
            pl.BlockSpec((lc, A_KV_W), lambda b, i, j: (ctx_blk(b), 0)),
            pl.BlockSpec((A_KV_HEADS * V_GROUP, lc), lambda b, i, j: (0, ctx_blk(b))),
        ],
        out_specs=pl.BlockSpec((lc, A_Q_W), lambda b, i, j: (b, 0)),
        out_shape=jax.ShapeDtypeStruct((bsz * lc, A_Q_W), BF16),
        scratch_shapes=_gqa_scratch(lc),
        compiler_params=_cparams(("arbitrary", "arbitrary", "arbitrary"), 32),
        name="gqa_context",
    )(bound, q, k, vt)
    return o, o_ctx


def _dft_a_kernel(y_ref, ca_ref, tc_ref, ts_ref, d_ref, *, nbb):
    ca = ca_ref[...]
    ra = ca.shape[1]
    for j in range(nbb):
        yj = y_ref[:, j, :].astype(BF16)
        p = _dot(ca, yj)
        br = p[:ra, :B_W] - p[ra:, B_W:]
        bi = -p[:ra, B_W:] - p[ra:, :B_W]
        tc = jnp.concatenate([tc_ref[j]] * (B_W // LANES), axis=1)
        ts = jnp.concatenate([ts_ref[j]] * (B_W // LANES), axis=1)
        d_ref[:, 0, j, :] = br * tc + bi * ts
        d_ref[:, 1, j, :] = bi * tc - br * ts


def _dft_b_kernel(d_ref, cb_ref, o_ref, *, kb, scale):
    cb = cb_ref[...]
    for j in range(kb):
        o_ref[:, j, :] = _dot(cb, d_ref[j].astype(BF16)) * scale


def _dft_ctx_kernel(y_ref, cs_ref, o_ref, *, scale):
    y = y_ref[...]
    cs = cs_ref[...]
    lc = cs.shape[0]
    yc = y[:, :B_W].astype(BF16)
    ys = y[:, B_W:].astype(BF16)
    o_ref[...] = (_dot(cs[:, :lc], yc) - _dot(cs[:, lc:], ys)) * scale


def _fourier(y, *, bsz, seq, lc):
    ntok = y.shape[0]
    ra = seq // GRID_W
    nb = GRID_W
    nbb = SUBLANES
    kb = SUBLANES
    assert ntok % nb == 0 and ra % SUBLANES == 0 and (ntok // nb) % 1 == 0
    ca_c, ca_s = _cos_sin(ra)
    ca = _const_bf16(np.concatenate([ca_c, ca_s], axis=0))
    th = 2.0 * np.pi * np.outer(np.arange(nb), np.arange(ra)) / seq
    tc = jnp.asarray(np.broadcast_to(np.cos(th)[:, :, None], (nb, ra, LANES)).astype(np.float32))
    ts = jnp.asarray(np.broadcast_to(np.sin(th)[:, :, None], (nb, ra, LANES)).astype(np.float32))
    cb_c, cb_s = _cos_sin(nb)
    cb = _const_bf16(np.concatenate([cb_c, cb_s], axis=1))

    y3 = y.reshape(ntok // nb, nb, 2 * B_W)
    d = pl.pallas_call(
        functools.partial(_dft_a_kernel, nbb=nbb),
        grid=(bsz, nb // nbb),
        in_specs=[
            pl.BlockSpec((ra, nbb, 2 * B_W), lambda b, i: (b, i, 0)),
            pl.BlockSpec((2 * ra, ra), lambda b, i: (0, 0)),
            pl.BlockSpec((nbb, ra, LANES), lambda b, i: (i, 0, 0)),
            pl.BlockSpec((nbb, ra, LANES), lambda b, i: (i, 0, 0)),
        ],
        out_specs=pl.BlockSpec((None, ra, 2, nbb, B_W), lambda b, i: (b, 0, 0, i, 0)),
        out_shape=jax.ShapeDtypeStruct((bsz, ra, 2, nb, B_W), F32),
        compiler_params=_cparams(("arbitrary", "arbitrary"), 32),
        name="fourier_rows",
    )(y3, ca, tc, ts)
    d = d.reshape(bsz, ra, 2 * nb, B_W)

    scale = 1.0 / math.sqrt(seq * B_GROUP_DIM)
    n_out = bsz * nb
    z3 = pl.pallas_call(
        functools.partial(_dft_b_kernel, kb=kb, scale=scale),
        grid=(bsz, ra // kb),
        in_specs=[
            pl.BlockSpec((None, kb, 2 * nb, B_W), lambda b, i: (b, i, 0, 0)),
            pl.BlockSpec((nb, 2 * nb), lambda b, i: (0, 0)),
        ],
        out_specs=pl.BlockSpec((nb, kb, B_W), lambda b, i: (b, i, 0)),
        out_shape=jax.ShapeDtypeStruct((n_out, ra, B_W), F32),
        compiler_params=_cparams(("arbitrary", "arbitrary"), 32),
        name="fourier_cols",
    )(d, cb)
    z = z3.reshape(bsz * seq, B_W)

    cc, cs_ = _cos_sin(lc)
    csm = _const_bf16(np.concatenate([cc, cs_], axis=1))
    ctx_blk = lambda b: (bsz * seq) // lc + b
    z_ctx = pl.pallas_call(
        functools.partial(_dft_ctx_kernel, scale=1.0 / math.sqrt(lc * B_GROUP_DIM)),
        grid=(bsz,),
        in_specs=[
            pl.BlockSpec((lc, 2 * B_W), lambda b: (ctx_blk(b), 0)),
            pl.BlockSpec((lc, 2 * lc), lambda b: (0, 0)),
        ],
        out_specs=pl.BlockSpec((lc, B_W), lambda b: (b, 0)),
        out_shape=jax.ShapeDtypeStruct((bsz * lc, B_W), F32),
        compiler_params=_cparams(("arbitrary",), 32),
        name="fourier_context",
    )(y, csm)
    return z, z_ctx


def _values_with_ones_t(v, n_heads):
    vt = v.T
    ones = jnp.ones((ONES_ROWS, vt.shape[1]), F32)
    groups = []
    for h in range(n_heads):
        groups += [vt[h * HEAD_DIM:(h + 1) * HEAD_DIM], ones]
    return jnp.concatenate(groups, axis=0)


def _inproj_cd_kernel(x_ref, mod_ref, w_ref, u_ref, q_ref, k_ref, vt_ref):
    x = x_ref[...]
    m = mod_ref[...]
    h = _modulate(x, m[3:4], m[4:5]).astype(BF16)
    y = _dot(h, w_ref[...])
    c = C_CHANNELS
    u_ref[...] = y[:, :c] * _sigmoid(y[:, c:2 * c])
    q_ref[...] = (y[:, 2 * c:2 * c + D_W] * Q_SCALE_LOG2).astype(BF16)
    k_ref[...] = y[:, 2 * c + D_W:2 * c + 2 * D_W].astype(BF16)
    vt_ref[...] = _values_with_ones_t(y[:, 2 * c + 2 * D_W:], D_HEADS).astype(BF16)


def _inproj_cd(x, mod_l, w_in, *, n_tiles, mod_row):
    ntok, d = x.shape
    n_in = w_in.shape[1]
    row = lambda i: (i, 0)
    return pl.pallas_call(
        _inproj_cd_kernel,
        grid=(n_tiles,),
        in_specs=[
            pl.BlockSpec((TM, d), row),
            pl.BlockSpec((None, N_MOD, d), lambda i: (mod_row(i), 0, 0)),
            pl.BlockSpec((d, n_in), lambda i: (0, 0)),
        ],
        out_specs=[
            pl.BlockSpec((TM, C_CHANNELS), row),
            pl.BlockSpec((TM, D_W), row),
            pl.BlockSpec((TM, D_W), row),
            pl.BlockSpec((D_HEADS * V_GROUP, TM), lambda i: (0, i)),
        ],
        out_shape=[
            jax.ShapeDtypeStruct((ntok, C_CHANNELS), F32),
            jax.ShapeDtypeStruct((ntok, D_W), BF16),
            jax.ShapeDtypeStruct((ntok, D_W), BF16),
            jax.ShapeDtypeStruct((D_HEADS * V_GROUP, ntok), BF16),
        ],
        compiler_params=_cparams(("arbitrary",), 40),
        name="in_proj_cd",
    )(x, mod_l, w_in)


def _conv_kernel(prev_ref, cur_ref, next_ref, w_ref, b_ref, lnw_ref, lnb_ref, o_ref, buf_ref, shift_ref, *,
                 tiles_per_seq):
    i = pl.program_id(1)
    halo = CONV_HALO
    tm = cur_ref.shape[0]
    zeros = jnp.zeros((halo, cur_ref.shape[1]), F32)
    buf_ref[0:halo, :] = jnp.where(i > 0, prev_ref[...], zeros)
    buf_ref[halo:halo + tm, :] = cur_ref[...]
    buf_ref[halo + tm:, :] = jnp.where(i < tiles_per_seq - 1, next_ref[...], zeros)
    w = w_ref[...]
    off = halo - C_KERNEL // 2
    y = None
    for r in range(SUBLANES):
        taps = [t for t in range(C_KERNEL) if (t + off) % SUBLANES == r]
        part = None
        for t in taps:
            start = t + off - r
            term = buf_ref[start:start + tm + SUBLANES, :] * w[t:t + 1]
            part = term if part is None else part + term
        if r == 0:
            shifted = part[:tm]
        else:
            shift_ref[...] = part
            shifted = shift_ref[r:r + tm, :]
        y = shifted if y is None else y + shifted
    y = y + b_ref[...]
    mu = jnp.mean(y, axis=-1, keepdims=True)
    yc = y - mu
    var = jnp.mean(yc * yc, axis=-1, keepdims=True)
    z = (yc * lax.rsqrt(var + NORM_EPS)) * lnw_ref[...] + lnb_ref[...]
    o_ref[...] = (z * _sigmoid(z)).astype(BF16)


def _conv_module(u, dw_w, dw_b, ln_w, ln_b, *, bsz, seq):
    c = u.shape[1]
    tiles = seq // TM
    hpt = TM // CONV_HALO
    n_halo_blocks = u.shape[0] // CONV_HALO
    cur = lambda b, i: (b * tiles + i, 0)
    prev = lambda b, i: (jnp.maximum((b * tiles + i) * hpt - 1, 0), 0)
    nxt = lambda b, i: (jnp.minimum((b * tiles + i + 1) * hpt, n_halo_blocks - 1), 0)
    vec = lambda b, i: (0, 0)
    return pl.pallas_call(
        functools.partial(_conv_kernel, tiles_per_seq=tiles),
        grid=(bsz, tiles),
        in_specs=[
            pl.BlockSpec((CONV_HALO, c), prev),
            pl.BlockSpec((TM, c), cur),
            pl.BlockSpec((CONV_HALO, c), nxt),
            pl.BlockSpec((C_KERNEL, c), vec),
            pl.BlockSpec((1, c), vec),
            pl.BlockSpec((1, c), vec),
            pl.BlockSpec((1, c), vec),
        ],
        out_specs=pl.BlockSpec((TM, c), cur),
        out_shape=jax.ShapeDtypeStruct((bsz * seq, c), BF16),
        scratch_shapes=[pltpu.VMEM((TM + 2 * CONV_HALO, c), F32), pltpu.VMEM((TM + SUBLANES, c), F32)],
        compiler_params=_cparams(("arbitrary", "arbitrary"), 32),
        name="conv_module",
    )(u, u, u, dw_w, dw_b.reshape(1, c), ln_w.reshape(1, c), ln_b.reshape(1, c))


def _na_kernel(q_ref, k0_ref, k1_ref, k2_ref, vt0_ref, vt1_ref, vt2_ref, kc_ref, vtc_ref, bias_ref, o_ref):
    lane = lax.broadcasted_iota(jnp.int32, (1, LANES), 1)
    hi_half = lane >= HEAD_DIM

    def scores(h):
        sl = slice((h // 2) * LANES, (h // 2 + 1) * LANES)
        qp = q_ref[:, sl]
        keep = hi_half if h % 2 else jnp.logical_not(hi_half)
        qm = jnp.where(keep, qp, jnp.zeros_like(qp))
        kn = jnp.concatenate([k0_ref[:, sl], k1_ref[:, sl], k2_ref[:, sl]], axis=0)
        return _dot_nt(kn, qm) + bias_ref[h], _dot_nt(kc_ref[:, sl], qm)

    lookahead = min(NA_LOOKAHEAD, D_HEADS)
    queue = [scores(h) for h in range(lookahead)]
    outs = []
    for h in range(D_HEADS):
        s_nb, s_cx = queue.pop(0)
        if h + lookahead < D_HEADS:
            queue.append(scores(h + lookahead))
        mx = jnp.maximum(jnp.max(s_nb, axis=0, keepdims=True), jnp.max(s_cx, axis=0, keepdims=True))
        p = jnp.concatenate([jnp.exp2(s_nb - mx).astype(BF16), jnp.exp2(s_cx - mx).astype(BF16)], axis=0)
        rs = slice(h * V_GROUP, (h + 1) * V_GROUP)
        vt = jnp.concatenate([vt0_ref[rs, :], vt1_ref[rs, :], vt2_ref[rs, :], vtc_ref[rs, :]], axis=1)
        outs.append(_dot(vt, p))
    for pb in range(D_HEADS // 2):
        parts = [outs[h][:HEAD_DIM] / outs[h][HEAD_DIM:HEAD_DIM + 1] for h in (2 * pb, 2 * pb + 1)]
        o_ref[:, pb * LANES:(pb + 1) * LANES] = jnp.concatenate(parts, axis=0).T.astype(BF16)


def _na_bias_kernel(rpb_ref, o_ref, *, rows):
    h = pl.program_id(0)
    n_ro, n_co = 2 * NA_WIN_ROWS - 1, 2 * NA_WIN_COLS - 1
    kc = lax.broadcasted_iota(jnp.int32, (GRID_W, LANES), 0)
    lane = lax.broadcasted_iota(jnp.int32, (GRID_W, LANES), 1)
    qc = lane % GRID_W
    d = kc - qc + (NA_WIN_COLS - 1)
    cs = jnp.clip(qc - NA_WIN_COLS // 2, 0, GRID_W - NA_WIN_COLS)
    col_ok = (kc >= cs) & (kc < cs + NA_WIN_COLS)
    neg = jnp.full((GRID_W, LANES), NEG_BIG, F32)
    log2e = math.log2(math.e)
    band = []
    for ro in range(n_ro):
        acc = neg
        for co in range(n_co):
            acc = jnp.where(d == co, rpb_ref[(h * n_ro + ro) * n_co + co] * log2e, acc)
        band.append(jnp.where(col_ok, acc, neg))
    qr = NA_QROWS
    for kind, r0 in enumerate((0, qr, rows - qr)):
        for b in range(3 * qr):
            rk = r0 - qr + b
            for pair in range(qr // 2):
                halves = []
                for a in (2 * pair, 2 * pair + 1):
                    r = r0 + a
                    start = min(max(r - NA_WIN_ROWS // 2, 0), rows - NA_WIN_ROWS)
                    ok = start <= rk < start + NA_WIN_ROWS and 0 <= rk < rows
                    halves.append(band[rk - r + NA_WIN_ROWS - 1] if ok else neg)
                o_ref[kind, b * GRID_W:(b + 1) * GRID_W, pair * LANES:(pair + 1) * LANES] = jnp.where(
                    lane < GRID_W, halves[0], halves[1])


def _na_bias(rpb, rows):
    heads = rpb.shape[0]
    tq = NA_QROWS * GRID_W
    return pl.pallas_call(
        functools.partial(_na_bias_kernel, rows=rows),
        grid=(heads,),
        in_specs=[pl.BlockSpec(memory_space=pltpu.SMEM)],
        out_specs=pl.BlockSpec((3, None, 3 * tq, tq), lambda h: (0, h, 0, 0)),
        out_shape=jax.ShapeDtypeStruct((3, heads, 3 * tq, tq), F32),
        compiler_params=_cparams(("arbitrary",), 32),
        name="na_bias",
    )(rpb.reshape(-1))


def _na(q, k, vt, rpb, *, bsz, seq, lc):
    rows = seq // GRID_W
    tq = NA_QROWS * GRID_W
    nblk = seq // tq
    assert rows >= NA_WIN_ROWS and rows % NA_QROWS == 0 and NA_WIN_ROWS == 2 * NA_QROWS and nblk >= 3
    assert lc == tq
    bias = _na_bias(rpb, rows)
    n_lat = bsz * seq
    qmap = lambda b, i: (b * nblk + i, 0)

    def near(off, transposed):
        def index(b, i):
            blk = b * nblk + jnp.clip(i + off, 0, nblk - 1)
            return (0, blk) if transposed else (blk, 0)
        return index

    kind = lambda b, i: (jnp.where(i == 0, 0, jnp.where(i == nblk - 1, 2, 1)), 0, 0, 0)
    rows_blk = lambda m: pl.BlockSpec((tq, D_W), m)
    cols_blk = lambda m: pl.BlockSpec((D_HEADS * V_GROUP, tq), m)
    return pl.pallas_call(
        _na_kernel,
        grid=(bsz, nblk),
        in_specs=[
            rows_blk(qmap),
            rows_blk(near(-1, False)), rows_blk(near(0, False)), rows_blk(near(1, False)),
            cols_blk(near(-1, True)), cols_blk(near(0, True)), cols_blk(near(1, True)),
            rows_blk(lambda b, i: (n_lat // lc + b, 0)),
            cols_blk(lambda b, i: (0, n_lat // lc + b)),
            pl.BlockSpec((None, D_HEADS, 3 * tq, tq), kind),
        ],
        out_specs=rows_blk(qmap),
        out_shape=jax.ShapeDtypeStruct((n_lat, D_W), BF16),
        compiler_params=_cparams(("arbitrary", "arbitrary"), 48),
        name="neighbourhood_attention",
    )(q, k, k, k, vt, vt, vt, k, vt, bias)


def kernel(x, c, ctx, c_ctx, w_mod, b_mod, ffn_w_gate, ffn_w_up, ffn_w_down, ab_w_in, ab_w_out, ab_q_norm,
           ab_k_norm, cd_w_in, cd_w_out, cd_dw_w, cd_dw_b, cd_ln_w, cd_ln_b, cd_rpb, final_norm):
    bsz, seq, d = x.shape
    lc = ctx.shape[1]
    depth = w_mod.shape[0]
    assert depth == 2, "layer 0 (A/B mixer) updates the context, layer 1 (C/D mixer) is the last layer"
    assert bsz * lc == TM and seq % TM == 0 and seq % ATT_TQ == 0 and seq % ATT_TK == 0
    n_lat_tiles = bsz * seq // TM
    n_all_tiles = n_lat_tiles + 1
    tiles_per_seq = seq // TM
    mod_row = lambda i: jnp.where(i == n_lat_tiles, bsz, i // tiles_per_seq)
    rope_row = lambda i: jnp.where(i == n_lat_tiles, tiles_per_seq, i % tiles_per_seq)

    mod = _mod_params(c, c_ctx, w_mod, b_mod)
    cos_t, sin_t = _rope_tables(seq)

    ffn_w = (ffn_w_gate.astype(BF16), ffn_w_up.astype(BF16), ffn_w_down.astype(BF16))

    xt = x.reshape(bsz * seq, d)
    for layer in range(depth):
        last = layer == depth - 1
        i = layer // 2
        mod_l = mod[layer]
        xt = _ffn(xt, mod_l, ffn_w, layer, 0, base=0, n_tiles=n_all_tiles, mod_row=mod_row,
                  x_ctx=ctx.reshape(bsz * lc, d) if layer == 0 else None)
        if layer % 2 == 0:
            q, k, vt, y = _inproj_ab(xt, mod_l, ab_w_in[i].astype(BF16), ab_q_norm[i], ab_k_norm[i], cos_t, sin_t,
                                     n_tiles=n_all_tiles, mod_row=mod_row, rope_row=rope_row)
            a1, a1_ctx = _gqa(q, k, vt, ab_q_norm[i], ab_k_norm[i], bsz=bsz, seq=seq, lc=lc)
            a2, a2_ctx = _fourier(y, bsz=bsz, seq=seq, lc=lc)
            mixer = (a1, a2, ab_w_out[i].astype(BF16), a1_ctx, a2_ctx)
        else:
            u, q, k, vt = _inproj_cd(xt, mod_l, cd_w_in[i].astype(BF16), n_tiles=n_all_tiles, mod_row=mod_row)
            a1 = _conv_module(u, cd_dw_w[i], cd_dw_b[i], cd_ln_w[i], cd_ln_b[i], bsz=bsz, seq=seq)
            a2 = _na(q, k, vt, cd_rpb[i], bsz=bsz, seq=seq, lc=lc)
            mixer = (a1, a2, cd_w_out[i].astype(BF16), None, None)
        assert last == (mixer[3] is None)
        xt = _ffn(xt, mod_l, ffn_w, layer, 1, base=6, n_tiles=n_lat_tiles if last else n_all_tiles, mod_row=mod_row,
                  final_norm=final_norm if last else None, mixer=mixer)
    return xt.reshape(bsz, seq, d)
```

```python
import functools
import math

import numpy as np
import jax
import jax.numpy as jnp
from jax import lax
from jax.experimental import pallas as pl
from jax.experimental.pallas import tpu as pltpu

F32 = jnp.float32
BF16 = jnp.bfloat16

GRID_W = 64
HEAD_DIM = 64
A_Q_HEADS = 12
A_KV_HEADS = 4
B_GROUPS = 4
B_GROUP_DIM = 64
C_CHANNELS = 512
C_KERNEL = 31
D_HEADS = 8
NA_WIN_ROWS = 8
NA_WIN_COLS = 16
ROPE_THETA = 10000.0
NORM_EPS = 1e-6
N_MOD = 9
FFN_RES_WEIGHT = 0.5

A_Q_W = A_Q_HEADS * HEAD_DIM
A_KV_W = A_KV_HEADS * HEAD_DIM
B_W = B_GROUPS * B_GROUP_DIM
D_W = D_HEADS * HEAD_DIM

LANES = 128
SUBLANES = 8
MXU_DIM = 256

TM = 512
FFN_CHUNK = 256
ATT_TQ = 1024
ATT_TK = 1024
ATT_KBODY = 512
NA_QROWS = 4
NA_LOOKAHEAD = 3
CONV_HALO = 16
ATT_QCHUNK = 256
ATT_LOOKAHEAD = 4
SAFE_SCORE_BOUND_LOG2 = 100.0
ONES_ROWS = 16
V_GROUP = HEAD_DIM + ONES_ROWS
NEG_BIG = -1e30
Q_SCALE_LOG2 = (HEAD_DIM ** -0.5) * math.log2(math.e)

_MiB = 1 << 20


def _cparams(sem, vmem_mib):
    return pltpu.CompilerParams(dimension_semantics=sem, vmem_limit_bytes=vmem_mib * _MiB)


def _dot(a, b):
    return jnp.dot(a, b, preferred_element_type=F32)


def _dot_nt(a, b):
    return lax.dot_general(a, b, (((1,), (1,)), ((), ())), preferred_element_type=F32)


def _sigmoid(x):
    return 1.0 / (1.0 + jnp.exp(-x))


def _cos_sin(n):
    idx = np.arange(n)
    ang = 2.0 * np.pi * (np.outer(idx, idx) % n) / n
    return np.cos(ang), np.sin(ang)


def _const_bf16(a):
    return jnp.asarray(np.asarray(a, np.float32)).astype(BF16)


def _modulate(x, shift, scale):
    ms = jnp.mean(x * x, axis=-1, keepdims=True)
    return (x * lax.rsqrt(ms + NORM_EPS)) * (1.0 + scale) + shift


def _mod_kernel(ct_ref, w_ref, b_ref, o_ref, *, n_rows):
    ct = ct_ref[...]
    a = ct * _sigmoid(ct)
    w = w_ref[...]
    rows = [jnp.sum(w * a[:, r:r + 1], axis=0, keepdims=True) for r in range(n_rows)]
    rows.append(jnp.zeros((SUBLANES - n_rows, w.shape[1]), F32))
    o_ref[...] = jnp.concatenate(rows, axis=0) + b_ref[...]


def _mod_params(c, c_ctx, w_mod, b_mod):
    depth, d, nmd = w_mod.shape
    bsz = c.shape[0]
    n_rows = bsz + 1
    assert n_rows <= SUBLANES
    ct = jnp.concatenate([c, c_ctx[None], jnp.zeros((SUBLANES - n_rows, d), F32)], axis=0).T
    tn = d
    out = pl.pallas_call(
        functools.partial(_mod_kernel, n_rows=n_rows),
        grid=(depth, nmd // tn),
        in_specs=[
            pl.BlockSpec((d, SUBLANES), lambda l, n: (0, 0)),
            pl.BlockSpec((None, d, tn), lambda l, n: (l, 0, n)),
            pl.BlockSpec((None, 1, tn), lambda l, n: (l, 0, n)),
        ],
        out_specs=pl.BlockSpec((None, SUBLANES, tn), lambda l, n: (l, 0, n)),
        out_shape=jax.ShapeDtypeStruct((depth, SUBLANES, nmd), F32),
        compiler_params=_cparams(("arbitrary", "arbitrary"), 32),
        name="adaln_params",
    )(ct, w_mod, b_mod.reshape(depth, 1, nmd))
    return out.reshape(depth, SUBLANES, N_MOD, d)


def _pick_rows(lat_ref, ctx_ref, n_lat_tiles):
    return jnp.where(pl.program_id(0) == n_lat_tiles, ctx_ref[...], lat_ref[...])


def _ffn_kernel(*refs, base, final, n_chunks, ctx_tile, x_has_ctx, mixer):
    refs = list(refs)
    x_ref = refs.pop(0)
    x = x_ref[...] if not x_has_ctx else _pick_rows(x_ref, refs.pop(0), ctx_tile)
    if mixer:
        a1_ref, a2_ref = refs.pop(0), refs.pop(0)
        if ctx_tile is None:
            a1, a2 = a1_ref[...], a2_ref[...]
        else:
            a1 = _pick_rows(a1_ref, refs.pop(0), ctx_tile)
            a2 = _pick_rows(a2_ref, refs.pop(0), ctx_tile)
        w1_ref, w2_ref = refs.pop(0), refs.pop(0)
    mod_ref, wg_ref, wu_ref, wd_ref = refs[:4]
    if final:
        fn_ref, o_ref, acc_ref = refs[4:]
    else:
        o_ref, acc_ref = refs[4:]
    m = mod_ref[...]
    if mixer:
        x = x + m[5:6] * (_dot(a1.astype(BF16), w1_ref[...].astype(BF16)) +
                          _dot(a2.astype(BF16), w2_ref[...].astype(BF16)))
    h = _modulate(x, m[base:base + 1], m[base + 1:base + 2]).astype(BF16)
    for c in range(n_chunks):
        sl = slice(c * FFN_CHUNK, (c + 1) * FFN_CHUNK)
        g = _dot(h, wg_ref[:, sl].astype(BF16))
        u = _dot(h, wu_ref[:, sl].astype(BF16))
        a = ((g * _sigmoid(g)) * u).astype(BF16)
        part = _dot(a, wd_ref[sl, :].astype(BF16))
        if c == 0:
            acc_ref[...] = part
        else:
            acc_ref[...] += part
    y = x + (FFN_RES_WEIGHT * m[base + 2:base + 3]) * acc_ref[...]
    if final:
        ms = jnp.mean(y * y, axis=-1, keepdims=True)
        y = (y * lax.rsqrt(ms + NORM_EPS)) * fn_ref[...]
    o_ref[...] = y


def _ffn(x, mod_l, ffn_w, layer, half, *, base, n_tiles, mod_row, final_norm=None, x_ctx=None, mixer=None):
    d = x.shape[1]
    wg, wu, wd = ffn_w
    dff = wg.shape[-1]
    assert dff % FFN_CHUNK == 0
    final = final_norm is not None
    resident = dict(pipeline_mode=pl.Buffered(1))
    const = lambda i: (0, 0)
    last_lat = n_tiles - 2
    lat_only = lambda i: (jnp.minimum(i, last_lat), 0)
    ctx_tile = None
    in_specs = [pl.BlockSpec((TM, d), lambda i: (i, 0))]
    args = [x]
    if x_ctx is not None:
        ctx_tile = n_tiles - 1
        assert x.shape[0] == ctx_tile * TM and x_ctx.shape == (TM, d)
        in_specs = [pl.BlockSpec((TM, d), lat_only), pl.BlockSpec((TM, d), const)]
        args.append(x_ctx)
    if mixer is not None:
        a1, a2, w_out, a1_ctx, a2_ctx = mixer
        d1, d2 = a1.shape[1], a2.shape[1]
        if a1_ctx is None:
            in_specs += [pl.BlockSpec((TM, d1), lambda i: (i, 0)), pl.BlockSpec((TM, d2), lambda i: (i, 0))]
            args += [a1, a2]
        else:
            ctx_tile = n_tiles - 1
            assert a1.shape[0] == ctx_tile * TM and a1_ctx.shape[0] == TM and a2_ctx.shape[0] == TM
            in_specs += [pl.BlockSpec((TM, d1), lat_only), pl.BlockSpec((TM, d2), lat_only),
                         pl.BlockSpec((TM, d1), const), pl.BlockSpec((TM, d2), const)]
            args += [a1, a2, a1_ctx, a2_ctx]
        in_specs += [pl.BlockSpec((d1, d), const, **resident), pl.BlockSpec((d2, d), lambda i: (d1 // d2, 0), **resident)]
        assert d1 % d2 == 0
        args += [w_out, w_out]
    sel = lambda i: (layer, half, 0, 0)
    in_specs += [
        pl.BlockSpec((None, N_MOD, d), lambda i: (mod_row(i), 0, 0)),
        pl.BlockSpec((None, None, d, dff), sel, **resident),
        pl.BlockSpec((None, None, d, dff), sel, **resident),
        pl.BlockSpec((None, None, dff, d), sel, **resident),
    ]
    args += [mod_l, wg, wu, wd]
    if final:
        in_specs.append(pl.BlockSpec((1, d), const))
        args.append(final_norm.reshape(1, d))
    return pl.pallas_call(
        functools.partial(_ffn_kernel, base=base, final=final, n_chunks=dff // FFN_CHUNK, ctx_tile=ctx_tile,
                          x_has_ctx=x_ctx is not None, mixer=mixer is not None),
        grid=(n_tiles,),
        in_specs=in_specs,
        out_specs=pl.BlockSpec((TM, d), lambda i: (i, 0)),
        out_shape=jax.ShapeDtypeStruct((n_tiles * TM, d), F32),
        scratch_shapes=[pltpu.VMEM((TM, d), F32)],
        compiler_params=_cparams(("arbitrary",), 60),
        name=("mix_ffn_final" if final else "mix_ffn") if mixer is not None else "ffn",
    )(*args)


def _swap_halves(z, low_half):
    n = z.shape[1]
    return jnp.where(low_half, pltpu.roll(z, n - HEAD_DIM // 2, 1), pltpu.roll(z, HEAD_DIM // 2, 1))


def _head_sumsq(z, ones_bd):
    zz = z * z
    hi = zz.astype(BF16)
    lo = (zz - hi.astype(F32)).astype(BF16)
    return _dot(hi, ones_bd) + _dot(lo, ones_bd)


def _inproj_ab_kernel(x_ref, mod_ref, w_ref, qn_ref, kn_ref, cos_ref, sin_ref, ones_ref, dft_ref,
                      q_ref, k_ref, vt_ref, y_ref):
    x = x_ref[...]
    m = mod_ref[...]
    h = _modulate(x, m[3:4], m[4:5]).astype(BF16)
    y = _dot(h, w_ref[...].astype(BF16))
    ones_bd = ones_ref[...]
    cos = jnp.concatenate([cos_ref[...]] * 2, axis=1)
    sin = jnp.concatenate([sin_ref[...]] * 2, axis=1)
    lane = lax.broadcasted_iota(jnp.int32, (1, MXU_DIM), 1)
    low_half = (lane % HEAD_DIM) < (HEAD_DIM // 2)

    def norm_rope(z, nw):
        ss = _head_sumsq(z, ones_bd)
        zn = (z * lax.rsqrt(ss * (1.0 / HEAD_DIM) + NORM_EPS)) * nw
        return zn * cos + _swap_halves(zn, low_half) * sin

    qn = qn_ref[...]
    kn = kn_ref[...]
    for blk in range(A_Q_W // MXU_DIM):
        sl = slice(blk * MXU_DIM, (blk + 1) * MXU_DIM)
        q_ref[:, sl] = (norm_rope(y[:, sl], qn) * Q_SCALE_LOG2).astype(BF16)
    k_ref[...] = norm_rope(y[:, A_Q_W:A_Q_W + A_KV_W], kn).astype(BF16)
    vt_ref[...] = _values_with_ones_t(y[:, A_Q_W + A_KV_W:A_Q_W + 2 * A_KV_W], A_KV_HEADS).astype(BF16)
    f = y[:, A_Q_W + 2 * A_KV_W:].astype(BF16)
    y_ref[...] = _dot(f, dft_ref[...])


def _inproj_ab(x, mod_l, w_in, q_norm, k_norm, cos_t, sin_t, *, n_tiles, mod_row, rope_row):
    ntok, d = x.shape
    n_in = w_in.shape[1]
    ones_bd = _const_bf16(np.kron(np.eye(MXU_DIM // HEAD_DIM), np.ones((HEAD_DIM, HEAD_DIM))))
    gc, gs = _cos_sin(B_GROUP_DIM)
    eye = np.eye(B_GROUPS)
    dft = _const_bf16(np.concatenate([np.kron(eye, gc), np.kron(eye, gs)], axis=1))
    qn = jnp.tile(q_norm, MXU_DIM // HEAD_DIM).reshape(1, MXU_DIM)
    kn = jnp.tile(k_norm, MXU_DIM // HEAD_DIM).reshape(1, MXU_DIM)
    const = lambda i: (0, 0)
    row = lambda i: (i, 0)
    return pl.pallas_call(
        _inproj_ab_kernel,
        grid=(n_tiles,),
        in_specs=[
            pl.BlockSpec((TM, d), row),
            pl.BlockSpec((None, N_MOD, d), lambda i: (mod_row(i), 0, 0)),
            pl.BlockSpec((d, n_in), const),
            pl.BlockSpec((1, MXU_DIM), const),
            pl.BlockSpec((1, MXU_DIM), const),
            pl.BlockSpec((TM, LANES), lambda i: (rope_row(i), 0)),
            pl.BlockSpec((TM, LANES), lambda i: (rope_row(i), 0)),
            pl.BlockSpec((MXU_DIM, MXU_DIM), const),
            pl.BlockSpec((B_W, 2 * B_W), const),
        ],
        out_specs=[
            pl.BlockSpec((TM, A_Q_W), row),
            pl.BlockSpec((TM, A_KV_W), row),
            pl.BlockSpec((A_KV_HEADS * V_GROUP, TM), lambda i: (0, i)),
            pl.BlockSpec((TM, 2 * B_W), row),
        ],
        out_shape=[
            jax.ShapeDtypeStruct((ntok, A_Q_W), BF16),
            jax.ShapeDtypeStruct((ntok, A_KV_W), BF16),
            jax.ShapeDtypeStruct((A_KV_HEADS * V_GROUP, ntok), BF16),
            jax.ShapeDtypeStruct((ntok, 2 * B_W), F32),
        ],
        compiler_params=_cparams(("arbitrary",), 40),
        name="in_proj_ab",
    )(x, mod_l, w_in, qn, kn, cos_t, sin_t, ones_bd, dft)


def _rope_tables(seq):
    t = jnp.arange(seq)
    row = (t // GRID_W).astype(F32)
    col = (t % GRID_W).astype(F32)
    n_ax = HEAD_DIM // 4
    inv = ROPE_THETA ** (-jnp.arange(n_ax, dtype=F32) / n_ax)
    ang = jnp.concatenate([row[:, None] * inv, col[:, None] * inv], axis=-1)
    cos, sin = jnp.cos(ang), jnp.sin(ang)
    cos_h = jnp.concatenate([cos, cos], axis=-1)
    sin_h = jnp.concatenate([-sin, sin], axis=-1)
    cos_t = jnp.concatenate([jnp.tile(cos_h, (1, 2)), jnp.ones((TM, LANES), F32)], axis=0)
    sin_t = jnp.concatenate([jnp.tile(sin_h, (1, 2)), jnp.zeros((TM, LANES), F32)], axis=0)
    return cos_t, sin_t


def _gqa_kernel(bound_ref, q_ref, kc_ref, vtc_ref, *rest, has_lat):
    if has_lat:
        kl_ref, vtl_ref, o_ref, qm_ref, m_ref, acc_ref = rest
    else:
        o_ref, qm_ref, m_ref, acc_ref = rest
    j = pl.program_id(2)
    group = A_Q_HEADS // A_KV_HEADS
    lane = lax.broadcasted_iota(jnp.int32, (1, LANES), 1)
    hi_half = lane >= HEAD_DIM
    n_chunks = q_ref.shape[0] // ATT_QCHUNK
    no_shift = bound_ref[0] <= SAFE_SCORE_BOUND_LOG2

    def update(k_ref, vt_ref, shifted):
        kbody = min(ATT_KBODY, k_ref.shape[0])
        bodies = [(kb, h, c) for kb in range(k_ref.shape[0] // kbody) for h in range(A_Q_HEADS) for c in range(n_chunks)]

        def scores(kb, h, c):
            kvp = (h // group) // 2
            kp = k_ref[kb * kbody:(kb + 1) * kbody, kvp * LANES:(kvp + 1) * LANES]
            return _dot_nt(kp, qm_ref[h, c * ATT_QCHUNK:(c + 1) * ATT_QCHUNK, :])

        def accumulate(h, c, alpha, pv):
            cs = slice(c * ATT_QCHUNK, (c + 1) * ATT_QCHUNK)
            prev = acc_ref[h, :, cs] if alpha is None else alpha * acc_ref[h, :, cs]
            acc_ref[h, :, cs] = prev + pv

        queue = [scores(*bodies[t]) for t in range(min(ATT_LOOKAHEAD, len(bodies)))]
        pending = None
        for idx, (kb, h, c) in enumerate(bodies):
            s = queue.pop(0)
            if idx + ATT_LOOKAHEAD < len(bodies):
                queue.append(scores(*bodies[idx + ATT_LOOKAHEAD]))
            cs = slice(c * ATT_QCHUNK, (c + 1) * ATT_QCHUNK)
            kvh = h // group
            vt = vt_ref[kvh * V_GROUP:(kvh + 1) * V_GROUP, kb * kbody:(kb + 1) * kbody]
            if shifted:
                m_prev = m_ref[h, :, cs]
                m_new = jnp.maximum(m_prev, jnp.max(s, axis=0, keepdims=True))
                alpha = jnp.exp2(m_prev - m_new)
                p = jnp.exp2(s - m_new).astype(BF16)
                m_ref[h, :, cs] = m_new
            else:
                alpha = None
                p = jnp.exp2(s).astype(BF16)
            pv = _dot(vt, p)
            if pending is not None:
                accumulate(*pending)
            pending = (h, c, alpha, pv)
        accumulate(*pending)

    def update_either(k_ref, vt_ref):
        @pl.when(no_shift)
        def _plain():
            update(k_ref, vt_ref, shifted=False)

        @pl.when(jnp.logical_not(no_shift))
        def _running_max():
            update(k_ref, vt_ref, shifted=True)

    @pl.when(j == 0)
    def _first():
        m_ref[...] = jnp.full(m_ref.shape, NEG_BIG, F32)
        acc_ref[...] = jnp.zeros(acc_ref.shape, F32)
        for h in range(A_Q_HEADS):
            qp = q_ref[:, (h // 2) * LANES:(h // 2 + 1) * LANES].astype(F32)
            kv_half = (h // group) % 2
            if kv_half != h % 2:
                qp = pltpu.roll(qp, HEAD_DIM, 1)
            keep = hi_half if kv_half else jnp.logical_not(hi_half)
            qm_ref[h] = jnp.where(keep, qp, jnp.zeros_like(qp)).astype(BF16)
        update_either(kc_ref, vtc_ref)

    if has_lat:
        @pl.when(j > 0)
        def _rest():
            update_either(kl_ref, vtl_ref)

    @pl.when(j == pl.num_programs(2) - 1)
    def _final():
        for pb in range(A_Q_HEADS // 2):
            for c in range(n_chunks):
                cs = slice(c * ATT_QCHUNK, (c + 1) * ATT_QCHUNK)
                parts = []
                for h in (2 * pb, 2 * pb + 1):
                    a = acc_ref[h, :, cs]
                    parts.append(a[:HEAD_DIM] / a[HEAD_DIM:HEAD_DIM + 1])
                o_ref[cs, pb * LANES:(pb + 1) * LANES] = jnp.concatenate(parts, axis=0).T.astype(BF16)


def _gqa_scratch(tq):
    return [
        pltpu.VMEM((A_Q_HEADS, tq, LANES), BF16),
        pltpu.VMEM((A_Q_HEADS, 1, tq), F32),
        pltpu.VMEM((A_Q_HEADS, V_GROUP, tq), F32),
    ]


def _gqa(q, k, vt, q_norm, k_norm, *, bsz, seq, lc):
    n_lat = bsz * seq
    tq, tk = ATT_TQ, ATT_TK
    ctx_blk = lambda b: n_lat // lc + b
    bound = (1.05 * HEAD_DIM * Q_SCALE_LOG2 * jnp.max(jnp.abs(q_norm)) * jnp.max(jnp.abs(k_norm))).reshape(1)
    smem = pl.BlockSpec(memory_space=pltpu.SMEM)
    o = pl.pallas_call(
        functools.partial(_gqa_kernel, has_lat=True),
        grid=(bsz, seq // tq, 1 + seq // tk),
        in_specs=[
            smem,
            pl.BlockSpec((tq, A_Q_W), lambda b, i, j: (b * (seq // tq) + i, 0)),
            pl.BlockSpec((lc, A_KV_W), lambda b, i, j: (ctx_blk(b), 0)),
            pl.BlockSpec((A_KV_HEADS * V_GROUP, lc), lambda b, i, j: (0, ctx_blk(b))),
            pl.BlockSpec((tk, A_KV_W), lambda b, i, j: (b * (seq // tk) + jnp.maximum(j - 1, 0), 0)),
            pl.BlockSpec((A_KV_HEADS * V_GROUP, tk), lambda b, i, j: (0, b * (seq // tk) + jnp.maximum(j - 1, 0))),
        ],
        out_specs=pl.BlockSpec((tq, A_Q_W), lambda b, i, j: (b * (seq // tq) + i, 0)),
        out_shape=jax.ShapeDtypeStruct((n_lat, A_Q_W), BF16),
        scratch_shapes=_gqa_scratch(tq),
        compiler_params=_cparams(("arbitrary", "arbitrary", "arbitrary"), 48),
        name="gqa_latent",
    )(bound, q, k, vt, k, vt)
    o_ctx = pl.pallas_call(
        functools.partial(_gqa_kernel, has_lat=False),
        grid=(bsz, 1, 1),
        in_specs=[
            smem,
            pl.BlockSpec((lc, A_Q_W), lambda b, i, j: (ctx_blk(b), 0)),
            pl.BlockSpec((lc, A_KV_W), lambda b, i, j: (ctx_blk(b), 0)),
            pl.BlockSpec((A_KV_HEADS * V_GROUP, lc), lambda b, i, j: (0, ctx_blk(b))),
        ],
        out_specs=pl.BlockSpec((lc, A_Q_W), lambda b, i, j: (b, 0)),
        out_shape=jax.ShapeDtypeStruct((bsz * lc, A_Q_W), BF16),
        scratch_shapes=_gqa_scratch(lc),
        compiler_params=_cparams(("arbitrary", "arbitrary", "arbitrary"), 32),
        name="gqa_context",
    )(bound, q, k, vt)
    return o, o_ctx


def _dft_a_kernel(y_ref, ca_ref, tc_ref, ts_ref, d_ref, *, nbb):
    ca = ca_ref[...]
    ra = ca.shape[1]
    for j in range(nbb):
        yj = y_ref[:, j, :].astype(BF16)
        p = _dot(ca, yj)
        br = p[:ra, :B_W] - p[ra:, B_W:]
        bi = -p[:ra, B_W:] - p[ra:, :B_W]
        tc = jnp.concatenate([tc_ref[j]] * (B_W // LANES), axis=1)
        ts = jnp.concatenate([ts_ref[j]] * (B_W // LANES), axis=1)
        d_ref[:, 0, j, :] = br * tc + bi * ts
        d_ref[:, 1, j, :] = bi * tc - br * ts


def _dft_b_kernel(d_ref, cb_ref, o_ref, *, kb, scale):
    cb = cb_ref[...]
    for j in range(kb):
        o_ref[:, j, :] = _dot(cb, d_ref[j].astype(BF16)) * scale


def _dft_ctx_kernel(y_ref, cs_ref, o_ref, *, scale):
    y = y_ref[...]
    cs = cs_ref[...]
    lc = cs.shape[0]
    yc = y[:, :B_W].astype(BF16)
    ys = y[:, B_W:].astype(BF16)
    o_ref[...] = (_dot(cs[:, :lc], yc) - _dot(cs[:, lc:], ys)) * scale


def _fourier(y, *, bsz, seq, lc):
    ntok = y.shape[0]
    ra = seq // GRID_W
    nb = GRID_W
    nbb = SUBLANES
    kb = SUBLANES
    assert ntok % nb == 0 and ra % SUBLANES == 0 and (ntok // nb) % 1 == 0
    ca_c, ca_s = _cos_sin(ra)
    ca = _const_bf16(np.concatenate([ca_c, ca_s], axis=0))
    th = 2.0 * np.pi * np.outer(np.arange(nb), np.arange(ra)) / seq
    tc = jnp.asarray(np.broadcast_to(np.cos(th)[:, :, None], (nb, ra, LANES)).astype(np.float32))
    ts = jnp.asarray(np.broadcast_to(np.sin(th)[:, :, None], (nb, ra, LANES)).astype(np.float32))
    cb_c, cb_s = _cos_sin(nb)
    cb = _const_bf16(np.concatenate([cb_c, cb_s], axis=1))

    y3 = y.reshape(ntok // nb, nb, 2 * B_W)
    d = pl.pallas_call(
        functools.partial(_dft_a_kernel, nbb=nbb),
        grid=(bsz, nb // nbb),
        in_specs=[
            pl.BlockSpec((ra, nbb, 2 * B_W), lambda b, i: (b, i, 0)),
            pl.BlockSpec((2 * ra, ra), lambda b, i: (0, 0)),
            pl.BlockSpec((nbb, ra, LANES), lambda b, i: (i, 0, 0)),
            pl.BlockSpec((nbb, ra, LANES), lambda b, i: (i, 0, 0)),
        ],
        out_specs=pl.BlockSpec((None, ra, 2, nbb, B_W), lambda b, i: (b, 0, 0, i, 0)),
        out_shape=jax.ShapeDtypeStruct((bsz, ra, 2, nb, B_W), F32),
        compiler_params=_cparams(("arbitrary", "arbitrary"), 32),
        name="fourier_rows",
    )(y3, ca, tc, ts)
    d = d.reshape(bsz, ra, 2 * nb, B_W)

    scale = 1.0 / math.sqrt(seq * B_GROUP_DIM)
    n_out = bsz * nb
    z3 = pl.pallas_call(
        functools.partial(_dft_b_kernel, kb=kb, scale=scale),
        grid=(bsz, ra // kb),
        in_specs=[
            pl.BlockSpec((None, kb, 2 * nb, B_W), lambda b, i: (b, i, 0, 0)),
            pl.BlockSpec((nb, 2 * nb), lambda b, i: (0, 0)),
        ],
        out_specs=pl.BlockSpec((nb, kb, B_W), lambda b, i: (b, i, 0)),
        out_shape=jax.ShapeDtypeStruct((n_out, ra, B_W), F32),
        compiler_params=_cparams(("arbitrary", "arbitrary"), 32),
        name="fourier_cols",
    )(d, cb)
    z = z3.reshape(bsz * seq, B_W)

    cc, cs_ = _cos_sin(lc)
    csm = _const_bf16(np.concatenate([cc, cs_], axis=1))
    ctx_blk = lambda b: (bsz * seq) // lc + b
    z_ctx = pl.pallas_call(
        functools.partial(_dft_ctx_kernel, scale=1.0 / math.sqrt(lc * B_GROUP_DIM)),
        grid=(bsz,),
        in_specs=[
            pl.BlockSpec((lc, 2 * B_W), lambda b: (ctx_blk(b), 0)),
            pl.BlockSpec((lc, 2 * lc), lambda b: (0, 0)),
        ],
        out_specs=pl.BlockSpec((lc, B_W), lambda b: (b, 0)),
        out_shape=jax.ShapeDtypeStruct((bsz * lc, B_W), F32),
        compiler_params=_cparams(("arbitrary",), 32),
        name="fourier_context",
    )(y, csm)
    return z, z_ctx


def _values_with_ones_t(v, n_heads):
    vt = v.T
    ones = jnp.ones((ONES_ROWS, vt.shape[1]), F32)
    groups = []
    for h in range(n_heads):
        groups += [vt[h * HEAD_DIM:(h + 1) * HEAD_DIM], ones]
    return jnp.concatenate(groups, axis=0)


def _inproj_cd_kernel(x_ref, mod_ref, w_ref, u_ref, q_ref, k_ref, vt_ref):
    x = x_ref[...]
    m = mod_ref[...]
    h = _modulate(x, m[3:4], m[4:5]).astype(BF16)
    y = _dot(h, w_ref[...].astype(BF16))
    c = C_CHANNELS
    u_ref[...] = y[:, :c] * _sigmoid(y[:, c:2 * c])
    q_ref[...] = (y[:, 2 * c:2 * c + D_W] * Q_SCALE_LOG2).astype(BF16)
    k_ref[...] = y[:, 2 * c + D_W:2 * c + 2 * D_W].astype(BF16)
    vt_ref[...] = _values_with_ones_t(y[:, 2 * c + 2 * D_W:], D_HEADS).astype(BF16)


def _inproj_cd(x, mod_l, w_in, *, n_tiles, mod_row):
    ntok, d = x.shape
    n_in = w_in.shape[1]
    row = lambda i: (i, 0)
    return pl.pallas_call(
        _inproj_cd_kernel,
        grid=(n_tiles,),
        in_specs=[
            pl.BlockSpec((TM, d), row),
            pl.BlockSpec((None, N_MOD, d), lambda i: (mod_row(i), 0, 0)),
            pl.BlockSpec((d, n_in), lambda i: (0, 0)),
        ],
        out_specs=[
            pl.BlockSpec((TM, C_CHANNELS), row),
            pl.BlockSpec((TM, D_W), row),
            pl.BlockSpec((TM, D_W), row),
            pl.BlockSpec((D_HEADS * V_GROUP, TM), lambda i: (0, i)),
        ],
        out_shape=[
            jax.ShapeDtypeStruct((ntok, C_CHANNELS), F32),
            jax.ShapeDtypeStruct((ntok, D_W), BF16),
            jax.ShapeDtypeStruct((ntok, D_W), BF16),
            jax.ShapeDtypeStruct((D_HEADS * V_GROUP, ntok), BF16),
        ],
        compiler_params=_cparams(("arbitrary",), 40),
        name="in_proj_cd",
    )(x, mod_l, w_in)


def _conv_kernel(prev_ref, cur_ref, next_ref, w_ref, b_ref, lnw_ref, lnb_ref, o_ref, buf_ref, shift_ref, *,
                 tiles_per_seq):
    i = pl.program_id(1)
    halo = CONV_HALO
    tm = cur_ref.shape[0]
    zeros = jnp.zeros((halo, cur_ref.shape[1]), F32)
    buf_ref[0:halo, :] = jnp.where(i > 0, prev_ref[...], zeros)
    buf_ref[halo:halo + tm, :] = cur_ref[...]
    buf_ref[halo + tm:, :] = jnp.where(i < tiles_per_seq - 1, next_ref[...], zeros)
    w = w_ref[...]
    off = halo - C_KERNEL // 2
    y = None
    for r in range(SUBLANES):
        taps = [t for t in range(C_KERNEL) if (t + off) % SUBLANES == r]
        part = None
        for t in taps:
            start = t + off - r
            term = buf_ref[start:start + tm + SUBLANES, :] * w[t:t + 1]
            part = term if part is None else part + term
        if r == 0:
            shifted = part[:tm]
        else:
            shift_ref[...] = part
            shifted = shift_ref[r:r + tm, :]
        y = shifted if y is None else y + shifted
    y = y + b_ref[...]
    mu = jnp.mean(y, axis=-1, keepdims=True)
    yc = y - mu
    var = jnp.mean(yc * yc, axis=-1, keepdims=True)
    z = (yc * lax.rsqrt(var + NORM_EPS)) * lnw_ref[...] + lnb_ref[...]
    o_ref[...] = (z * _sigmoid(z)).astype(BF16)


def _conv_module(u, dw_w, dw_b, ln_w, ln_b, *, bsz, seq):
    c = u.shape[1]
    tiles = seq // TM
    hpt = TM // CONV_HALO
    n_halo_blocks = u.shape[0] // CONV_HALO
    cur = lambda b, i: (b * tiles + i, 0)
    prev = lambda b, i: (jnp.maximum((b * tiles + i) * hpt - 1, 0), 0)
    nxt = lambda b, i: (jnp.minimum((b * tiles + i + 1) * hpt, n_halo_blocks - 1), 0)
    vec = lambda b, i: (0, 0)
    return pl.pallas_call(
        functools.partial(_conv_kernel, tiles_per_seq=tiles),
        grid=(bsz, tiles),
        in_specs=[
            pl.BlockSpec((CONV_HALO, c), prev),
            pl.BlockSpec((TM, c), cur),
            pl.BlockSpec((CONV_HALO, c), nxt),
            pl.BlockSpec((C_KERNEL, c), vec),
            pl.BlockSpec((1, c), vec),
            pl.BlockSpec((1, c), vec),
            pl.BlockSpec((1, c), vec),
        ],
        out_specs=pl.BlockSpec((TM, c), cur),
        out_shape=jax.ShapeDtypeStruct((bsz * seq, c), BF16),
        scratch_shapes=[pltpu.VMEM((TM + 2 * CONV_HALO, c), F32), pltpu.VMEM((TM + SUBLANES, c), F32)],
        compiler_params=_cparams(("arbitrary", "arbitrary"), 32),
        name="conv_module",
    )(u, u, u, dw_w, dw_b.reshape(1, c), ln_w.reshape(1, c), ln_b.reshape(1, c))


def _na_kernel(q_ref, k0_ref, k1_ref, k2_ref, vt0_ref, vt1_ref, vt2_ref, kc_ref, vtc_ref, bias_ref, o_ref):
    lane = lax.broadcasted_iota(jnp.int32, (1, LANES), 1)
    hi_half = lane >= HEAD_DIM

    def scores(h):
        sl = slice((h // 2) * LANES, (h // 2 + 1) * LANES)
        qp = q_ref[:, sl]
        keep = hi_half if h % 2 else jnp.logical_not(hi_half)
        qm = jnp.where(keep, qp, jnp.zeros_like(qp))
        kn = jnp.concatenate([k0_ref[:, sl], k1_ref[:, sl], k2_ref[:, sl]], axis=0)
        return _dot_nt(kn, qm) + bias_ref[h], _dot_nt(kc_ref[:, sl], qm)

    lookahead = min(NA_LOOKAHEAD, D_HEADS)
    queue = [scores(h) for h in range(lookahead)]
    outs = []
    for h in range(D_HEADS):
        s_nb, s_cx = queue.pop(0)
        if h + lookahead < D_HEADS:
            queue.append(scores(h + lookahead))
        mx = jnp.maximum(jnp.max(s_nb, axis=0, keepdims=True), jnp.max(s_cx, axis=0, keepdims=True))
        p = jnp.concatenate([jnp.exp2(s_nb - mx).astype(BF16), jnp.exp2(s_cx - mx).astype(BF16)], axis=0)
        rs = slice(h * V_GROUP, (h + 1) * V_GROUP)
        vt = jnp.concatenate([vt0_ref[rs, :], vt1_ref[rs, :], vt2_ref[rs, :], vtc_ref[rs, :]], axis=1)
        outs.append(_dot(vt, p))
    for pb in range(D_HEADS // 2):
        parts = [outs[h][:HEAD_DIM] / outs[h][HEAD_DIM:HEAD_DIM + 1] for h in (2 * pb, 2 * pb + 1)]
        o_ref[:, pb * LANES:(pb + 1) * LANES] = jnp.concatenate(parts, axis=0).T.astype(BF16)


def _na_bias_kernel(rpb_ref, o_ref, *, rows):
    h = pl.program_id(0)
    n_ro, n_co = 2 * NA_WIN_ROWS - 1, 2 * NA_WIN_COLS - 1
    kc = lax.broadcasted_iota(jnp.int32, (GRID_W, LANES), 0)
    lane = lax.broadcasted_iota(jnp.int32, (GRID_W, LANES), 1)
    qc = lane % GRID_W
    d = kc - qc + (NA_WIN_COLS - 1)
    cs = jnp.clip(qc - NA_WIN_COLS // 2, 0, GRID_W - NA_WIN_COLS)
    col_ok = (kc >= cs) & (kc < cs + NA_WIN_COLS)
    neg = jnp.full((GRID_W, LANES), NEG_BIG, F32)
    log2e = math.log2(math.e)
    band = []
    for ro in range(n_ro):
        acc = neg
        for co in range(n_co):
            acc = jnp.where(d == co, rpb_ref[(h * n_ro + ro) * n_co + co] * log2e, acc)
        band.append(jnp.where(col_ok, acc, neg))
    qr = NA_QROWS
    for kind, r0 in enumerate((0, qr, rows - qr)):
        for b in range(3 * qr):
            rk = r0 - qr + b
            for pair in range(qr // 2):
                halves = []
                for a in (2 * pair, 2 * pair + 1):
                    r = r0 + a
                    start = min(max(r - NA_WIN_ROWS // 2, 0), rows - NA_WIN_ROWS)
                    ok = start <= rk < start + NA_WIN_ROWS and 0 <= rk < rows
                    halves.append(band[rk - r + NA_WIN_ROWS - 1] if ok else neg)
                o_ref[kind, b * GRID_W:(b + 1) * GRID_W, pair * LANES:(pair + 1) * LANES] = jnp.where(
                    lane < GRID_W, halves[0], halves[1])


def _na_bias(rpb, rows):
    heads = rpb.shape[0]
    tq = NA_QROWS * GRID_W
    return pl.pallas_call(
        functools.partial(_na_bias_kernel, rows=rows),
        grid=(heads,),
        in_specs=[pl.BlockSpec(memory_space=pltpu.SMEM)],
        out_specs=pl.BlockSpec((3, None, 3 * tq, tq), lambda h: (0, h, 0, 0)),
        out_shape=jax.ShapeDtypeStruct((3, heads, 3 * tq, tq), F32),
        compiler_params=_cparams(("arbitrary",), 32),
        name="na_bias",
    )(rpb.reshape(-1))


def _na(q, k, vt, rpb, *, bsz, seq, lc):
    rows = seq // GRID_W
    tq = NA_QROWS * GRID_W
    nblk = seq // tq
    assert rows >= NA_WIN_ROWS and rows % NA_QROWS == 0 and NA_WIN_ROWS == 2 * NA_QROWS and nblk >= 3
    assert lc == tq
    bias = _na_bias(rpb, rows)
    n_lat = bsz * seq
    qmap = lambda b, i: (b * nblk + i, 0)

    def near(off, transposed):
        def index(b, i):
            blk = b * nblk + jnp.clip(i + off, 0, nblk - 1)
            return (0, blk) if transposed else (blk, 0)
        return index

    kind = lambda b, i: (jnp.where(i == 0, 0, jnp.where(i == nblk - 1, 2, 1)), 0, 0, 0)
    rows_blk = lambda m: pl.BlockSpec((tq, D_W), m)
    cols_blk = lambda m: pl.BlockSpec((D_HEADS * V_GROUP, tq), m)
    return pl.pallas_call(
        _na_kernel,
        grid=(bsz, nblk),
        in_specs=[
            rows_blk(qmap),
            rows_blk(near(-1, False)), rows_blk(near(0, False)), rows_blk(near(1, False)),
            cols_blk(near(-1, True)), cols_blk(near(0, True)), cols_blk(near(1, True)),
            rows_blk(lambda b, i: (n_lat // lc + b, 0)),
            cols_blk(lambda b, i: (0, n_lat // lc + b)),
            pl.BlockSpec((None, D_HEADS, 3 * tq, tq), kind),
        ],
        out_specs=rows_blk(qmap),
        out_shape=jax.ShapeDtypeStruct((n_lat, D_W), BF16),
        compiler_params=_cparams(("arbitrary", "arbitrary"), 48),
        name="neighbourhood_attention",
    )(q, k, k, k, vt, vt, vt, k, vt, bias)


def kernel(x, c, ctx, c_ctx, w_mod, b_mod, ffn_w_gate, ffn_w_up, ffn_w_down, ab_w_in, ab_w_out, ab_q_norm,
           ab_k_norm, cd_w_in, cd_w_out, cd_dw_w, cd_dw_b, cd_ln_w, cd_ln_b, cd_rpb, final_norm):
    bsz, seq, d = x.shape
    lc = ctx.shape[1]
    depth = w_mod.shape[0]
    assert depth == 2, "layer 0 (A/B mixer) updates the context, layer 1 (C/D mixer) is the last layer"
    assert bsz * lc == TM and seq % TM == 0 and seq % ATT_TQ == 0 and seq % ATT_TK == 0
    n_lat_tiles = bsz * seq // TM
    n_all_tiles = n_lat_tiles + 1
    tiles_per_seq = seq // TM
    mod_row = lambda i: jnp.where(i == n_lat_tiles, bsz, i // tiles_per_seq)
    rope_row = lambda i: jnp.where(i == n_lat_tiles, tiles_per_seq, i % tiles_per_seq)

    mod = _mod_params(c, c_ctx, w_mod, b_mod)
    cos_t, sin_t = _rope_tables(seq)

    ffn_w = (ffn_w_gate, ffn_w_up, ffn_w_down)

    xt = x.reshape(bsz * seq, d)
    for layer in range(depth):
        last = layer == depth - 1
        i = layer // 2
        mod_l = mod[layer]
        xt = _ffn(xt, mod_l, ffn_w, layer, 0, base=0, n_tiles=n_all_tiles, mod_row=mod_row,
                  x_ctx=ctx.reshape(bsz * lc, d) if layer == 0 else None)
        if layer % 2 == 0:
            q, k, vt, y = _inproj_ab(xt, mod_l, ab_w_in[i], ab_q_norm[i], ab_k_norm[i], cos_t, sin_t,
                                     n_tiles=n_all_tiles, mod_row=mod_row, rope_row=rope_row)
            a1, a1_ctx = _gqa(q, k, vt, ab_q_norm[i], ab_k_norm[i], bsz=bsz, seq=seq, lc=lc)
            a2, a2_ctx = _fourier(y, bsz=bsz, seq=seq, lc=lc)
            mixer = (a1, a2, ab_w_out[i], a1_ctx, a2_ctx)
        else:
            u, q, k, vt = _inproj_cd(xt, mod_l, cd_w_in[i], n_tiles=n_all_tiles, mod_row=mod_row)
            a1 = _conv_module(u, cd_dw_w[i], cd_dw_b[i], cd_ln_w[i], cd_ln_b[i], bsz=bsz, seq=seq)
            a2 = _na(q, k, vt, cd_rpb[i], bsz=bsz, seq=seq, lc=lc)
            mixer = (a1, a2, cd_w_out[i], None, None)
        assert last == (mixer[3] is None)
        xt = _ffn(xt, mod_l, ffn_w, layer, 1, base=6, n_tiles=n_lat_tiles if last else n_all_tiles, mod_row=mod_row,
                  final_norm=final_norm if last else None, mixer=mixer)
    return xt.reshape(bsz, seq, d)
```

```python
import functools
import math

import numpy as np
import jax
import jax.numpy as jnp
from jax import lax
from jax.experimental import pallas as pl
from jax.experimental.pallas import tpu as pltpu

F32 = jnp.float32
BF16 = jnp.bfloat16

GRID_W = 64
HEAD_DIM = 64
A_Q_HEADS = 12
A_KV_HEADS = 4
B_GROUPS = 4
B_GROUP_DIM = 64
C_CHANNELS = 512
C_KERNEL = 31
D_HEADS = 8
NA_WIN_ROWS = 8
NA_WIN_COLS = 16
ROPE_THETA = 10000.0
NORM_EPS = 1e-6
N_MOD = 9
FFN_RES_WEIGHT = 0.5

A_Q_W = A_Q_HEADS * HEAD_DIM
A_KV_W = A_KV_HEADS * HEAD_DIM
B_W = B_GROUPS * B_GROUP_DIM
D_W = D_HEADS * HEAD_DIM

LANES = 128
SUBLANES = 8
MXU_DIM = 256

TM = 512
FFN_CHUNK = 256
ATT_TQ = 1024
ATT_TK = 1024
ATT_KBODY = 512
NA_QROWS = 4
NA_LOOKAHEAD = 3
CONV_HALO = 16
ATT_QCHUNK = 256
ATT_LOOKAHEAD = 4
SAFE_SCORE_BOUND_LOG2 = 100.0
ONES_ROWS = 16
V_GROUP = HEAD_DIM + ONES_ROWS
NEG_BIG = -1e30
Q_SCALE_LOG2 = (HEAD_DIM ** -0.5) * math.log2(math.e)

_MiB = 1 << 20


def _cparams(sem, vmem_mib):
    return pltpu.CompilerParams(dimension_semantics=sem, vmem_limit_bytes=vmem_mib * _MiB)


def _dot(a, b):
    return jnp.dot(a, b, preferred_element_type=F32)


def _dot_nt(a, b):
    return lax.dot_general(a, b, (((1,), (1,)), ((), ())), preferred_element_type=F32)


def _sigmoid(x):
    return 1.0 / (1.0 + jnp.exp(-x))


def _cos_sin(n):
    idx = np.arange(n)
    ang = 2.0 * np.pi * (np.outer(idx, idx) % n) / n
    return np.cos(ang), np.sin(ang)


def _const_bf16(a):
    return jnp.asarray(np.asarray(a, np.float32)).astype(BF16)


def _modulate(x, shift, scale):
    ms = jnp.mean(x * x, axis=-1, keepdims=True)
    return (x * lax.rsqrt(ms + NORM_EPS)) * (1.0 + scale) + shift


def _mod_kernel(ct_ref, w_ref, b_ref, o_ref, *, n_rows):
    ct = ct_ref[...]
    a = ct * _sigmoid(ct)
    w = w_ref[...]
    rows = [jnp.sum(w * a[:, r:r + 1], axis=0, keepdims=True) for r in range(n_rows)]
    rows.append(jnp.zeros((SUBLANES - n_rows, w.shape[1]), F32))
    o_ref[...] = jnp.concatenate(rows, axis=0) + b_ref[...]


def _mod_params(c, c_ctx, w_mod, b_mod):
    depth, d, nmd = w_mod.shape
    bsz = c.shape[0]
    n_rows = bsz + 1
    assert n_rows <= SUBLANES
    ct = jnp.concatenate([c, c_ctx[None], jnp.zeros((SUBLANES - n_rows, d), F32)], axis=0).T
    tn = d
    out = pl.pallas_call(
        functools.partial(_mod_kernel, n_rows=n_rows),
        grid=(depth, nmd // tn),
        in_specs=[
            pl.BlockSpec((d, SUBLANES), lambda l, n: (0, 0)),
            pl.BlockSpec((None, d, tn), lambda l, n: (l, 0, n)),
            pl.BlockSpec((None, 1, tn), lambda l, n: (l, 0, n)),
        ],
        out_specs=pl.BlockSpec((None, SUBLANES, tn), lambda l, n: (l, 0, n)),
        out_shape=jax.ShapeDtypeStruct((depth, SUBLANES, nmd), F32),
        compiler_params=_cparams(("arbitrary", "arbitrary"), 32),
        name="adaln_params",
    )(ct, w_mod, b_mod.reshape(depth, 1, nmd))
    return out.reshape(depth, SUBLANES, N_MOD, d)


def _pick_rows(lat_ref, ctx_ref, n_lat_tiles):
    return jnp.where(pl.program_id(0) == n_lat_tiles, ctx_ref[...], lat_ref[...])


def _ffn_kernel(*refs, base, final, n_chunks, ctx_tile, x_has_ctx, mixer):
    refs = list(refs)
    x_ref = refs.pop(0)
    x = x_ref[...] if not x_has_ctx else _pick_rows(x_ref, refs.pop(0), ctx_tile)
    if mixer:
        a1_ref, a2_ref = refs.pop(0), refs.pop(0)
        if ctx_tile is None:
            a1, a2 = a1_ref[...], a2_ref[...]
        else:
            a1 = _pick_rows(a1_ref, refs.pop(0), ctx_tile)
            a2 = _pick_rows(a2_ref, refs.pop(0), ctx_tile)
        w1_ref, w2_ref = refs.pop(0), refs.pop(0)
    mod_ref, wg_ref, wu_ref, wd_ref = refs[:4]
    if final:
        fn_ref, o_ref, acc_ref = refs[4:]
    else:
        o_ref, acc_ref = refs[4:]
    m = mod_ref[...]
    if mixer:
        x = x + m[5:6] * (_dot(a1.astype(BF16), w1_ref[...].astype(BF16)) +
                          _dot(a2.astype(BF16), w2_ref[...].astype(BF16)))
    h = _modulate(x, m[base:base + 1], m[base + 1:base + 2]).astype(BF16)
    for c in range(n_chunks):
        sl = slice(c * FFN_CHUNK, (c + 1) * FFN_CHUNK)
        g = _dot(h, wg_ref[:, sl].astype(BF16))
        u = _dot(h, wu_ref[:, sl].astype(BF16))
        a = ((g * _sigmoid(g)) * u).astype(BF16)
        part = _dot(a, wd_ref[sl, :].astype(BF16))
        if c == 0:
            acc_ref[...] = part
        else:
            acc_ref[...] += part
    y = x + (FFN_RES_WEIGHT * m[base + 2:base + 3]) * acc_ref[...]
    if final:
        ms = jnp.mean(y * y, axis=-1, keepdims=True)
        y = (y * lax.rsqrt(ms + NORM_EPS)) * fn_ref[...]
    o_ref[...] = y


def _ffn(x, mod_l, ffn_w, layer, half, *, base, n_tiles, mod_row, final_norm=None, x_ctx=None, mixer=None):
    d = x.shape[1]
    wg, wu, wd = ffn_w
    dff = wg.shape[-1]
    assert dff % FFN_CHUNK == 0
    final = final_norm is not None
    resident = dict(pipeline_mode=pl.Buffered(1))
    const = lambda i: (0, 0)
    last_lat = n_tiles - 2
    lat_only = lambda i: (jnp.minimum(i, last_lat), 0)
    ctx_tile = None
    in_specs = [pl.BlockSpec((TM, d), lambda i: (i, 0))]
    args = [x]
    if x_ctx is not None:
        ctx_tile = n_tiles - 1
        assert x.shape[0] == ctx_tile * TM and x_ctx.shape == (TM, d)
        in_specs = [pl.BlockSpec((TM, d), lat_only), pl.BlockSpec((TM, d), const)]
        args.append(x_ctx)
    if mixer is not None:
        a1, a2, w_out, w_idx, a1_ctx, a2_ctx = mixer
        d1, d2 = a1.shape[1], a2.shape[1]
        assert d1 % d2 == 0
        if a1_ctx is None:
            in_specs += [pl.BlockSpec((TM, d1), lambda i: (i, 0)), pl.BlockSpec((TM, d2), lambda i: (i, 0))]
            args += [a1, a2]
        else:
            ctx_tile = n_tiles - 1
            assert a1.shape[0] == ctx_tile * TM and a1_ctx.shape[0] == TM and a2_ctx.shape[0] == TM
            in_specs += [pl.BlockSpec((TM, d1), lat_only), pl.BlockSpec((TM, d2), lat_only),
                         pl.BlockSpec((TM, d1), const), pl.BlockSpec((TM, d2), const)]
            args += [a1, a2, a1_ctx, a2_ctx]
        in_specs += [pl.BlockSpec((None, d1, d), lambda i: (w_idx, 0, 0), **resident),
                     pl.BlockSpec((None, d2, d), lambda i: (w_idx, d1 // d2, 0), **resident)]
        args += [w_out, w_out]
    sel = lambda i: (layer, half, 0, 0)
    in_specs += [
        pl.BlockSpec((None, N_MOD, d), lambda i: (mod_row(i), 0, 0)),
        pl.BlockSpec((None, None, d, dff), sel, **resident),
        pl.BlockSpec((None, None, d, dff), sel, **resident),
        pl.BlockSpec((None, None, dff, d), sel, **resident),
    ]
    args += [mod_l, wg, wu, wd]
    if final:
        in_specs.append(pl.BlockSpec((1, d), const))
        args.append(final_norm.reshape(1, d))
    return pl.pallas_call(
        functools.partial(_ffn_kernel, base=base, final=final, n_chunks=dff // FFN_CHUNK, ctx_tile=ctx_tile,
                          x_has_ctx=x_ctx is not None, mixer=mixer is not None),
        grid=(n_tiles,),
        in_specs=in_specs,
        out_specs=pl.BlockSpec((TM, d), lambda i: (i, 0)),
        out_shape=jax.ShapeDtypeStruct((n_tiles * TM, d), F32),
        scratch_shapes=[pltpu.VMEM((TM, d), F32)],
        compiler_params=_cparams(("arbitrary",), 60),
        name=("mix_ffn_final" if final else "mix_ffn") if mixer is not None else "ffn",
    )(*args)


def _swap_halves(z, low_half):
    n = z.shape[1]
    return jnp.where(low_half, pltpu.roll(z, n - HEAD_DIM // 2, 1), pltpu.roll(z, HEAD_DIM // 2, 1))


def _head_sumsq(z, ones_bd):
    zz = z * z
    hi = zz.astype(BF16)
    lo = (zz - hi.astype(F32)).astype(BF16)
    return _dot(hi, ones_bd) + _dot(lo, ones_bd)


def _inproj_ab_kernel(x_ref, mod_ref, w_ref, qn_ref, kn_ref, cos_ref, sin_ref, ones_ref, dft_ref,
                      q_ref, k_ref, vt_ref, y_ref):
    x = x_ref[...]
    m = mod_ref[...]
    h = _modulate(x, m[3:4], m[4:5]).astype(BF16)
    y = _dot(h, w_ref[...].astype(BF16))
    ones_bd = ones_ref[...]
    cos = jnp.concatenate([cos_ref[...]] * 2, axis=1)
    sin = jnp.concatenate([sin_ref[...]] * 2, axis=1)
    lane = lax.broadcasted_iota(jnp.int32, (1, MXU_DIM), 1)
    low_half = (lane % HEAD_DIM) < (HEAD_DIM // 2)

    def norm_rope(z, nw):
        ss = _head_sumsq(z, ones_bd)
        zn = (z * lax.rsqrt(ss * (1.0 / HEAD_DIM) + NORM_EPS)) * nw
        return zn * cos + _swap_halves(zn, low_half) * sin

    qn = qn_ref[...]
    kn = kn_ref[...]
    for blk in range(A_Q_W // MXU_DIM):
        sl = slice(blk * MXU_DIM, (blk + 1) * MXU_DIM)
        q_ref[:, sl] = (norm_rope(y[:, sl], qn) * Q_SCALE_LOG2).astype(BF16)
    k_ref[...] = norm_rope(y[:, A_Q_W:A_Q_W + A_KV_W], kn).astype(BF16)
    vt_ref[...] = _values_with_ones_t(y[:, A_Q_W + A_KV_W:A_Q_W + 2 * A_KV_W], A_KV_HEADS).astype(BF16)
    f = y[:, A_Q_W + 2 * A_KV_W:].astype(BF16)
    y_ref[...] = _dot(f, dft_ref[...])


def _inproj_ab(x, mod_l, w_in, w_idx, q_norm, k_norm, cos_t, sin_t, *, n_tiles, mod_row, rope_row):
    ntok, d = x.shape
    n_in = w_in.shape[-1]
    ones_bd = _const_bf16(np.kron(np.eye(MXU_DIM // HEAD_DIM), np.ones((HEAD_DIM, HEAD_DIM))))
    gc, gs = _cos_sin(B_GROUP_DIM)
    eye = np.eye(B_GROUPS)
    dft = _const_bf16(np.concatenate([np.kron(eye, gc), np.kron(eye, gs)], axis=1))
    qn = jnp.tile(q_norm, MXU_DIM // HEAD_DIM).reshape(1, MXU_DIM)
    kn = jnp.tile(k_norm, MXU_DIM // HEAD_DIM).reshape(1, MXU_DIM)
    const = lambda i: (0, 0)
    row = lambda i: (i, 0)
    return pl.pallas_call(
        _inproj_ab_kernel,
        grid=(n_tiles,),
        in_specs=[
            pl.BlockSpec((TM, d), row),
            pl.BlockSpec((None, N_MOD, d), lambda i: (mod_row(i), 0, 0)),
            pl.BlockSpec((None, d, n_in), lambda i: (w_idx, 0, 0)),
            pl.BlockSpec((1, MXU_DIM), const),
            pl.BlockSpec((1, MXU_DIM), const),
            pl.BlockSpec((TM, LANES), lambda i: (rope_row(i), 0)),
            pl.BlockSpec((TM, LANES), lambda i: (rope_row(i), 0)),
            pl.BlockSpec((MXU_DIM, MXU_DIM), const),
            pl.BlockSpec((B_W, 2 * B_W), const),
        ],
        out_specs=[
            pl.BlockSpec((TM, A_Q_W), row),
            pl.BlockSpec((TM, A_KV_W), row),
            pl.BlockSpec((A_KV_HEADS * V_GROUP, TM), lambda i: (0, i)),
            pl.BlockSpec((TM, 2 * B_W), row),
        ],
        out_shape=[
            jax.ShapeDtypeStruct((ntok, A_Q_W), BF16),
            jax.ShapeDtypeStruct((ntok, A_KV_W), BF16),
            jax.ShapeDtypeStruct((A_KV_HEADS * V_GROUP, ntok), BF16),
            jax.ShapeDtypeStruct((ntok, 2 * B_W), F32),
        ],
        compiler_params=_cparams(("arbitrary",), 40),
        name="in_proj_ab",
    )(x, mod_l, w_in, qn, kn, cos_t, sin_t, ones_bd, dft)


def _rope_tables(seq):
    t = jnp.arange(seq)
    row = (t // GRID_W).astype(F32)
    col = (t % GRID_W).astype(F32)
    n_ax = HEAD_DIM // 4
    inv = ROPE_THETA ** (-jnp.arange(n_ax, dtype=F32) / n_ax)
    ang = jnp.concatenate([row[:, None] * inv, col[:, None] * inv], axis=-1)
    cos, sin = jnp.cos(ang), jnp.sin(ang)
    cos_h = jnp.concatenate([cos, cos], axis=-1)
    sin_h = jnp.concatenate([-sin, sin], axis=-1)
    cos_t = jnp.concatenate([jnp.tile(cos_h, (1, 2)), jnp.ones((TM, LANES), F32)], axis=0)
    sin_t = jnp.concatenate([jnp.tile(sin_h, (1, 2)), jnp.zeros((TM, LANES), F32)], axis=0)
    return cos_t, sin_t


def _gqa_kernel(bound_ref, q_ref, kc_ref, vtc_ref, *rest, has_lat):
    if has_lat:
        kl_ref, vtl_ref, o_ref, qm_ref, m_ref, acc_ref = rest
    else:
        o_ref, qm_ref, m_ref, acc_ref = rest
    j = pl.program_id(2)
    group = A_Q_HEADS // A_KV_HEADS
    lane = lax.broadcasted_iota(jnp.int32, (1, LANES), 1)
    hi_half = lane >= HEAD_DIM
    n_chunks = q_ref.shape[0] // ATT_QCHUNK
    no_shift = bound_ref[0] <= SAFE_SCORE_BOUND_LOG2

    def update(k_ref, vt_ref, shifted):
        kbody = min(ATT_KBODY, k_ref.shape[0])
        bodies = [(kb, h, c) for kb in range(k_ref.shape[0] // kbody) for h in range(A_Q_HEADS) for c in range(n_chunks)]

        def scores(kb, h, c):
            kvp = (h // group) // 2
            kp = k_ref[kb * kbody:(kb + 1) * kbody, kvp * LANES:(kvp + 1) * LANES]
            return _dot_nt(kp, qm_ref[h, c * ATT_QCHUNK:(c + 1) * ATT_QCHUNK, :])

        def accumulate(h, c, alpha, pv):
            cs = slice(c * ATT_QCHUNK, (c + 1) * ATT_QCHUNK)
            prev = acc_ref[h, :, cs] if alpha is None else alpha * acc_ref[h, :, cs]
            acc_ref[h, :, cs] = prev + pv

        queue = [scores(*bodies[t]) for t in range(min(ATT_LOOKAHEAD, len(bodies)))]
        pending = None
        for idx, (kb, h, c) in enumerate(bodies):
            s = queue.pop(0)
            if idx + ATT_LOOKAHEAD < len(bodies):
                queue.append(scores(*bodies[idx + ATT_LOOKAHEAD]))
            cs = slice(c * ATT_QCHUNK, (c + 1) * ATT_QCHUNK)
            kvh = h // group
            vt = vt_ref[kvh * V_GROUP:(kvh + 1) * V_GROUP, kb * kbody:(kb + 1) * kbody]
            if shifted:
                m_prev = m_ref[h, :, cs]
                m_new = jnp.maximum(m_prev, jnp.max(s, axis=0, keepdims=True))
                alpha = jnp.exp2(m_prev - m_new)
                p = jnp.exp2(s - m_new).astype(BF16)
                m_ref[h, :, cs] = m_new
            else:
                alpha = None
                p = jnp.exp2(s).astype(BF16)
            pv = _dot(vt, p)
            if pending is not None:
                accumulate(*pending)
            pending = (h, c, alpha, pv)
        accumulate(*pending)

    def update_either(k_ref, vt_ref):
        @pl.when(no_shift)
        def _plain():
            update(k_ref, vt_ref, shifted=False)

        @pl.when(jnp.logical_not(no_shift))
        def _running_max():
            update(k_ref, vt_ref, shifted=True)

    @pl.when(j == 0)
    def _first():
        m_ref[...] = jnp.full(m_ref.shape, NEG_BIG, F32)
        acc_ref[...] = jnp.zeros(acc_ref.shape, F32)
        for h in range(A_Q_HEADS):
            qp = q_ref[:, (h // 2) * LANES:(h // 2 + 1) * LANES].astype(F32)
            kv_half = (h // group) % 2
            if kv_half != h % 2:
                qp = pltpu.roll(qp, HEAD_DIM, 1)
            keep = hi_half if kv_half else jnp.logical_not(hi_half)
            qm_ref[h] = jnp.where(keep, qp, jnp.zeros_like(qp)).astype(BF16)
        update_either(kc_ref, vtc_ref)

    if has_lat:
        @pl.when(j > 0)
        def _rest():
            update_either(kl_ref, vtl_ref)

    @pl.when(j == pl.num_programs(2) - 1)
    def _final():
        for pb in range(A_Q_HEADS // 2):
            for c in range(n_chunks):
                cs = slice(c * ATT_QCHUNK, (c + 1) * ATT_QCHUNK)
                parts = []
                for h in (2 * pb, 2 * pb + 1):
                    a = acc_ref[h, :, cs]
                    parts.append(a[:HEAD_DIM] / a[HEAD_DIM:HEAD_DIM + 1])
                o_ref[cs, pb * LANES:(pb + 1) * LANES] = jnp.concatenate(parts, axis=0).T.astype(BF16)


def _gqa_scratch(tq):
    return [
        pltpu.VMEM((A_Q_HEADS, tq, LANES), BF16),
        pltpu.VMEM((A_Q_HEADS, 1, tq), F32),
        pltpu.VMEM((A_Q_HEADS, V_GROUP, tq), F32),
    ]


def _gqa(q, k, vt, q_norm, k_norm, *, bsz, seq, lc):
    n_lat = bsz * seq
    tq, tk = ATT_TQ, ATT_TK
    ctx_blk = lambda b: n_lat // lc + b
    bound = (1.05 * HEAD_DIM * Q_SCALE_LOG2 * jnp.max(jnp.abs(q_norm)) * jnp.max(jnp.abs(k_norm))).reshape(1)
    smem = pl.BlockSpec(memory_space=pltpu.SMEM)
    o = pl.pallas_call(
        functools.partial(_gqa_kernel, has_lat=True),
        grid=(bsz, seq // tq, 1 + seq // tk),
        in_specs=[
            smem,
            pl.BlockSpec((tq, A_Q_W), lambda b, i, j: (b * (seq // tq) + i, 0)),
            pl.BlockSpec((lc, A_KV_W), lambda b, i, j: (ctx_blk(b), 0)),
            pl.BlockSpec((A_KV_HEADS * V_GROUP, lc), lambda b, i, j: (0, ctx_blk(b))),
            pl.BlockSpec((tk, A_KV_W), lambda b, i, j: (b * (seq // tk) + jnp.maximum(j - 1, 0), 0)),
            pl.BlockSpec((A_KV_HEADS * V_GROUP, tk), lambda b, i, j: (0, b * (seq // tk) + jnp.maximum(j - 1, 0))),
        ],
        out_specs=pl.BlockSpec((tq, A_Q_W), lambda b, i, j: (b * (seq // tq) + i, 0)),
        out_shape=jax.ShapeDtypeStruct((n_lat, A_Q_W), BF16),
        scratch_shapes=_gqa_scratch(tq),
        compiler_params=_cparams(("arbitrary", "arbitrary", "arbitrary"), 48),
        name="gqa_latent",
    )(bound, q, k, vt, k, vt)
    o_ctx = pl.pallas_call(
        functools.partial(_gqa_kernel, has_lat=False),
        grid=(bsz, 1, 1),
        in_specs=[
            smem,
            pl.BlockSpec((lc, A_Q_W), lambda b, i, j: (ctx_blk(b), 0)),
            pl.BlockSpec((lc, A_KV_W), lambda b, i, j: (ctx_blk(b), 0)),
            pl.BlockSpec((A_KV_HEADS * V_GROUP, lc), lambda b, i, j: (0, ctx_blk(b))),
        ],
        out_specs=pl.BlockSpec((lc, A_Q_W), lambda b, i, j: (b, 0)),
        out_shape=jax.ShapeDtypeStruct((bsz * lc, A_Q_W), BF16),
        scratch_shapes=_gqa_scratch(lc),
        compiler_params=_cparams(("arbitrary", "arbitrary", "arbitrary"), 32),
        name="gqa_context",
    )(bound, q, k, vt)
    return o, o_ctx


def _dft_a_kernel(y_ref, ca_ref, tc_ref, ts_ref, d_ref, *, nbb):
    ca = ca_ref[...]
    ra = ca.shape[1]
    for j in range(nbb):
        yj = y_ref[:, j, :].astype(BF16)
        p = _dot(ca, yj)
        br = p[:ra, :B_W] - p[ra:, B_W:]
        bi = -p[:ra, B_W:] - p[ra:, :B_W]
        tc = jnp.concatenate([tc_ref[j]] * (B_W // LANES), axis=1)
        ts = jnp.concatenate([ts_ref[j]] * (B_W // LANES), axis=1)
        d_ref[:, 0, j, :] = br * tc + bi * ts
        d_ref[:, 1, j, :] = bi * tc - br * ts


def _dft_b_kernel(d_ref, cb_ref, o_ref, *, kb, scale):
    cb = cb_ref[...]
    for j in range(kb):
        o_ref[:, j, :] = _dot(cb, d_ref[j].astype(BF16)) * scale


def _dft_ctx_kernel(y_ref, cs_ref, o_ref, *, scale):
    y = y_ref[...]
    cs = cs_ref[...]
    lc = cs.shape[0]
    yc = y[:, :B_W].astype(BF16)
    ys = y[:, B_W:].astype(BF16)
    o_ref[...] = (_dot(cs[:, :lc], yc) - _dot(cs[:, lc:], ys)) * scale


def _fourier(y, *, bsz, seq, lc):
    ntok = y.shape[0]
    ra = seq // GRID_W
    nb = GRID_W
    nbb = SUBLANES
    kb = SUBLANES
    assert ntok % nb == 0 and ra % SUBLANES == 0 and (ntok // nb) % 1 == 0
    ca_c, ca_s = _cos_sin(ra)
    ca = _const_bf16(np.concatenate([ca_c, ca_s], axis=0))
    th = 2.0 * np.pi * np.outer(np.arange(nb), np.arange(ra)) / seq
    tc = jnp.asarray(np.broadcast_to(np.cos(th)[:, :, None], (nb, ra, LANES)).astype(np.float32))
    ts = jnp.asarray(np.broadcast_to(np.sin(th)[:, :, None], (nb, ra, LANES)).astype(np.float32))
    cb_c, cb_s = _cos_sin(nb)
    cb = _const_bf16(np.concatenate([cb_c, cb_s], axis=1))

    y3 = y.reshape(ntok // nb, nb, 2 * B_W)
    d = pl.pallas_call(
        functools.partial(_dft_a_kernel, nbb=nbb),
        grid=(bsz, nb // nbb),
        in_specs=[
            pl.BlockSpec((ra, nbb, 2 * B_W), lambda b, i: (b, i, 0)),
            pl.BlockSpec((2 * ra, ra), lambda b, i: (0, 0)),
            pl.BlockSpec((nbb, ra, LANES), lambda b, i: (i, 0, 0)),
            pl.BlockSpec((nbb, ra, LANES), lambda b, i: (i, 0, 0)),
        ],
        out_specs=pl.BlockSpec((None, ra, 2, nbb, B_W), lambda b, i: (b, 0, 0, i, 0)),
        out_shape=jax.ShapeDtypeStruct((bsz, ra, 2, nb, B_W), F32),
        compiler_params=_cparams(("arbitrary", "arbitrary"), 32),
        name="fourier_rows",
    )(y3, ca, tc, ts)
    d = d.reshape(bsz, ra, 2 * nb, B_W)

    scale = 1.0 / math.sqrt(seq * B_GROUP_DIM)
    n_out = bsz * nb
    z3 = pl.pallas_call(
        functools.partial(_dft_b_kernel, kb=kb, scale=scale),
        grid=(bsz, ra // kb),
        in_specs=[
            pl.BlockSpec((None, kb, 2 * nb, B_W), lambda b, i: (b, i, 0, 0)),
            pl.BlockSpec((nb, 2 * nb), lambda b, i: (0, 0)),
        ],
        out_specs=pl.BlockSpec((nb, kb, B_W), lambda b, i: (b, i, 0)),
        out_shape=jax.ShapeDtypeStruct((n_out, ra, B_W), F32),
        compiler_params=_cparams(("arbitrary", "arbitrary"), 32),
        name="fourier_cols",
    )(d, cb)
    z = z3.reshape(bsz * seq, B_W)

    cc, cs_ = _cos_sin(lc)
    csm = _const_bf16(np.concatenate([cc, cs_], axis=1))
    ctx_blk = lambda b: (bsz * seq) // lc + b
    z_ctx = pl.pallas_call(
        functools.partial(_dft_ctx_kernel, scale=1.0 / math.sqrt(lc * B_GROUP_DIM)),
        grid=(bsz,),
        in_specs=[
            pl.BlockSpec((lc, 2 * B_W), lambda b: (ctx_blk(b), 0)),
            pl.BlockSpec((lc, 2 * lc), lambda b: (0, 0)),
        ],
        out_specs=pl.BlockSpec((lc, B_W), lambda b: (b, 0)),
        out_shape=jax.ShapeDtypeStruct((bsz * lc, B_W), F32),
        compiler_params=_cparams(("arbitrary",), 32),
        name="fourier_context",
    )(y, csm)
    return z, z_ctx


def _values_with_ones_t(v, n_heads):
    vt = v.T
    ones = jnp.ones((ONES_ROWS, vt.shape[1]), F32)
    groups = []
    for h in range(n_heads):
        groups += [vt[h * HEAD_DIM:(h + 1) * HEAD_DIM], ones]
    return jnp.concatenate(groups, axis=0)


def _inproj_cd_kernel(x_ref, mod_ref, w_ref, u_ref, q_ref, k_ref, vt_ref):
    x = x_ref[...]
    m = mod_ref[...]
    h = _modulate(x, m[3:4], m[4:5]).astype(BF16)
    y = _dot(h, w_ref[...].astype(BF16))
    c = C_CHANNELS
    u_ref[...] = y[:, :c] * _sigmoid(y[:, c:2 * c])
    q_ref[...] = (y[:, 2 * c:2 * c + D_W] * Q_SCALE_LOG2).astype(BF16)
    k_ref[...] = y[:, 2 * c + D_W:2 * c + 2 * D_W].astype(BF16)
    vt_ref[...] = _values_with_ones_t(y[:, 2 * c + 2 * D_W:], D_HEADS).astype(BF16)


def _inproj_cd(x, mod_l, w_in, w_idx, *, n_tiles, mod_row):
    ntok, d = x.shape
    n_in = w_in.shape[-1]
    row = lambda i: (i, 0)
    return pl.pallas_call(
        _inproj_cd_kernel,
        grid=(n_tiles,),
        in_specs=[
            pl.BlockSpec((TM, d), row),
            pl.BlockSpec((None, N_MOD, d), lambda i: (mod_row(i), 0, 0)),
            pl.BlockSpec((None, d, n_in), lambda i: (w_idx, 0, 0)),
        ],
        out_specs=[
            pl.BlockSpec((TM, C_CHANNELS), row),
            pl.BlockSpec((TM, D_W), row),
            pl.BlockSpec((TM, D_W), row),
            pl.BlockSpec((D_HEADS * V_GROUP, TM), lambda i: (0, i)),
        ],
        out_shape=[
            jax.ShapeDtypeStruct((ntok, C_CHANNELS), F32),
            jax.ShapeDtypeStruct((ntok, D_W), BF16),
            jax.ShapeDtypeStruct((ntok, D_W), BF16),
            jax.ShapeDtypeStruct((D_HEADS * V_GROUP, ntok), BF16),
        ],
        compiler_params=_cparams(("arbitrary",), 40),
        name="in_proj_cd",
    )(x, mod_l, w_in)


def _conv_kernel(prev_ref, cur_ref, next_ref, w_ref, b_ref, lnw_ref, lnb_ref, o_ref, buf_ref, shift_ref, *,
                 tiles_per_seq):
    i = pl.program_id(1)
    halo = CONV_HALO
    tm = cur_ref.shape[0]
    zeros = jnp.zeros((halo, cur_ref.shape[1]), F32)
    buf_ref[0:halo, :] = jnp.where(i > 0, prev_ref[...], zeros)
    buf_ref[halo:halo + tm, :] = cur_ref[...]
    buf_ref[halo + tm:, :] = jnp.where(i < tiles_per_seq - 1, next_ref[...], zeros)
    w = w_ref[...]
    off = halo - C_KERNEL // 2
    y = None
    for r in range(SUBLANES):
        taps = [t for t in range(C_KERNEL) if (t + off) % SUBLANES == r]
        part = None
        for t in taps:
            start = t + off - r
            term = buf_ref[start:start + tm + SUBLANES, :] * w[t:t + 1]
            part = term if part is None else part + term
        if r == 0:
            shifted = part[:tm]
        else:
            shift_ref[...] = part
            shifted = shift_ref[r:r + tm, :]
        y = shifted if y is None else y + shifted
    y = y + b_ref[...]
    mu = jnp.mean(y, axis=-1, keepdims=True)
    yc = y - mu
    var = jnp.mean(yc * yc, axis=-1, keepdims=True)
    z = (yc * lax.rsqrt(var + NORM_EPS)) * lnw_ref[...] + lnb_ref[...]
    o_ref[...] = (z * _sigmoid(z)).astype(BF16)


def _conv_module(u, dw_w, dw_b, ln_w, ln_b, *, bsz, seq):
    c = u.shape[1]
    tiles = seq // TM
    hpt = TM // CONV_HALO
    n_halo_blocks = u.shape[0] // CONV_HALO
    cur = lambda b, i: (b * tiles + i, 0)
    prev = lambda b, i: (jnp.maximum((b * tiles + i) * hpt - 1, 0), 0)
    nxt = lambda b, i: (jnp.minimum((b * tiles + i + 1) * hpt, n_halo_blocks - 1), 0)
    vec = lambda b, i: (0, 0)
    return pl.pallas_call(
        functools.partial(_conv_kernel, tiles_per_seq=tiles),
        grid=(bsz, tiles),
        in_specs=[
            pl.BlockSpec((CONV_HALO, c), prev),
            pl.BlockSpec((TM, c), cur),
            pl.BlockSpec((CONV_HALO, c), nxt),
            pl.BlockSpec((C_KERNEL, c), vec),
            pl.BlockSpec((1, c), vec),
            pl.BlockSpec((1, c), vec),
            pl.BlockSpec((1, c), vec),
        ],
        out_specs=pl.BlockSpec((TM, c), cur),
        out_shape=jax.ShapeDtypeStruct((bsz * seq, c), BF16),
        scratch_shapes=[pltpu.VMEM((TM + 2 * CONV_HALO, c), F32), pltpu.VMEM((TM + SUBLANES, c), F32)],
        compiler_params=_cparams(("arbitrary", "arbitrary"), 32),
        name="conv_module",
    )(u, u, u, dw_w, dw_b.reshape(1, c), ln_w.reshape(1, c), ln_b.reshape(1, c))


def _na_kernel(q_ref, k0_ref, k1_ref, k2_ref, vt0_ref, vt1_ref, vt2_ref, kc_ref, vtc_ref, bias_ref, o_ref):
    lane = lax.broadcasted_iota(jnp.int32, (1, LANES), 1)
    hi_half = lane >= HEAD_DIM

    def scores(h):
        sl = slice((h // 2) * LANES, (h // 2 + 1) * LANES)
        qp = q_ref[:, sl]
        keep = hi_half if h % 2 else jnp.logical_not(hi_half)
        qm = jnp.where(keep, qp, jnp.zeros_like(qp))
        tq = qp.shape[0]
        near = [_dot_nt(k_ref[:, sl], qm) + bias_ref[h, n * tq:(n + 1) * tq, :]
                for n, k_ref in enumerate((k0_ref, k1_ref, k2_ref))]
        return near + [_dot_nt(kc_ref[:, sl], qm)]

    lookahead = min(NA_LOOKAHEAD, D_HEADS)
    queue = [scores(h) for h in range(lookahead)]
    outs = []
    for h in range(D_HEADS):
        tiles = queue.pop(0)
        if h + lookahead < D_HEADS:
            queue.append(scores(h + lookahead))
        mx = functools.reduce(jnp.maximum, [jnp.max(t, axis=0, keepdims=True) for t in tiles])
        p = jnp.concatenate([jnp.exp2(t - mx).astype(BF16) for t in tiles], axis=0)
        rs = slice(h * V_GROUP, (h + 1) * V_GROUP)
        vt = jnp.concatenate([vt0_ref[rs, :], vt1_ref[rs, :], vt2_ref[rs, :], vtc_ref[rs, :]], axis=1)
        outs.append(_dot(vt, p))
    for pb in range(D_HEADS // 2):
        parts = [outs[h][:HEAD_DIM] / outs[h][HEAD_DIM:HEAD_DIM + 1] for h in (2 * pb, 2 * pb + 1)]
        o_ref[:, pb * LANES:(pb + 1) * LANES] = jnp.concatenate(parts, axis=0).T.astype(BF16)


def _na_bias_kernel(rpb_ref, o_ref, *, rows):
    h = pl.program_id(0)
    n_ro, n_co = 2 * NA_WIN_ROWS - 1, 2 * NA_WIN_COLS - 1
    kc = lax.broadcasted_iota(jnp.int32, (GRID_W, LANES), 0)
    lane = lax.broadcasted_iota(jnp.int32, (GRID_W, LANES), 1)
    qc = lane % GRID_W
    d = kc - qc + (NA_WIN_COLS - 1)
    cs = jnp.clip(qc - NA_WIN_COLS // 2, 0, GRID_W - NA_WIN_COLS)
    col_ok = (kc >= cs) & (kc < cs + NA_WIN_COLS)
    neg = jnp.full((GRID_W, LANES), NEG_BIG, F32)
    log2e = math.log2(math.e)
    band = []
    for ro in range(n_ro):
        acc = neg
        for co in range(n_co):
            acc = jnp.where(d == co, rpb_ref[(h * n_ro + ro) * n_co + co] * log2e, acc)
        band.append(jnp.where(col_ok, acc, neg))
    qr = NA_QROWS
    for kind, r0 in enumerate((0, qr, rows - qr)):
        for b in range(3 * qr):
            rk = r0 - qr + b
            for pair in range(qr // 2):
                halves = []
                for a in (2 * pair, 2 * pair + 1):
                    r = r0 + a
                    start = min(max(r - NA_WIN_ROWS // 2, 0), rows - NA_WIN_ROWS)
                    ok = start <= rk < start + NA_WIN_ROWS and 0 <= rk < rows
                    halves.append(band[rk - r + NA_WIN_ROWS - 1] if ok else neg)
                o_ref[kind, b * GRID_W:(b + 1) * GRID_W, pair * LANES:(pair + 1) * LANES] = jnp.where(
                    lane < GRID_W, halves[0], halves[1])


def _na_bias(rpb, rows):
    heads = rpb.shape[0]
    tq = NA_QROWS * GRID_W
    return pl.pallas_call(
        functools.partial(_na_bias_kernel, rows=rows),
        grid=(heads,),
        in_specs=[pl.BlockSpec(memory_space=pltpu.SMEM)],
        out_specs=pl.BlockSpec((3, None, 3 * tq, tq), lambda h: (0, h, 0, 0)),
        out_shape=jax.ShapeDtypeStruct((3, heads, 3 * tq, tq), F32),
        compiler_params=_cparams(("arbitrary",), 32),
        name="na_bias",
    )(rpb.reshape(-1))


def _na(q, k, vt, rpb, *, bsz, seq, lc):
    rows = seq // GRID_W
    tq = NA_QROWS * GRID_W
    nblk = seq // tq
    assert rows >= NA_WIN_ROWS and rows % NA_QROWS == 0 and NA_WIN_ROWS == 2 * NA_QROWS and nblk >= 3
    assert lc == tq
    bias = _na_bias(rpb, rows)
    n_lat = bsz * seq
    qmap = lambda b, i: (b * nblk + i, 0)

    def near(off, transposed):
        def index(b, i):
            blk = b * nblk + jnp.clip(i + off, 0, nblk - 1)
            return (0, blk) if transposed else (blk, 0)
        return index

    kind = lambda b, i: (jnp.where(i == 0, 0, jnp.where(i == nblk - 1, 2, 1)), 0, 0, 0)
    rows_blk = lambda m: pl.BlockSpec((tq, D_W), m)
    cols_blk = lambda m: pl.BlockSpec((D_HEADS * V_GROUP, tq), m)
    return pl.pallas_call(
        _na_kernel,
        grid=(bsz, nblk),
        in_specs=[
            rows_blk(qmap),
            rows_blk(near(-1, False)), rows_blk(near(0, False)), rows_blk(near(1, False)),
            cols_blk(near(-1, True)), cols_blk(near(0, True)), cols_blk(near(1, True)),
            rows_blk(lambda b, i: (n_lat // lc + b, 0)),
            cols_blk(lambda b, i: (0, n_lat // lc + b)),
            pl.BlockSpec((None, D_HEADS, 3 * tq, tq), kind),
        ],
        out_specs=rows_blk(qmap),
        out_shape=jax.ShapeDtypeStruct((n_lat, D_W), BF16),
        compiler_params=_cparams(("arbitrary", "arbitrary"), 48),
        name="neighbourhood_attention",
    )(q, k, k, k, vt, vt, vt, k, vt, bias)


def kernel(x, c, ctx, c_ctx, w_mod, b_mod, ffn_w_gate, ffn_w_up, ffn_w_down, ab_w_in, ab_w_out, ab_q_norm,
           ab_k_norm, cd_w_in, cd_w_out, cd_dw_w, cd_dw_b, cd_ln_w, cd_ln_b, cd_rpb, final_norm):
    bsz, seq, d = x.shape
    lc = ctx.shape[1]
    depth = w_mod.shape[0]
    assert depth == 2, "layer 0 (A/B mixer) updates the context, layer 1 (C/D mixer) is the last layer"
    assert bsz * lc == TM and seq % TM == 0 and seq % ATT_TQ == 0 and seq % ATT_TK == 0
    n_lat_tiles = bsz * seq // TM
    n_all_tiles = n_lat_tiles + 1
    tiles_per_seq = seq // TM
    mod_row = lambda i: jnp.where(i == n_lat_tiles, bsz, i // tiles_per_seq)
    rope_row = lambda i: jnp.where(i == n_lat_tiles, tiles_per_seq, i % tiles_per_seq)

    mod = _mod_params(c, c_ctx, w_mod, b_mod)
    cos_t, sin_t = _rope_tables(seq)

    ffn_w = (ffn_w_gate, ffn_w_up, ffn_w_down)

    xt = x.reshape(bsz * seq, d)
    for layer in range(depth):
        last = layer == depth - 1
        i = layer // 2
        mod_l = mod[layer]
        xt = _ffn(xt, mod_l, ffn_w, layer, 0, base=0, n_tiles=n_all_tiles, mod_row=mod_row,
                  x_ctx=ctx.reshape(bsz * lc, d) if layer == 0 else None)
        if layer % 2 == 0:
            q, k, vt, y = _inproj_ab(xt, mod_l, ab_w_in, i, ab_q_norm[i], ab_k_norm[i], cos_t, sin_t,
                                     n_tiles=n_all_tiles, mod_row=mod_row, rope_row=rope_row)
            a1, a1_ctx = _gqa(q, k, vt, ab_q_norm[i], ab_k_norm[i], bsz=bsz, seq=seq, lc=lc)
            a2, a2_ctx = _fourier(y, bsz=bsz, seq=seq, lc=lc)
            mixer = (a1, a2, ab_w_out, i, a1_ctx, a2_ctx)
        else:
            u, q, k, vt = _inproj_cd(xt, mod_l, cd_w_in, i, n_tiles=n_all_tiles, mod_row=mod_row)
            a1 = _conv_module(u, cd_dw_w[i], cd_dw_b[i], cd_ln_w[i], cd_ln_b[i], bsz=bsz, seq=seq)
            a2 = _na(q, k, vt, cd_rpb[i], bsz=bsz, seq=seq, lc=lc)
            mixer = (a1, a2, cd_w_out, i, None, None)
        assert last == (mixer[4] is None)
        xt = _ffn(xt, mod_l, ffn_w, layer, 1, base=6, n_tiles=n_lat_tiles if last else n_all_tiles, mod_row=mod_row,
                  final_norm=final_norm if last else None, mixer=mixer)
    return xt.reshape(bsz, seq, d)
```

```python
import functools
import math

import numpy as np
import jax
import jax.numpy as jnp
from jax import lax
from jax.experimental import pallas as pl
from jax.experimental.pallas import tpu as pltpu

F32 = jnp.float32
BF16 = jnp.bfloat16

GRID_W = 64
HEAD_DIM = 64
A_Q_HEADS = 12
A_KV_HEADS = 4
B_GROUPS = 4
B_GROUP_DIM = 64
C_CHANNELS = 512
C_KERNEL = 31
D_HEADS = 8
NA_WIN_ROWS = 8
NA_WIN_COLS = 16
ROPE_THETA = 10000.0
NORM_EPS = 1e-6
N_MOD = 9
FFN_RES_WEIGHT = 0.5

A_Q_W = A_Q_HEADS * HEAD_DIM
A_KV_W = A_KV_HEADS * HEAD_DIM
B_W = B_GROUPS * B_GROUP_DIM
D_W = D_HEADS * HEAD_DIM

LANES = 128
SUBLANES = 8
MXU_DIM = 256

TM = 512
FFN_CHUNK = 256
ATT_TQ = 1024
ATT_TK = 1024
ATT_KBODY = 512
NA_QROWS = 4
NA_LOOKAHEAD = 3
CONV_HALO = 16
ATT_QCHUNK = 256
ATT_LOOKAHEAD = 2
SAFE_SCORE_BOUND_LOG2 = 100.0
ONES_ROWS = 16
V_GROUP = HEAD_DIM + ONES_ROWS
NEG_BIG = -1e30
Q_SCALE_LOG2 = (HEAD_DIM ** -0.5) * math.log2(math.e)

_MiB = 1 << 20


def _cparams(sem, vmem_mib):
    return pltpu.CompilerParams(dimension_semantics=sem, vmem_limit_bytes=vmem_mib * _MiB)


def _dot(a, b):
    return jnp.dot(a, b, preferred_element_type=F32)


def _dot_nt(a, b):
    return lax.dot_general(a, b, (((1,), (1,)), ((), ())), preferred_element_type=F32)


def _sigmoid(x):
    return 1.0 / (1.0 + jnp.exp(-x))


def _cos_sin(n):
    idx = np.arange(n)
    ang = 2.0 * np.pi * (np.outer(idx, idx) % n) / n
    return np.cos(ang), np.sin(ang)


def _const_bf16(a):
    return jnp.asarray(np.asarray(a, np.float32)).astype(BF16)


def _modulate(x, shift, scale):
    ms = jnp.mean(x * x, axis=-1, keepdims=True)
    return (x * lax.rsqrt(ms + NORM_EPS)) * (1.0 + scale) + shift


def _mod_kernel(ct_ref, w_ref, b_ref, o_ref, *, n_rows):
    ct = ct_ref[...]
    a = ct * _sigmoid(ct)
    w = w_ref[...]
    rows = [jnp.sum(w * a[:, r:r + 1], axis=0, keepdims=True) for r in range(n_rows)]
    rows.append(jnp.zeros((SUBLANES - n_rows, w.shape[1]), F32))
    o_ref[...] = jnp.concatenate(rows, axis=0) + b_ref[...]


def _mod_params(c, c_ctx, w_mod, b_mod):
    depth, d, nmd = w_mod.shape
    bsz = c.shape[0]
    n_rows = bsz + 1
    assert n_rows <= SUBLANES
    ct = jnp.concatenate([c, c_ctx[None], jnp.zeros((SUBLANES - n_rows, d), F32)], axis=0).T
    tn = d
    out = pl.pallas_call(
        functools.partial(_mod_kernel, n_rows=n_rows),
        grid=(depth, nmd // tn),
        in_specs=[
            pl.BlockSpec((d, SUBLANES), lambda l, n: (0, 0)),
            pl.BlockSpec((None, d, tn), lambda l, n: (l, 0, n)),
            pl.BlockSpec((None, 1, tn), lambda l, n: (l, 0, n)),
        ],
        out_specs=pl.BlockSpec((None, SUBLANES, tn), lambda l, n: (l, 0, n)),
        out_shape=jax.ShapeDtypeStruct((depth, SUBLANES, nmd), F32),
        compiler_params=_cparams(("arbitrary", "arbitrary"), 32),
        name="adaln_params",
    )(ct, w_mod, b_mod.reshape(depth, 1, nmd))
    return out.reshape(depth, SUBLANES, N_MOD, d)


def _pick_rows(lat_ref, ctx_ref, n_lat_tiles):
    return jnp.where(pl.program_id(0) == n_lat_tiles, ctx_ref[...], lat_ref[...])


def _ffn_kernel(*refs, base, final, n_chunks, ctx_tile, x_has_ctx, mixer):
    refs = list(refs)
    x_ref = refs.pop(0)
    x = x_ref[...] if not x_has_ctx else _pick_rows(x_ref, refs.pop(0), ctx_tile)
    if mixer:
        a1_ref, a2_ref = refs.pop(0), refs.pop(0)
        if ctx_tile is None:
            a1, a2 = a1_ref[...], a2_ref[...]
        else:
            a1 = _pick_rows(a1_ref, refs.pop(0), ctx_tile)
            a2 = _pick_rows(a2_ref, refs.pop(0), ctx_tile)
        w1_ref, w2_ref = refs.pop(0), refs.pop(0)
    mod_ref, wg_ref, wu_ref, wd_ref = refs[:4]
    if final:
        fn_ref, o_ref, acc_ref = refs[4:]
    else:
        o_ref, acc_ref = refs[4:]
    m = mod_ref[...]
    if mixer:
        x = x + m[5:6] * (_dot(a1.astype(BF16), w1_ref[...].astype(BF16)) +
                          _dot(a2.astype(BF16), w2_ref[...].astype(BF16)))
    h = _modulate(x, m[base:base + 1], m[base + 1:base + 2]).astype(BF16)
    for c in range(n_chunks):
        sl = slice(c * FFN_CHUNK, (c + 1) * FFN_CHUNK)
        g = _dot(h, wg_ref[:, sl].astype(BF16))
        u = _dot(h, wu_ref[:, sl].astype(BF16))
        a = ((g * _sigmoid(g)) * u).astype(BF16)
        part = _dot(a, wd_ref[sl, :].astype(BF16))
        if c == 0:
            acc_ref[...] = part
        else:
            acc_ref[...] += part
    y = x + (FFN_RES_WEIGHT * m[base + 2:base + 3]) * acc_ref[...]
    if final:
        ms = jnp.mean(y * y, axis=-1, keepdims=True)
        y = (y * lax.rsqrt(ms + NORM_EPS)) * fn_ref[...]
    o_ref[...] = y


def _ffn(x, mod_l, ffn_w, layer, half, *, base, n_tiles, mod_row, final_norm=None, x_ctx=None, mixer=None):
    d = x.shape[1]
    wg, wu, wd = ffn_w
    dff = wg.shape[-1]
    assert dff % FFN_CHUNK == 0
    final = final_norm is not None
    resident = dict(pipeline_mode=pl.Buffered(1))
    const = lambda i: (0, 0)
    last_lat = n_tiles - 2
    lat_only = lambda i: (jnp.minimum(i, last_lat), 0)
    ctx_tile = None
    in_specs = [pl.BlockSpec((TM, d), lambda i: (i, 0))]
    args = [x]
    if x_ctx is not None:
        ctx_tile = n_tiles - 1
        assert x.shape[0] == ctx_tile * TM and x_ctx.shape == (TM, d)
        in_specs = [pl.BlockSpec((TM, d), lat_only), pl.BlockSpec((TM, d), const)]
        args.append(x_ctx)
    if mixer is not None:
        a1, a2, w_out, w_idx, a1_ctx, a2_ctx = mixer
        d1, d2 = a1.shape[1], a2.shape[1]
        assert d1 % d2 == 0
        if a1_ctx is None:
            in_specs += [pl.BlockSpec((TM, d1), lambda i: (i, 0)), pl.BlockSpec((TM, d2), lambda i: (i, 0))]
            args += [a1, a2]
        else:
            ctx_tile = n_tiles - 1
            assert a1.shape[0] == ctx_tile * TM and a1_ctx.shape[0] == TM and a2_ctx.shape[0] == TM
            in_specs += [pl.BlockSpec((TM, d1), lat_only), pl.BlockSpec((TM, d2), lat_only),
                         pl.BlockSpec((TM, d1), const), pl.BlockSpec((TM, d2), const)]
            args += [a1, a2, a1_ctx, a2_ctx]
        in_specs += [pl.BlockSpec((None, d1, d), lambda i: (w_idx, 0, 0), **resident),
                     pl.BlockSpec((None, d2, d), lambda i: (w_idx, d1 // d2, 0), **resident)]
        args += [w_out, w_out]
    sel = lambda i: (layer, half, 0, 0)
    in_specs += [
        pl.BlockSpec((None, N_MOD, d), lambda i: (mod_row(i), 0, 0)),
        pl.BlockSpec((None, None, d, dff), sel, **resident),
        pl.BlockSpec((None, None, d, dff), sel, **resident),
        pl.BlockSpec((None, None, dff, d), sel, **resident),
    ]
    args += [mod_l, wg, wu, wd]
    if final:
        in_specs.append(pl.BlockSpec((1, d), const))
        args.append(final_norm.reshape(1, d))
    return pl.pallas_call(
        functools.partial(_ffn_kernel, base=base, final=final, n_chunks=dff // FFN_CHUNK, ctx_tile=ctx_tile,
                          x_has_ctx=x_ctx is not None, mixer=mixer is not None),
        grid=(n_tiles,),
        in_specs=in_specs,
        out_specs=pl.BlockSpec((TM, d), lambda i: (i, 0)),
        out_shape=jax.ShapeDtypeStruct((n_tiles * TM, d), F32),
        scratch_shapes=[pltpu.VMEM((TM, d), F32)],
        compiler_params=_cparams(("arbitrary",), 60),
        name=("mix_ffn_final" if final else "mix_ffn") if mixer is not None else "ffn",
    )(*args)


def _swap_halves(z, low_half):
    n = z.shape[1]
    return jnp.where(low_half, pltpu.roll(z, n - HEAD_DIM // 2, 1), pltpu.roll(z, HEAD_DIM // 2, 1))


def _head_sumsq(z, ones_bd):
    zz = z * z
    hi = zz.astype(BF16)
    lo = (zz - hi.astype(F32)).astype(BF16)
    return _dot(hi, ones_bd) + _dot(lo, ones_bd)


def _inproj_ab_kernel(x_ref, mod_ref, w_ref, qn_ref, kn_ref, cos_ref, sin_ref, ones_ref, dft_ref,
                      q_ref, k_ref, vt_ref, y_ref):
    x = x_ref[...]
    m = mod_ref[...]
    h = _modulate(x, m[3:4], m[4:5]).astype(BF16)
    y = _dot(h, w_ref[...].astype(BF16))
    ones_bd = ones_ref[...]
    cos = jnp.concatenate([cos_ref[...]] * 2, axis=1)
    sin = jnp.concatenate([sin_ref[...]] * 2, axis=1)
    lane = lax.broadcasted_iota(jnp.int32, (1, MXU_DIM), 1)
    low_half = (lane % HEAD_DIM) < (HEAD_DIM // 2)

    def norm_rope(z, nw):
        ss = _head_sumsq(z, ones_bd)
        zn = (z * lax.rsqrt(ss * (1.0 / HEAD_DIM) + NORM_EPS)) * nw
        return zn * cos + _swap_halves(zn, low_half) * sin

    qn = qn_ref[...]
    kn = kn_ref[...]
    for blk in range(A_Q_W // MXU_DIM):
        sl = slice(blk * MXU_DIM, (blk + 1) * MXU_DIM)
        q_ref[:, sl] = (norm_rope(y[:, sl], qn) * Q_SCALE_LOG2).astype(BF16)
    k_ref[...] = norm_rope(y[:, A_Q_W:A_Q_W + A_KV_W], kn).astype(BF16)
    vt_ref[...] = _values_with_ones_t(y[:, A_Q_W + A_KV_W:A_Q_W + 2 * A_KV_W], A_KV_HEADS).astype(BF16)
    f = y[:, A_Q_W + 2 * A_KV_W:].astype(BF16)
    y_ref[...] = _dot(f, dft_ref[...])


def _inproj_ab(x, mod_l, w_in, w_idx, q_norm, k_norm, cos_t, sin_t, *, n_tiles, mod_row, rope_row):
    ntok, d = x.shape
    n_in = w_in.shape[-1]
    ones_bd = _const_bf16(np.kron(np.eye(MXU_DIM // HEAD_DIM), np.ones((HEAD_DIM, HEAD_DIM))))
    gc, gs = _cos_sin(B_GROUP_DIM)
    eye = np.eye(B_GROUPS)
    dft = _const_bf16(np.concatenate([np.kron(eye, gc), np.kron(eye, gs)], axis=1))
    qn = jnp.tile(q_norm, MXU_DIM // HEAD_DIM).reshape(1, MXU_DIM)
    kn = jnp.tile(k_norm, MXU_DIM // HEAD_DIM).reshape(1, MXU_DIM)
    const = lambda i: (0, 0)
    row = lambda i: (i, 0)
    return pl.pallas_call(
        _inproj_ab_kernel,
        grid=(n_tiles,),
        in_specs=[
            pl.BlockSpec((TM, d), row),
            pl.BlockSpec((None, N_MOD, d), lambda i: (mod_row(i), 0, 0)),
            pl.BlockSpec((None, d, n_in), lambda i: (w_idx, 0, 0)),
            pl.BlockSpec((1, MXU_DIM), const),
            pl.BlockSpec((1, MXU_DIM), const),
            pl.BlockSpec((TM, LANES), lambda i: (rope_row(i), 0)),
            pl.BlockSpec((TM, LANES), lambda i: (rope_row(i), 0)),
            pl.BlockSpec((MXU_DIM, MXU_DIM), const),
            pl.BlockSpec((B_W, 2 * B_W), const),
        ],
        out_specs=[
            pl.BlockSpec((TM, A_Q_W), row),
            pl.BlockSpec((TM, A_KV_W), row),
            pl.BlockSpec((A_KV_HEADS * V_GROUP, TM), lambda i: (0, i)),
            pl.BlockSpec((TM, 2 * B_W), row),
        ],
        out_shape=[
            jax.ShapeDtypeStruct((ntok, A_Q_W), BF16),
            jax.ShapeDtypeStruct((ntok, A_KV_W), BF16),
            jax.ShapeDtypeStruct((A_KV_HEADS * V_GROUP, ntok), BF16),
            jax.ShapeDtypeStruct((ntok, 2 * B_W), F32),
        ],
        compiler_params=_cparams(("arbitrary",), 40),
        name="in_proj_ab",
    )(x, mod_l, w_in, qn, kn, cos_t, sin_t, ones_bd, dft)


def _rope_tables(seq):
    t = np.arange(seq)
    row = (t // GRID_W).astype(np.float64)
    col = (t % GRID_W).astype(np.float64)
    n_ax = HEAD_DIM // 4
    inv = ROPE_THETA ** (-np.arange(n_ax, dtype=np.float64) / n_ax)
    ang = np.concatenate([row[:, None] * inv, col[:, None] * inv], axis=-1)
    cos, sin = np.cos(ang), np.sin(ang)
    cos_h = np.concatenate([cos, cos], axis=-1)
    sin_h = np.concatenate([-sin, sin], axis=-1)
    cos_t = np.concatenate([np.tile(cos_h, (1, 2)), np.ones((TM, LANES))], axis=0)
    sin_t = np.concatenate([np.tile(sin_h, (1, 2)), np.zeros((TM, LANES))], axis=0)
    return jnp.asarray(cos_t.astype(np.float32)), jnp.asarray(sin_t.astype(np.float32))


def _gqa_kernel(bound_ref, q_ref, kc_ref, vtc_ref, *rest, has_lat):
    if has_lat:
        kl_ref, vtl_ref, o_ref, qm_ref, m_ref, acc_ref = rest
    else:
        o_ref, qm_ref, m_ref, acc_ref = rest
    j = pl.program_id(2)
    group = A_Q_HEADS // A_KV_HEADS
    lane = lax.broadcasted_iota(jnp.int32, (1, LANES), 1)
    hi_half = lane >= HEAD_DIM
    n_chunks = q_ref.shape[0] // ATT_QCHUNK
    no_shift = bound_ref[0] <= SAFE_SCORE_BOUND_LOG2

    def update(k_ref, vt_ref, shifted):
        kbody = min(ATT_KBODY, k_ref.shape[0])
        bodies = [(kb, h, c) for kb in range(k_ref.shape[0] // kbody) for h in range(A_Q_HEADS) for c in range(n_chunks)]

        def scores(kb, h, c):
            kvp = (h // group) // 2
            kp = k_ref[kb * kbody:(kb + 1) * kbody, kvp * LANES:(kvp + 1) * LANES]
            return _dot_nt(kp, qm_ref[h, c * ATT_QCHUNK:(c + 1) * ATT_QCHUNK, :])

        def accumulate(h, c, alpha, pv):
            cs = slice(c * ATT_QCHUNK, (c + 1) * ATT_QCHUNK)
            prev = acc_ref[h, :, cs] if alpha is None else alpha * acc_ref[h, :, cs]
            acc_ref[h, :, cs] = prev + pv

        queue = [scores(*bodies[t]) for t in range(min(ATT_LOOKAHEAD, len(bodies)))]
        pending = None
        for idx, (kb, h, c) in enumerate(bodies):
            s = queue.pop(0)
            if idx + ATT_LOOKAHEAD < len(bodies):
                queue.append(scores(*bodies[idx + ATT_LOOKAHEAD]))
            cs = slice(c * ATT_QCHUNK, (c + 1) * ATT_QCHUNK)
            kvh = h // group
            vt = vt_ref[kvh * V_GROUP:(kvh + 1) * V_GROUP, kb * kbody:(kb + 1) * kbody]
            if shifted:
                m_prev = m_ref[h, :, cs]
                m_new = jnp.maximum(m_prev, jnp.max(s, axis=0, keepdims=True))
                alpha = jnp.exp2(m_prev - m_new)
                p = jnp.exp2(s - m_new).astype(BF16)
                m_ref[h, :, cs] = m_new
            else:
                alpha = None
                p = jnp.exp2(s).astype(BF16)
            pv = _dot(vt, p)
            if pending is not None:
                accumulate(*pending)
            pending = (h, c, alpha, pv)
        accumulate(*pending)

    def update_either(k_ref, vt_ref):
        @pl.when(no_shift)
        def _plain():
            update(k_ref, vt_ref, shifted=False)

        @pl.when(jnp.logical_not(no_shift))
        def _running_max():
            update(k_ref, vt_ref, shifted=True)

    @pl.when(j == 0)
    def _first():
        m_ref[...] = jnp.full(m_ref.shape, NEG_BIG, F32)
        acc_ref[...] = jnp.zeros(acc_ref.shape, F32)
        for h in range(A_Q_HEADS):
            qp = q_ref[:, (h // 2) * LANES:(h // 2 + 1) * LANES].astype(F32)
            kv_half = (h // group) % 2
            if kv_half != h % 2:
                qp = pltpu.roll(qp, HEAD_DIM, 1)
            keep = hi_half if kv_half else jnp.logical_not(hi_half)
            qm_ref[h] = jnp.where(keep, qp, jnp.zeros_like(qp)).astype(BF16)
        update_either(kc_ref, vtc_ref)

    if has_lat:
        @pl.when(j > 0)
        def _rest():
            update_either(kl_ref, vtl_ref)

    @pl.when(j == pl.num_programs(2) - 1)
    def _final():
        for pb in range(A_Q_HEADS // 2):
            for c in range(n_chunks):
                cs = slice(c * ATT_QCHUNK, (c + 1) * ATT_QCHUNK)
                parts = []
                for h in (2 * pb, 2 * pb + 1):
                    a = acc_ref[h, :, cs]
                    parts.append(a[:HEAD_DIM] / a[HEAD_DIM:HEAD_DIM + 1])
                o_ref[cs, pb * LANES:(pb + 1) * LANES] = jnp.concatenate(parts, axis=0).T.astype(BF16)


def _gqa_scratch(tq):
    return [
        pltpu.VMEM((A_Q_HEADS, tq, LANES), BF16),
        pltpu.VMEM((A_Q_HEADS, 1, tq), F32),
        pltpu.VMEM((A_Q_HEADS, V_GROUP, tq), F32),
    ]


def _gqa(q, k, vt, q_norm, k_norm, *, bsz, seq, lc):
    n_lat = bsz * seq
    tq, tk = ATT_TQ, ATT_TK
    ctx_blk = lambda b: n_lat // lc + b
    bound = (1.05 * HEAD_DIM * Q_SCALE_LOG2 * jnp.max(jnp.abs(q_norm)) * jnp.max(jnp.abs(k_norm))).reshape(1)
    smem = pl.BlockSpec(memory_space=pltpu.SMEM)
    o = pl.pallas_call(
        functools.partial(_gqa_kernel, has_lat=True),
        grid=(bsz, seq // tq, 1 + seq // tk),
        in_specs=[
            smem,
            pl.BlockSpec((tq, A_Q_W), lambda b, i, j: (b * (seq // tq) + i, 0)),
            pl.BlockSpec((lc, A_KV_W), lambda b, i, j: (ctx_blk(b), 0)),
            pl.BlockSpec((A_KV_HEADS * V_GROUP, lc), lambda b, i, j: (0, ctx_blk(b))),
            pl.BlockSpec((tk, A_KV_W), lambda b, i, j: (b * (seq // tk) + jnp.maximum(j - 1, 0), 0)),
            pl.BlockSpec((A_KV_HEADS * V_GROUP, tk), lambda b, i, j: (0, b * (seq // tk) + jnp.maximum(j - 1, 0))),
        ],
        out_specs=pl.BlockSpec((tq, A_Q_W), lambda b, i, j: (b * (seq // tq) + i, 0)),
        out_shape=jax.ShapeDtypeStruct((n_lat, A_Q_W), BF16),
        scratch_shapes=_gqa_scratch(tq),
        compiler_params=_cparams(("arbitrary", "arbitrary", "arbitrary"), 48),
        name="gqa_latent",
    )(bound, q, k, vt, k, vt)
    o_ctx = pl.pallas_call(
        functools.partial(_gqa_kernel, has_lat=False),
        grid=(bsz, 1, 1),
        in_specs=[
            smem,
            pl.BlockSpec((lc, A_Q_W), lambda b, i, j: (ctx_blk(b), 0)),
            pl.BlockSpec((lc, A_KV_W), lambda b, i, j: (ctx_blk(b), 0)),
            pl.BlockSpec((A_KV_HEADS * V_GROUP, lc), lambda b, i, j: (0, ctx_blk(b))),
        ],
        out_specs=pl.BlockSpec((lc, A_Q_W), lambda b, i, j: (b, 0)),
        out_shape=jax.ShapeDtypeStruct((bsz * lc, A_Q_W), BF16),
        scratch_shapes=_gqa_scratch(lc),
        compiler_params=_cparams(("arbitrary", "arbitrary", "arbitrary"), 32),
        name="gqa_context",
    )(bound, q, k, vt)
    return o, o_ctx


def _dft_a_kernel(y_ref, ca_ref, tc_ref, ts_ref, d_ref, *, nbb):
    ca = ca_ref[...]
    ra = ca.shape[1]
    for j in range(nbb):
        yj = y_ref[:, j, :].astype(BF16)
        p = _dot(ca, yj)
        br = p[:ra, :B_W] - p[ra:, B_W:]
        bi = -p[:ra, B_W:] - p[ra:, :B_W]
        tc = jnp.concatenate([tc_ref[j]] * (B_W // LANES), axis=1)
        ts = jnp.concatenate([ts_ref[j]] * (B_W // LANES), axis=1)
        d_ref[:, 0, j, :] = br * tc + bi * ts
        d_ref[:, 1, j, :] = bi * tc - br * ts


def _dft_b_kernel(d_ref, cb_ref, o_ref, *, kb, scale):
    cb = cb_ref[...]
    for j in range(kb):
        o_ref[:, j, :] = _dot(cb, d_ref[j].astype(BF16)) * scale


def _dft_ctx_kernel(y_ref, cs_ref, o_ref, *, scale):
    y = y_ref[...]
    cs = cs_ref[...]
    lc = cs.shape[0]
    yc = y[:, :B_W].astype(BF16)
    ys = y[:, B_W:].astype(BF16)
    o_ref[...] = (_dot(cs[:, :lc], yc) - _dot(cs[:, lc:], ys)) * scale


def _fourier(y, *, bsz, seq, lc):
    ntok = y.shape[0]
    ra = seq // GRID_W
    nb = GRID_W
    nbb = SUBLANES
    kb = SUBLANES
    assert ntok % nb == 0 and ra % SUBLANES == 0 and (ntok // nb) % 1 == 0
    ca_c, ca_s = _cos_sin(ra)
    ca = _const_bf16(np.concatenate([ca_c, ca_s], axis=0))
    th = 2.0 * np.pi * np.outer(np.arange(nb), np.arange(ra)) / seq
    tc = jnp.asarray(np.broadcast_to(np.cos(th)[:, :, None], (nb, ra, LANES)).astype(np.float32))
    ts = jnp.asarray(np.broadcast_to(np.sin(th)[:, :, None], (nb, ra, LANES)).astype(np.float32))
    cb_c, cb_s = _cos_sin(nb)
    cb = _const_bf16(np.concatenate([cb_c, cb_s], axis=1))

    y3 = y.reshape(ntok // nb, nb, 2 * B_W)
    d = pl.pallas_call(
        functools.partial(_dft_a_kernel, nbb=nbb),
        grid=(bsz, nb // nbb),
        in_specs=[
            pl.BlockSpec((ra, nbb, 2 * B_W), lambda b, i: (b, i, 0)),
            pl.BlockSpec((2 * ra, ra), lambda b, i: (0, 0)),
            pl.BlockSpec((nbb, ra, LANES), lambda b, i: (i, 0, 0)),
            pl.BlockSpec((nbb, ra, LANES), lambda b, i: (i, 0, 0)),
        ],
        out_specs=pl.BlockSpec((None, ra, 2, nbb, B_W), lambda b, i: (b, 0, 0, i, 0)),
        out_shape=jax.ShapeDtypeStruct((bsz, ra, 2, nb, B_W), F32),
        compiler_params=_cparams(("arbitrary", "arbitrary"), 32),
        name="fourier_rows",
    )(y3, ca, tc, ts)
    d = d.reshape(bsz, ra, 2 * nb, B_W)

    scale = 1.0 / math.sqrt(seq * B_GROUP_DIM)
    n_out = bsz * nb
    z3 = pl.pallas_call(
        functools.partial(_dft_b_kernel, kb=kb, scale=scale),
        grid=(bsz, ra // kb),
        in_specs=[
            pl.BlockSpec((None, kb, 2 * nb, B_W), lambda b, i: (b, i, 0, 0)),
            pl.BlockSpec((nb, 2 * nb), lambda b, i: (0, 0)),
        ],
        out_specs=pl.BlockSpec((nb, kb, B_W), lambda b, i: (b, i, 0)),
        out_shape=jax.ShapeDtypeStruct((n_out, ra, B_W), F32),
        compiler_params=_cparams(("arbitrary", "arbitrary"), 32),
        name="fourier_cols",
    )(d, cb)
    z = z3.reshape(bsz * seq, B_W)

    cc, cs_ = _cos_sin(lc)
    csm = _const_bf16(np.concatenate([cc, cs_], axis=1))
    ctx_blk = lambda b: (bsz * seq) // lc + b
    z_ctx = pl.pallas_call(
        functools.partial(_dft_ctx_kernel, scale=1.0 / math.sqrt(lc * B_GROUP_DIM)),
        grid=(bsz,),
        in_specs=[
            pl.BlockSpec((lc, 2 * B_W), lambda b: (ctx_blk(b), 0)),
            pl.BlockSpec((lc, 2 * lc), lambda b: (0, 0)),
        ],
        out_specs=pl.BlockSpec((lc, B_W), lambda b: (b, 0)),
        out_shape=jax.ShapeDtypeStruct((bsz * lc, B_W), F32),
        compiler_params=_cparams(("arbitrary",), 32),
        name="fourier_context",
    )(y, csm)
    return z, z_ctx


def _values_with_ones_t(v, n_heads):
    vt = v.T
    ones = jnp.ones((ONES_ROWS, vt.shape[1]), F32)
    groups = []
    for h in range(n_heads):
        groups += [vt[h * HEAD_DIM:(h + 1) * HEAD_DIM], ones]
    return jnp.concatenate(groups, axis=0)


def _inproj_cd_kernel(x_ref, mod_ref, w_ref, u_ref, q_ref, k_ref, vt_ref):
    x = x_ref[...]
    m = mod_ref[...]
    h = _modulate(x, m[3:4], m[4:5]).astype(BF16)
    y = _dot(h, w_ref[...].astype(BF16))
    c = C_CHANNELS
    u_ref[...] = y[:, :c] * _sigmoid(y[:, c:2 * c])
    q_ref[...] = (y[:, 2 * c:2 * c + D_W] * Q_SCALE_LOG2).astype(BF16)
    k_ref[...] = y[:, 2 * c + D_W:2 * c + 2 * D_W].astype(BF16)
    vt_ref[...] = _values_with_ones_t(y[:, 2 * c + 2 * D_W:], D_HEADS).astype(BF16)


def _inproj_cd(x, mod_l, w_in, w_idx, *, n_tiles, mod_row):
    ntok, d = x.shape
    n_in = w_in.shape[-1]
    row = lambda i: (i, 0)
    return pl.pallas_call(
        _inproj_cd_kernel,
        grid=(n_tiles,),
        in_specs=[
            pl.BlockSpec((TM, d), row),
            pl.BlockSpec((None, N_MOD, d), lambda i: (mod_row(i), 0, 0)),
            pl.BlockSpec((None, d, n_in), lambda i: (w_idx, 0, 0)),
        ],
        out_specs=[
            pl.BlockSpec((TM, C_CHANNELS), row),
            pl.BlockSpec((TM, D_W), row),
            pl.BlockSpec((TM, D_W), row),
            pl.BlockSpec((D_HEADS * V_GROUP, TM), lambda i: (0, i)),
        ],
        out_shape=[
            jax.ShapeDtypeStruct((ntok, C_CHANNELS), F32),
            jax.ShapeDtypeStruct((ntok, D_W), BF16),
            jax.ShapeDtypeStruct((ntok, D_W), BF16),
            jax.ShapeDtypeStruct((D_HEADS * V_GROUP, ntok), BF16),
        ],
        compiler_params=_cparams(("arbitrary",), 40),
        name="in_proj_cd",
    )(x, mod_l, w_in)


def _conv_kernel(prev_ref, cur_ref, next_ref, w_ref, b_ref, lnw_ref, lnb_ref, o_ref, buf_ref, shift_ref, *,
                 tiles_per_seq):
    i = pl.program_id(1)
    halo = CONV_HALO
    tm = cur_ref.shape[0]
    zeros = jnp.zeros((halo, cur_ref.shape[1]), F32)
    buf_ref[0:halo, :] = jnp.where(i > 0, prev_ref[...], zeros)
    buf_ref[halo:halo + tm, :] = cur_ref[...]
    buf_ref[halo + tm:, :] = jnp.where(i < tiles_per_seq - 1, next_ref[...], zeros)
    w = w_ref[...]
    off = halo - C_KERNEL // 2
    y = None
    for r in range(SUBLANES):
        taps = [t for t in range(C_KERNEL) if (t + off) % SUBLANES == r]
        part = None
        for t in taps:
            start = t + off - r
            term = buf_ref[start:start + tm + SUBLANES, :] * w[t:t + 1]
            part = term if part is None else part + term
        if r == 0:
            shifted = part[:tm]
        else:
            shift_ref[...] = part
            shifted = shift_ref[r:r + tm, :]
        y = shifted if y is None else y + shifted
    y = y + b_ref[...]
    mu = jnp.mean(y, axis=-1, keepdims=True)
    yc = y - mu
    var = jnp.mean(yc * yc, axis=-1, keepdims=True)
    z = (yc * lax.rsqrt(var + NORM_EPS)) * lnw_ref[...] + lnb_ref[...]
    o_ref[...] = (z * _sigmoid(z)).astype(BF16)


def _conv_module(u, dw_w, dw_b, ln_w, ln_b, *, bsz, seq):
    c = u.shape[1]
    tiles = seq // TM
    hpt = TM // CONV_HALO
    n_halo_blocks = u.shape[0] // CONV_HALO
    cur = lambda b, i: (b * tiles + i, 0)
    prev = lambda b, i: (jnp.maximum((b * tiles + i) * hpt - 1, 0), 0)
    nxt = lambda b, i: (jnp.minimum((b * tiles + i + 1) * hpt, n_halo_blocks - 1), 0)
    vec = lambda b, i: (0, 0)
    return pl.pallas_call(
        functools.partial(_conv_kernel, tiles_per_seq=tiles),
        grid=(bsz, tiles),
        in_specs=[
            pl.BlockSpec((CONV_HALO, c), prev),
            pl.BlockSpec((TM, c), cur),
            pl.BlockSpec((CONV_HALO, c), nxt),
            pl.BlockSpec((C_KERNEL, c), vec),
            pl.BlockSpec((1, c), vec),
            pl.BlockSpec((1, c), vec),
            pl.BlockSpec((1, c), vec),
        ],
        out_specs=pl.BlockSpec((TM, c), cur),
        out_shape=jax.ShapeDtypeStruct((bsz * seq, c), BF16),
        scratch_shapes=[pltpu.VMEM((TM + 2 * CONV_HALO, c), F32), pltpu.VMEM((TM + SUBLANES, c), F32)],
        compiler_params=_cparams(("arbitrary", "arbitrary"), 32),
        name="conv_module",
    )(u, u, u, dw_w, dw_b.reshape(1, c), ln_w.reshape(1, c), ln_b.reshape(1, c))


def _na_kernel(q_ref, k0_ref, k1_ref, k2_ref, vt0_ref, vt1_ref, vt2_ref, kc_ref, vtc_ref, bias_ref, o_ref):
    lane = lax.broadcasted_iota(jnp.int32, (1, LANES), 1)
    hi_half = lane >= HEAD_DIM

    def scores(h):
        sl = slice((h // 2) * LANES, (h // 2 + 1) * LANES)
        qp = q_ref[:, sl]
        keep = hi_half if h % 2 else jnp.logical_not(hi_half)
        qm = jnp.where(keep, qp, jnp.zeros_like(qp))
        tq = qp.shape[0]
        near = [_dot_nt(k_ref[:, sl], qm) + bias_ref[h, n * tq:(n + 1) * tq, :]
                for n, k_ref in enumerate((k0_ref, k1_ref, k2_ref))]
        return near + [_dot_nt(kc_ref[:, sl], qm)]

    lookahead = min(NA_LOOKAHEAD, D_HEADS)
    queue = [scores(h) for h in range(lookahead)]
    outs = []
    for h in range(D_HEADS):
        tiles = queue.pop(0)
        if h + lookahead < D_HEADS:
            queue.append(scores(h + lookahead))
        mx = functools.reduce(jnp.maximum, [jnp.max(t, axis=0, keepdims=True) for t in tiles])
        p = jnp.concatenate([jnp.exp2(t - mx).astype(BF16) for t in tiles], axis=0)
        rs = slice(h * V_GROUP, (h + 1) * V_GROUP)
        vt = jnp.concatenate([vt0_ref[rs, :], vt1_ref[rs, :], vt2_ref[rs, :], vtc_ref[rs, :]], axis=1)
        outs.append(_dot(vt, p))
    for pb in range(D_HEADS // 2):
        parts = [outs[h][:HEAD_DIM] / outs[h][HEAD_DIM:HEAD_DIM + 1] for h in (2 * pb, 2 * pb + 1)]
        o_ref[:, pb * LANES:(pb + 1) * LANES] = jnp.concatenate(parts, axis=0).T.astype(BF16)


def _na_bias_kernel(rpb_ref, o_ref, *, rows):
    h = pl.program_id(0)
    n_ro, n_co = 2 * NA_WIN_ROWS - 1, 2 * NA_WIN_COLS - 1
    kc = lax.broadcasted_iota(jnp.int32, (GRID_W, LANES), 0)
    lane = lax.broadcasted_iota(jnp.int32, (GRID_W, LANES), 1)
    qc = lane % GRID_W
    d = kc - qc + (NA_WIN_COLS - 1)
    cs = jnp.clip(qc - NA_WIN_COLS // 2, 0, GRID_W - NA_WIN_COLS)
    col_ok = (kc >= cs) & (kc < cs + NA_WIN_COLS)
    neg = jnp.full((GRID_W, LANES), NEG_BIG, F32)
    log2e = math.log2(math.e)
    band = []
    for ro in range(n_ro):
        acc = neg
        for co in range(n_co):
            acc = jnp.where(d == co, rpb_ref[(h * n_ro + ro) * n_co + co] * log2e, acc)
        band.append(jnp.where(col_ok, acc, neg))
    qr = NA_QROWS
    for kind, r0 in enumerate((0, qr, rows - qr)):
        for b in range(3 * qr):
            rk = r0 - qr + b
            for pair in range(qr // 2):
                halves = []
                for a in (2 * pair, 2 * pair + 1):
                    r = r0 + a
                    start = min(max(r - NA_WIN_ROWS // 2, 0), rows - NA_WIN_ROWS)
                    ok = start <= rk < start + NA_WIN_ROWS and 0 <= rk < rows
                    halves.append(band[rk - r + NA_WIN_ROWS - 1] if ok else neg)
                o_ref[kind, b * GRID_W:(b + 1) * GRID_W, pair * LANES:(pair + 1) * LANES] = jnp.where(
                    lane < GRID_W, halves[0], halves[1])


def _na_bias(rpb, rows):
    heads = rpb.shape[0]
    tq = NA_QROWS * GRID_W
    return pl.pallas_call(
        functools.partial(_na_bias_kernel, rows=rows),
        grid=(heads,),
        in_specs=[pl.BlockSpec(memory_space=pltpu.SMEM)],
        out_specs=pl.BlockSpec((3, None, 3 * tq, tq), lambda h: (0, h, 0, 0)),
        out_shape=jax.ShapeDtypeStruct((3, heads, 3 * tq, tq), F32),
        compiler_params=_cparams(("arbitrary",), 32),
        name="na_bias",
    )(rpb.reshape(-1))


def _na(q, k, vt, rpb, *, bsz, seq, lc):
    rows = seq // GRID_W
    tq = NA_QROWS * GRID_W
    nblk = seq // tq
    assert rows >= NA_WIN_ROWS and rows % NA_QROWS == 0 and NA_WIN_ROWS == 2 * NA_QROWS and nblk >= 3
    assert lc == tq
    bias = _na_bias(rpb, rows)
    n_lat = bsz * seq
    qmap = lambda b, i: (b * nblk + i, 0)

    def near(off, transposed):
        def index(b, i):
            blk = b * nblk + jnp.clip(i + off, 0, nblk - 1)
            return (0, blk) if transposed else (blk, 0)
        return index

    kind = lambda b, i: (jnp.where(i == 0, 0, jnp.where(i == nblk - 1, 2, 1)), 0, 0, 0)
    rows_blk = lambda m: pl.BlockSpec((tq, D_W), m)
    cols_blk = lambda m: pl.BlockSpec((D_HEADS * V_GROUP, tq), m)
    return pl.pallas_call(
        _na_kernel,
        grid=(bsz, nblk),
        in_specs=[
            rows_blk(qmap),
            rows_blk(near(-1, False)), rows_blk(near(0, False)), rows_blk(near(1, False)),
            cols_blk(near(-1, True)), cols_blk(near(0, True)), cols_blk(near(1, True)),
            rows_blk(lambda b, i: (n_lat // lc + b, 0)),
            cols_blk(lambda b, i: (0, n_lat // lc + b)),
            pl.BlockSpec((None, D_HEADS, 3 * tq, tq), kind),
        ],
        out_specs=rows_blk(qmap),
        out_shape=jax.ShapeDtypeStruct((n_lat, D_W), BF16),
        compiler_params=_cparams(("arbitrary", "arbitrary"), 48),
        name="neighbourhood_attention",
    )(q, k, k, k, vt, vt, vt, k, vt, bias)


def kernel(x, c, ctx, c_ctx, w_mod, b_mod, ffn_w_gate, ffn_w_up, ffn_w_down, ab_w_in, ab_w_out, ab_q_norm,
           ab_k_norm, cd_w_in, cd_w_out, cd_dw_w, cd_dw_b, cd_ln_w, cd_ln_b, cd_rpb, final_norm):
    bsz, seq, d = x.shape
    lc = ctx.shape[1]
    depth = w_mod.shape[0]
    assert depth == 2, "layer 0 (A/B mixer) updates the context, layer 1 (C/D mixer) is the last layer"
    assert bsz * lc == TM and seq % TM == 0 and seq % ATT_TQ == 0 and seq % ATT_TK == 0
    n_lat_tiles = bsz * seq // TM
    n_all_tiles = n_lat_tiles + 1
    tiles_per_seq = seq // TM
    mod_row = lambda i: jnp.where(i == n_lat_tiles, bsz, i // tiles_per_seq)
    rope_row = lambda i: jnp.where(i == n_lat_tiles, tiles_per_seq, i % tiles_per_seq)

    mod = _mod_params(c, c_ctx, w_mod, b_mod)
    cos_t, sin_t = _rope_tables(seq)

    ffn_w = (ffn_w_gate, ffn_w_up, ffn_w_down)

    xt = x.reshape(bsz * seq, d)
    for layer in range(depth):
        last = layer == depth - 1
        i = layer // 2
        mod_l = mod[layer]
        xt = _ffn(xt, mod_l, ffn_w, layer, 0, base=0, n_tiles=n_all_tiles, mod_row=mod_row,
                  x_ctx=ctx.reshape(bsz * lc, d) if layer == 0 else None)
        if layer % 2 == 0:
            q, k, vt, y = _inproj_ab(xt, mod_l, ab_w_in, i, ab_q_norm[i], ab_k_norm[i], cos_t, sin_t,
                                     n_tiles=n_all_tiles, mod_row=mod_row, rope_row=rope_row)
            a1, a1_ctx = _gqa(q, k, vt, ab_q_norm[i], ab_k_norm[i], bsz=bsz, seq=seq, lc=lc)
            a2, a2_ctx = _fourier(y, bsz=bsz, seq=seq, lc=lc)
            mixer = (a1, a2, ab_w_out, i, a1_ctx, a2_ctx)
        else:
            u, q, k, vt = _inproj_cd(xt, mod_l, cd_w_in, i, n_tiles=n_all_tiles, mod_row=mod_row)
            a1 = _conv_module(u, cd_dw_w[i], cd_dw_b[i], cd_ln_w[i], cd_ln_b[i], bsz=bsz, seq=seq)
            a2 = _na(q, k, vt, cd_rpb[i], bsz=bsz, seq=seq, lc=lc)
            mixer = (a1, a2, cd_w_out, i, None, None)
        assert last == (mixer[4] is None)
        xt = _ffn(xt, mod_l, ffn_w, layer, 1, base=6, n_tiles=n_lat_tiles if last else n_all_tiles, mod_row=mod_row,
                  final_norm=final_norm if last else None, mixer=mixer)
    return xt.reshape(bsz, seq, d)
```

```python
import functools
import math

import numpy as np
import jax
import jax.numpy as jnp
from jax import lax
from jax.experimental import pallas as pl
from jax.experimental.pallas import tpu as pltpu

F32 = jnp.float32
BF16 = jnp.bfloat16

GRID_W = 64
HEAD_DIM = 64
A_Q_HEADS = 12
A_KV_HEADS = 4
B_GROUPS = 4
B_GROUP_DIM = 64
C_CHANNELS = 512
C_KERNEL = 31
D_HEADS = 8
NA_WIN_ROWS = 8
NA_WIN_COLS = 16
ROPE_THETA = 10000.0
NORM_EPS = 1e-6
N_MOD = 9
FFN_RES_WEIGHT = 0.5

A_Q_W = A_Q_HEADS * HEAD_DIM
A_KV_W = A_KV_HEADS * HEAD_DIM
B_W = B_GROUPS * B_GROUP_DIM
D_W = D_HEADS * HEAD_DIM

LANES = 128
SUBLANES = 8
MXU_DIM = 256

TM = 512
FFN_CHUNK = 256
ATT_TQ = 1024
ATT_TK = 1024
ATT_KBODY = 256
NA_QROWS = 4
NA_LOOKAHEAD = 3
CONV_HALO = 16
ATT_QCHUNK = 256
ATT_LOOKAHEAD = 4
SAFE_SCORE_BOUND_LOG2 = 100.0
ONES_ROWS = 16
V_GROUP = HEAD_DIM + ONES_ROWS
NEG_BIG = -1e30
Q_SCALE_LOG2 = (HEAD_DIM ** -0.5) * math.log2(math.e)

_MiB = 1 << 20


def _cparams(sem, vmem_mib):
    return pltpu.CompilerParams(dimension_semantics=sem, vmem_limit_bytes=vmem_mib * _MiB)


def _dot(a, b):
    return jnp.dot(a, b, preferred_element_type=F32)


def _dot_nt(a, b):
    return lax.dot_general(a, b, (((1,), (1,)), ((), ())), preferred_element_type=F32)


def _sigmoid(x):
    return 1.0 / (1.0 + jnp.exp(-x))


def _cos_sin(n):
    idx = np.arange(n)
    ang = 2.0 * np.pi * (np.outer(idx, idx) % n) / n
    return np.cos(ang), np.sin(ang)


def _const_bf16(a):
    return jnp.asarray(np.asarray(a, np.float32)).astype(BF16)


def _modulate(x, shift, scale):
    ms = jnp.mean(x * x, axis=-1, keepdims=True)
    return (x * lax.rsqrt(ms + NORM_EPS)) * (1.0 + scale) + shift


def _mod_kernel(ct_ref, w_ref, b_ref, o_ref, *, n_rows):
    ct = ct_ref[...]
    a = ct * _sigmoid(ct)
    w = w_ref[...]
    rows = [jnp.sum(w * a[:, r:r + 1], axis=0, keepdims=True) for r in range(n_rows)]
    rows.append(jnp.zeros((SUBLANES - n_rows, w.shape[1]), F32))
    o_ref[...] = jnp.concatenate(rows, axis=0) + b_ref[...]


def _mod_params(c, c_ctx, w_mod, b_mod):
    depth, d, nmd = w_mod.shape
    bsz = c.shape[0]
    n_rows = bsz + 1
    assert n_rows <= SUBLANES
    ct = jnp.concatenate([c, c_ctx[None], jnp.zeros((SUBLANES - n_rows, d), F32)], axis=0).T
    tn = d
    out = pl.pallas_call(
        functools.partial(_mod_kernel, n_rows=n_rows),
        grid=(depth, nmd // tn),
        in_specs=[
            pl.BlockSpec((d, SUBLANES), lambda l, n: (0, 0)),
            pl.BlockSpec((None, d, tn), lambda l, n: (l, 0, n)),
            pl.BlockSpec((None, 1, tn), lambda l, n: (l, 0, n)),
        ],
        out_specs=pl.BlockSpec((None, SUBLANES, tn), lambda l, n: (l, 0, n)),
        out_shape=jax.ShapeDtypeStruct((depth, SUBLANES, nmd), F32),
        compiler_params=_cparams(("arbitrary", "arbitrary"), 32),
        name="adaln_params",
    )(ct, w_mod, b_mod.reshape(depth, 1, nmd))
    return out.reshape(depth, SUBLANES, N_MOD, d)


def _pick_rows(lat_ref, ctx_ref, n_lat_tiles):
    return jnp.where(pl.program_id(0) == n_lat_tiles, ctx_ref[...], lat_ref[...])


def _ffn_kernel(*refs, base, final, n_chunks, ctx_tile, x_has_ctx, mixer):
    refs = list(refs)
    x_ref = refs.pop(0)
    x = x_ref[...] if not x_has_ctx else _pick_rows(x_ref, refs.pop(0), ctx_tile)
    if mixer:
        a1_ref, a2_ref = refs.pop(0), refs.pop(0)
        if ctx_tile is None:
            a1, a2 = a1_ref[...], a2_ref[...]
        else:
            a1 = _pick_rows(a1_ref, refs.pop(0), ctx_tile)
            a2 = _pick_rows(a2_ref, refs.pop(0), ctx_tile)
        w1_ref, w2_ref = refs.pop(0), refs.pop(0)
    mod_ref, wg_ref, wu_ref, wd_ref = refs[:4]
    if final:
        fn_ref, o_ref, acc_ref = refs[4:]
    else:
        o_ref, acc_ref = refs[4:]
    m = mod_ref[...]
    if mixer:
        x = x + m[5:6] * (_dot(a1.astype(BF16), w1_ref[...].astype(BF16)) +
                          _dot(a2.astype(BF16), w2_ref[...].astype(BF16)))
    h = _modulate(x, m[base:base + 1], m[base + 1:base + 2]).astype(BF16)
    for c in range(n_chunks):
        sl = slice(c * FFN_CHUNK, (c + 1) * FFN_CHUNK)
        g = _dot(h, wg_ref[:, sl].astype(BF16))
        u = _dot(h, wu_ref[:, sl].astype(BF16))
        a = ((g * _sigmoid(g)) * u).astype(BF16)
        part = _dot(a, wd_ref[sl, :].astype(BF16))
        if c == 0:
            acc_ref[...] = part
        else:
            acc_ref[...] += part
    y = x + (FFN_RES_WEIGHT * m[base + 2:base + 3]) * acc_ref[...]
    if final:
        ms = jnp.mean(y * y, axis=-1, keepdims=True)
        y = (y * lax.rsqrt(ms + NORM_EPS)) * fn_ref[...]
    o_ref[...] = y


def _ffn(x, mod_l, ffn_w, layer, half, *, base, n_tiles, mod_row, final_norm=None, x_ctx=None, mixer=None):
    d = x.shape[1]
    wg, wu, wd = ffn_w
    dff = wg.shape[-1]
    assert dff % FFN_CHUNK == 0
    final = final_norm is not None
    resident = dict(pipeline_mode=pl.Buffered(1))
    const = lambda i: (0, 0)
    last_lat = n_tiles - 2
    lat_only = lambda i: (jnp.minimum(i, last_lat), 0)
    ctx_tile = None
    in_specs = [pl.BlockSpec((TM, d), lambda i: (i, 0))]
    args = [x]
    if x_ctx is not None:
        ctx_tile = n_tiles - 1
        assert x.shape[0] == ctx_tile * TM and x_ctx.shape == (TM, d)
        in_specs = [pl.BlockSpec((TM, d), lat_only), pl.BlockSpec((TM, d), const)]
        args.append(x_ctx)
    if mixer is not None:
        a1, a2, w_out, w_idx, a1_ctx, a2_ctx = mixer
        d1, d2 = a1.shape[1], a2.shape[1]
        assert d1 % d2 == 0
        if a1_ctx is None:
            in_specs += [pl.BlockSpec((TM, d1), lambda i: (i, 0)), pl.BlockSpec((TM, d2), lambda i: (i, 0))]
            args += [a1, a2]
        else:
            ctx_tile = n_tiles - 1
            assert a1.shape[0] == ctx_tile * TM and a1_ctx.shape[0] == TM and a2_ctx.shape[0] == TM
            in_specs += [pl.BlockSpec((TM, d1), lat_only), pl.BlockSpec((TM, d2), lat_only),
                         pl.BlockSpec((TM, d1), const), pl.BlockSpec((TM, d2), const)]
            args += [a1, a2, a1_ctx, a2_ctx]
        in_specs += [pl.BlockSpec((None, d1, d), lambda i: (w_idx, 0, 0), **resident),
                     pl.BlockSpec((None, d2, d), lambda i: (w_idx, d1 // d2, 0), **resident)]
        args += [w_out, w_out]
    sel = lambda i: (layer, half, 0, 0)
    in_specs += [
        pl.BlockSpec((None, N_MOD, d), lambda i: (mod_row(i), 0, 0)),
        pl.BlockSpec((None, None, d, dff), sel, **resident),
        pl.BlockSpec((None, None, d, dff), sel, **resident),
        pl.BlockSpec((None, None, dff, d), sel, **resident),
    ]
    args += [mod_l, wg, wu, wd]
    if final:
        in_specs.append(pl.BlockSpec((1, d), const))
        args.append(final_norm.reshape(1, d))
    return pl.pallas_call(
        functools.partial(_ffn_kernel, base=base, final=final, n_chunks=dff // FFN_CHUNK, ctx_tile=ctx_tile,
                          x_has_ctx=x_ctx is not None, mixer=mixer is not None),
        grid=(n_tiles,),
        in_specs=in_specs,
        out_specs=pl.BlockSpec((TM, d), lambda i: (i, 0)),
        out_shape=jax.ShapeDtypeStruct((n_tiles * TM, d), F32),
        scratch_shapes=[pltpu.VMEM((TM, d), F32)],
        compiler_params=_cparams(("arbitrary",), 60),
        name=("mix_ffn_final" if final else "mix_ffn") if mixer is not None else "ffn",
    )(*args)


def _swap_halves(z, low_half):
    n = z.shape[1]
    return jnp.where(low_half, pltpu.roll(z, n - HEAD_DIM // 2, 1), pltpu.roll(z, HEAD_DIM // 2, 1))


def _head_sumsq(z, ones_bd):
    zz = z * z
    hi = zz.astype(BF16)
    lo = (zz - hi.astype(F32)).astype(BF16)
    return _dot(hi, ones_bd) + _dot(lo, ones_bd)


def _inproj_ab_kernel(x_ref, mod_ref, w_ref, qn_ref, kn_ref, cos_ref, sin_ref, ones_ref, dft_ref,
                      q_ref, k_ref, vt_ref, y_ref):
    x = x_ref[...]
    m = mod_ref[...]
    h = _modulate(x, m[3:4], m[4:5]).astype(BF16)
    y = _dot(h, w_ref[...].astype(BF16))
    ones_bd = ones_ref[...]
    cos = jnp.concatenate([cos_ref[...]] * 2, axis=1)
    sin = jnp.concatenate([sin_ref[...]] * 2, axis=1)
    lane = lax.broadcasted_iota(jnp.int32, (1, MXU_DIM), 1)
    low_half = (lane % HEAD_DIM) < (HEAD_DIM // 2)

    def norm_rope(z, nw):
        ss = _head_sumsq(z, ones_bd)
        zn = (z * lax.rsqrt(ss * (1.0 / HEAD_DIM) + NORM_EPS)) * nw
        return zn * cos + _swap_halves(zn, low_half) * sin

    qn = qn_ref[...]
    kn = kn_ref[...]
    for blk in range(A_Q_W // MXU_DIM):
        sl = slice(blk * MXU_DIM, (blk + 1) * MXU_DIM)
        q_ref[:, sl] = (norm_rope(y[:, sl], qn) * Q_SCALE_LOG2).astype(BF16)
    k_ref[...] = norm_rope(y[:, A_Q_W:A_Q_W + A_KV_W], kn).astype(BF16)
    vt_ref[...] = _values_with_ones_t(y[:, A_Q_W + A_KV_W:A_Q_W + 2 * A_KV_W], A_KV_HEADS).astype(BF16)
    f = y[:, A_Q_W + 2 * A_KV_W:].astype(BF16)
    y_ref[...] = _dot(f, dft_ref[...])


def _inproj_ab(x, mod_l, w_in, w_idx, q_norm, k_norm, cos_t, sin_t, *, n_tiles, mod_row, rope_row):
    ntok, d = x.shape
    n_in = w_in.shape[-1]
    ones_bd = _const_bf16(np.kron(np.eye(MXU_DIM // HEAD_DIM), np.ones((HEAD_DIM, HEAD_DIM))))
    gc, gs = _cos_sin(B_GROUP_DIM)
    eye = np.eye(B_GROUPS)
    dft = _const_bf16(np.concatenate([np.kron(eye, gc), np.kron(eye, gs)], axis=1))
    qn = jnp.tile(q_norm, MXU_DIM // HEAD_DIM).reshape(1, MXU_DIM)
    kn = jnp.tile(k_norm, MXU_DIM // HEAD_DIM).reshape(1, MXU_DIM)
    const = lambda i: (0, 0)
    row = lambda i: (i, 0)
    return pl.pallas_call(
        _inproj_ab_kernel,
        grid=(n_tiles,),
        in_specs=[
            pl.BlockSpec((TM, d), row),
            pl.BlockSpec((None, N_MOD, d), lambda i: (mod_row(i), 0, 0)),
            pl.BlockSpec((None, d, n_in), lambda i: (w_idx, 0, 0)),
            pl.BlockSpec((1, MXU_DIM), const),
            pl.BlockSpec((1, MXU_DIM), const),
            pl.BlockSpec((TM, LANES), lambda i: (rope_row(i), 0)),
            pl.BlockSpec((TM, LANES), lambda i: (rope_row(i), 0)),
            pl.BlockSpec((MXU_DIM, MXU_DIM), const),
            pl.BlockSpec((B_W, 2 * B_W), const),
        ],
        out_specs=[
            pl.BlockSpec((TM, A_Q_W), row),
            pl.BlockSpec((TM, A_KV_W), row),
            pl.BlockSpec((A_KV_HEADS * V_GROUP, TM), lambda i: (0, i)),
            pl.BlockSpec((TM, 2 * B_W), row),
        ],
        out_shape=[
            jax.ShapeDtypeStruct((ntok, A_Q_W), BF16),
            jax.ShapeDtypeStruct((ntok, A_KV_W), BF16),
            jax.ShapeDtypeStruct((A_KV_HEADS * V_GROUP, ntok), BF16),
            jax.ShapeDtypeStruct((ntok, 2 * B_W), F32),
        ],
        compiler_params=_cparams(("arbitrary",), 40),
        name="in_proj_ab",
    )(x, mod_l, w_in, qn, kn, cos_t, sin_t, ones_bd, dft)


def _rope_tables(seq):
    t = np.arange(seq)
    row = (t // GRID_W).astype(np.float64)
    col = (t % GRID_W).astype(np.float64)
    n_ax = HEAD_DIM // 4
    inv = ROPE_THETA ** (-np.arange(n_ax, dtype=np.float64) / n_ax)
    ang = np.concatenate([row[:, None] * inv, col[:, None] * inv], axis=-1)
    cos, sin = np.cos(ang), np.sin(ang)
    cos_h = np.concatenate([cos, cos], axis=-1)
    sin_h = np.concatenate([-sin, sin], axis=-1)
    cos_t = np.concatenate([np.tile(cos_h, (1, 2)), np.ones((TM, LANES))], axis=0)
    sin_t = np.concatenate([np.tile(sin_h, (1, 2)), np.zeros((TM, LANES))], axis=0)
    return jnp.asarray(cos_t.astype(np.float32)), jnp.asarray(sin_t.astype(np.float32))


def _gqa_kernel(bound_ref, q_ref, kc_ref, vtc_ref, *rest, has_lat):
    if has_lat:
        kl_ref, vtl_ref, o_ref, qm_ref, m_ref, acc_ref = rest
    else:
        o_ref, qm_ref, m_ref, acc_ref = rest
    j = pl.program_id(2)
    group = A_Q_HEADS // A_KV_HEADS
    lane = lax.broadcasted_iota(jnp.int32, (1, LANES), 1)
    hi_half = lane >= HEAD_DIM
    n_chunks = q_ref.shape[0] // ATT_QCHUNK
    no_shift = bound_ref[0] <= SAFE_SCORE_BOUND_LOG2

    def update(k_ref, vt_ref, shifted):
        kbody = min(ATT_KBODY, k_ref.shape[0])
        bodies = [(kb, h, c) for kb in range(k_ref.shape[0] // kbody) for h in range(A_Q_HEADS) for c in range(n_chunks)]

        def scores(kb, h, c):
            kvp = (h // group) // 2
            kp = k_ref[kb * kbody:(kb + 1) * kbody, kvp * LANES:(kvp + 1) * LANES]
            return _dot_nt(kp, qm_ref[h, c * ATT_QCHUNK:(c + 1) * ATT_QCHUNK, :])

        def accumulate(h, c, alpha, pv):
            cs = slice(c * ATT_QCHUNK, (c + 1) * ATT_QCHUNK)
            prev = acc_ref[h, :, cs] if alpha is None else alpha * acc_ref[h, :, cs]
            acc_ref[h, :, cs] = prev + pv

        queue = [scores(*bodies[t]) for t in range(min(ATT_LOOKAHEAD, len(bodies)))]
        pending = None
        for idx, (kb, h, c) in enumerate(bodies):
            s = queue.pop(0)
            if idx + ATT_LOOKAHEAD < len(bodies):
                queue.append(scores(*bodies[idx + ATT_LOOKAHEAD]))
            cs = slice(c * ATT_QCHUNK, (c + 1) * ATT_QCHUNK)
            kvh = h // group
            vt = vt_ref[kvh * V_GROUP:(kvh + 1) * V_GROUP, kb * kbody:(kb + 1) * kbody]
            if shifted:
                m_prev = m_ref[h, :, cs]
                m_new = jnp.maximum(m_prev, jnp.max(s, axis=0, keepdims=True))
                alpha = jnp.exp2(m_prev - m_new)
                p = jnp.exp2(s - m_new).astype(BF16)
                m_ref[h, :, cs] = m_new
            else:
                alpha = None
                p = jnp.exp2(s).astype(BF16)
            pv = _dot(vt, p)
            if pending is not None:
                accumulate(*pending)
            pending = (h, c, alpha, pv)
        accumulate(*pending)

    def update_either(k_ref, vt_ref):
        @pl.when(no_shift)
        def _plain():
            update(k_ref, vt_ref, shifted=False)

        @pl.when(jnp.logical_not(no_shift))
        def _running_max():
            update(k_ref, vt_ref, shifted=True)

    @pl.when(j == 0)
    def _first():
        m_ref[...] = jnp.full(m_ref.shape, NEG_BIG, F32)
        acc_ref[...] = jnp.zeros(acc_ref.shape, F32)
        for h in range(A_Q_HEADS):
            qp = q_ref[:, (h // 2) * LANES:(h // 2 + 1) * LANES].astype(F32)
            kv_half = (h // group) % 2
            if kv_half != h % 2:
                qp = pltpu.roll(qp, HEAD_DIM, 1)
            keep = hi_half if kv_half else jnp.logical_not(hi_half)
            qm_ref[h] = jnp.where(keep, qp, jnp.zeros_like(qp)).astype(BF16)
        update_either(kc_ref, vtc_ref)

    if has_lat:
        @pl.when(j > 0)
        def _rest():
            update_either(kl_ref, vtl_ref)

    @pl.when(j == pl.num_programs(2) - 1)
    def _final():
        for pb in range(A_Q_HEADS // 2):
            for c in range(n_chunks):
                cs = slice(c * ATT_QCHUNK, (c + 1) * ATT_QCHUNK)
                parts = []
                for h in (2 * pb, 2 * pb + 1):
                    a = acc_ref[h, :, cs]
                    parts.append(a[:HEAD_DIM] / a[HEAD_DIM:HEAD_DIM + 1])
                o_ref[cs, pb * LANES:(pb + 1) * LANES] = jnp.concatenate(parts, axis=0).T.astype(BF16)


def _gqa_scratch(tq):
    return [
        pltpu.VMEM((A_Q_HEADS, tq, LANES), BF16),
        pltpu.VMEM((A_Q_HEADS, 1, tq), F32),
        pltpu.VMEM((A_Q_HEADS, V_GROUP, tq), F32),
    ]


def _gqa(q, k, vt, q_norm, k_norm, *, bsz, seq, lc):
    n_lat = bsz * seq
    tq, tk = ATT_TQ, ATT_TK
    ctx_blk = lambda b: n_lat // lc + b
    bound = (1.05 * HEAD_DIM * Q_SCALE_LOG2 * jnp.max(jnp.abs(q_norm)) * jnp.max(jnp.abs(k_norm))).reshape(1)
    smem = pl.BlockSpec(memory_space=pltpu.SMEM)
    o = pl.pallas_call(
        functools.partial(_gqa_kernel, has_lat=True),
        grid=(bsz, seq // tq, 1 + seq // tk),
        in_specs=[
            smem,
            pl.BlockSpec((tq, A_Q_W), lambda b, i, j: (b * (seq // tq) + i, 0)),
            pl.BlockSpec((lc, A_KV_W), lambda b, i, j: (ctx_blk(b), 0)),
            pl.BlockSpec((A_KV_HEADS * V_GROUP, lc), lambda b, i, j: (0, ctx_blk(b))),
            pl.BlockSpec((tk, A_KV_W), lambda b, i, j: (b * (seq // tk) + jnp.maximum(j - 1, 0), 0)),
            pl.BlockSpec((A_KV_HEADS * V_GROUP, tk), lambda b, i, j: (0, b * (seq // tk) + jnp.maximum(j - 1, 0))),
        ],
        out_specs=pl.BlockSpec((tq, A_Q_W), lambda b, i, j: (b * (seq // tq) + i, 0)),
        out_shape=jax.ShapeDtypeStruct((n_lat, A_Q_W), BF16),
        scratch_shapes=_gqa_scratch(tq),
        compiler_params=_cparams(("arbitrary", "arbitrary", "arbitrary"), 48),
        name="gqa_latent",
    )(bound, q, k, vt, k, vt)
    o_ctx = pl.pallas_call(
        functools.partial(_gqa_kernel, has_lat=False),
        grid=(bsz, 1, 1),
        in_specs=[
            smem,
            pl.BlockSpec((lc, A_Q_W), lambda b, i, j: (ctx_blk(b), 0)),
            pl.BlockSpec((lc, A_KV_W), lambda b, i, j: (ctx_blk(b), 0)),
            pl.BlockSpec((A_KV_HEADS * V_GROUP, lc), lambda b, i, j: (0, ctx_blk(b))),
        ],
        out_specs=pl.BlockSpec((lc, A_Q_W), lambda b, i, j: (b, 0)),
        out_shape=jax.ShapeDtypeStruct((bsz * lc, A_Q_W), BF16),
        scratch_shapes=_gqa_scratch(lc),
        compiler_params=_cparams(("arbitrary", "arbitrary", "arbitrary"), 32),
        name="gqa_context",
    )(bound, q, k, vt)
    return o, o_ctx


def _dft_a_kernel(y_ref, ca_ref, tc_ref, ts_ref, d_ref, *, nbb):
    ca = ca_ref[...]
    ra = ca.shape[1]
    for j in range(nbb):
        yj = y_ref[:, j, :].astype(BF16)
        p = _dot(ca, yj)
        br = p[:ra, :B_W] - p[ra:, B_W:]
        bi = -p[:ra, B_W:] - p[ra:, :B_W]
        tc = jnp.concatenate([tc_ref[j]] * (B_W // LANES), axis=1)
        ts = jnp.concatenate([ts_ref[j]] * (B_W // LANES), axis=1)
        d_ref[:, 0, j, :] = br * tc + bi * ts
        d_ref[:, 1, j, :] = bi * tc - br * ts


def _dft_b_kernel(d_ref, cb_ref, o_ref, *, kb, scale):
    cb = cb_ref[...]
    for j in range(kb):
        o_ref[:, j, :] = _dot(cb, d_ref[j].astype(BF16)) * scale


def _dft_ctx_kernel(y_ref, cs_ref, o_ref, *, scale):
    y = y_ref[...]
    cs = cs_ref[...]
    lc = cs.shape[0]
    yc = y[:, :B_W].astype(BF16)
    ys = y[:, B_W:].astype(BF16)
    o_ref[...] = (_dot(cs[:, :lc], yc) - _dot(cs[:, lc:], ys)) * scale


def _fourier(y, *, bsz, seq, lc):
    ntok = y.shape[0]
    ra = seq // GRID_W
    nb = GRID_W
    nbb = SUBLANES
    kb = SUBLANES
    assert ntok % nb == 0 and ra % SUBLANES == 0 and (ntok // nb) % 1 == 0
    ca_c, ca_s = _cos_sin(ra)
    ca = _const_bf16(np.concatenate([ca_c, ca_s], axis=0))
    th = 2.0 * np.pi * np.outer(np.arange(nb), np.arange(ra)) / seq
    tc = jnp.asarray(np.broadcast_to(np.cos(th)[:, :, None], (nb, ra, LANES)).astype(np.float32))
    ts = jnp.asarray(np.broadcast_to(np.sin(th)[:, :, None], (nb, ra, LANES)).astype(np.float32))
    cb_c, cb_s = _cos_sin(nb)
    cb = _const_bf16(np.concatenate([cb_c, cb_s], axis=1))

    y3 = y.reshape(ntok // nb, nb, 2 * B_W)
    d = pl.pallas_call(
        functools.partial(_dft_a_kernel, nbb=nbb),
        grid=(bsz, nb // nbb),
        in_specs=[
            pl.BlockSpec((ra, nbb, 2 * B_W), lambda b, i: (b, i, 0)),
            pl.BlockSpec((2 * ra, ra), lambda b, i: (0, 0)),
            pl.BlockSpec((nbb, ra, LANES), lambda b, i: (i, 0, 0)),
            pl.BlockSpec((nbb, ra, LANES), lambda b, i: (i, 0, 0)),
        ],
        out_specs=pl.BlockSpec((None, ra, 2, nbb, B_W), lambda b, i: (b, 0, 0, i, 0)),
        out_shape=jax.ShapeDtypeStruct((bsz, ra, 2, nb, B_W), F32),
        compiler_params=_cparams(("arbitrary", "arbitrary"), 32),
        name="fourier_rows",
    )(y3, ca, tc, ts)
    d = d.reshape(bsz, ra, 2 * nb, B_W)

    scale = 1.0 / math.sqrt(seq * B_GROUP_DIM)
    n_out = bsz * nb
    z3 = pl.pallas_call(
        functools.partial(_dft_b_kernel, kb=kb, scale=scale),
        grid=(bsz, ra // kb),
        in_specs=[
            pl.BlockSpec((None, kb, 2 * nb, B_W), lambda b, i: (b, i, 0, 0)),
            pl.BlockSpec((nb, 2 * nb), lambda b, i: (0, 0)),
        ],
        out_specs=pl.BlockSpec((nb, kb, B_W), lambda b, i: (b, i, 0)),
        out_shape=jax.ShapeDtypeStruct((n_out, ra, B_W), F32),
        compiler_params=_cparams(("arbitrary", "arbitrary"), 32),
        name="fourier_cols",
    )(d, cb)
    z = z3.reshape(bsz * seq, B_W)

    cc, cs_ = _cos_sin(lc)
    csm = _const_bf16(np.concatenate([cc, cs_], axis=1))
    ctx_blk = lambda b: (bsz * seq) // lc + b
    z_ctx = pl.pallas_call(
        functools.partial(_dft_ctx_kernel, scale=1.0 / math.sqrt(lc * B_GROUP_DIM)),
        grid=(bsz,),
        in_specs=[
            pl.BlockSpec((lc, 2 * B_W), lambda b: (ctx_blk(b), 0)),
            pl.BlockSpec((lc, 2 * lc), lambda b: (0, 0)),
        ],
        out_specs=pl.BlockSpec((lc, B_W), lambda b: (b, 0)),
        out_shape=jax.ShapeDtypeStruct((bsz * lc, B_W), F32),
        compiler_params=_cparams(("arbitrary",), 32),
        name="fourier_context",
    )(y, csm)
    return z, z_ctx


def _values_with_ones_t(v, n_heads):
    vt = v.T
    ones = jnp.ones((ONES_ROWS, vt.shape[1]), F32)
    groups = []
    for h in range(n_heads):
        groups += [vt[h * HEAD_DIM:(h + 1) * HEAD_DIM], ones]
    return jnp.concatenate(groups, axis=0)


def _inproj_cd_kernel(x_ref, mod_ref, w_ref, u_ref, q_ref, k_ref, vt_ref):
    x = x_ref[...]
    m = mod_ref[...]
    h = _modulate(x, m[3:4], m[4:5]).astype(BF16)
    y = _dot(h, w_ref[...].astype(BF16))
    c = C_CHANNELS
    u_ref[...] = y[:, :c] * _sigmoid(y[:, c:2 * c])
    q_ref[...] = (y[:, 2 * c:2 * c + D_W] * Q_SCALE_LOG2).astype(BF16)
    k_ref[...] = y[:, 2 * c + D_W:2 * c + 2 * D_W].astype(BF16)
    vt_ref[...] = _values_with_ones_t(y[:, 2 * c + 2 * D_W:], D_HEADS).astype(BF16)


def _inproj_cd(x, mod_l, w_in, w_idx, *, n_tiles, mod_row):
    ntok, d = x.shape
    n_in = w_in.shape[-1]
    row = lambda i: (i, 0)
    return pl.pallas_call(
        _inproj_cd_kernel,
        grid=(n_tiles,),
        in_specs=[
            pl.BlockSpec((TM, d), row),
            pl.BlockSpec((None, N_MOD, d), lambda i: (mod_row(i), 0, 0)),
            pl.BlockSpec((None, d, n_in), lambda i: (w_idx, 0, 0)),
        ],
        out_specs=[
            pl.BlockSpec((TM, C_CHANNELS), row),
            pl.BlockSpec((TM, D_W), row),
            pl.BlockSpec((TM, D_W), row),
            pl.BlockSpec((D_HEADS * V_GROUP, TM), lambda i: (0, i)),
        ],
        out_shape=[
            jax.ShapeDtypeStruct((ntok, C_CHANNELS), F32),
            jax.ShapeDtypeStruct((ntok, D_W), BF16),
            jax.ShapeDtypeStruct((ntok, D_W), BF16),
            jax.ShapeDtypeStruct((D_HEADS * V_GROUP, ntok), BF16),
        ],
        compiler_params=_cparams(("arbitrary",), 40),
        name="in_proj_cd",
    )(x, mod_l, w_in)


def _conv_kernel(prev_ref, cur_ref, next_ref, w_ref, b_ref, lnw_ref, lnb_ref, o_ref, buf_ref, shift_ref, *,
                 tiles_per_seq):
    i = pl.program_id(1)
    halo = CONV_HALO
    tm = cur_ref.shape[0]
    zeros = jnp.zeros((halo, cur_ref.shape[1]), F32)
    buf_ref[0:halo, :] = jnp.where(i > 0, prev_ref[...], zeros)
    buf_ref[halo:halo + tm, :] = cur_ref[...]
    buf_ref[halo + tm:, :] = jnp.where(i < tiles_per_seq - 1, next_ref[...], zeros)
    w = w_ref[...]
    off = halo - C_KERNEL // 2
    y = None
    for r in range(SUBLANES):
        taps = [t for t in range(C_KERNEL) if (t + off) % SUBLANES == r]
        part = None
        for t in taps:
            start = t + off - r
            term = buf_ref[start:start + tm + SUBLANES, :] * w[t:t + 1]
            part = term if part is None else part + term
        if r == 0:
            shifted = part[:tm]
        else:
            shift_ref[...] = part
            shifted = shift_ref[r:r + tm, :]
        y = shifted if y is None else y + shifted
    y = y + b_ref[...]
    mu = jnp.mean(y, axis=-1, keepdims=True)
    yc = y - mu
    var = jnp.mean(yc * yc, axis=-1, keepdims=True)
    z = (yc * lax.rsqrt(var + NORM_EPS)) * lnw_ref[...] + lnb_ref[...]
    o_ref[...] = (z * _sigmoid(z)).astype(BF16)


def _conv_module(u, dw_w, dw_b, ln_w, ln_b, *, bsz, seq):
    c = u.shape[1]
    tiles = seq // TM
    hpt = TM // CONV_HALO
    n_halo_blocks = u.shape[0] // CONV_HALO
    cur = lambda b, i: (b * tiles + i, 0)
    prev = lambda b, i: (jnp.maximum((b * tiles + i) * hpt - 1, 0), 0)
    nxt = lambda b, i: (jnp.minimum((b * tiles + i + 1) * hpt, n_halo_blocks - 1), 0)
    vec = lambda b, i: (0, 0)
    return pl.pallas_call(
        functools.partial(_conv_kernel, tiles_per_seq=tiles),
        grid=(bsz, tiles),
        in_specs=[
            pl.BlockSpec((CONV_HALO, c), prev),
            pl.BlockSpec((TM, c), cur),
            pl.BlockSpec((CONV_HALO, c), nxt),
            pl.BlockSpec((C_KERNEL, c), vec),
            pl.BlockSpec((1, c), vec),
            pl.BlockSpec((1, c), vec),
            pl.BlockSpec((1, c), vec),
        ],
        out_specs=pl.BlockSpec((TM, c), cur),
        out_shape=jax.ShapeDtypeStruct((bsz * seq, c), BF16),
        scratch_shapes=[pltpu.VMEM((TM + 2 * CONV_HALO, c), F32), pltpu.VMEM((TM + SUBLANES, c), F32)],
        compiler_params=_cparams(("arbitrary", "arbitrary"), 32),
        name="conv_module",
    )(u, u, u, dw_w, dw_b.reshape(1, c), ln_w.reshape(1, c), ln_b.reshape(1, c))


def _na_kernel(q_ref, k0_ref, k1_ref, k2_ref, vt0_ref, vt1_ref, vt2_ref, kc_ref, vtc_ref, bias_ref, o_ref):
    lane = lax.broadcasted_iota(jnp.int32, (1, LANES), 1)
    hi_half = lane >= HEAD_DIM

    def scores(h):
        sl = slice((h // 2) * LANES, (h // 2 + 1) * LANES)
        qp = q_ref[:, sl]
        keep = hi_half if h % 2 else jnp.logical_not(hi_half)
        qm = jnp.where(keep, qp, jnp.zeros_like(qp))
        tq = qp.shape[0]
        near = [_dot_nt(k_ref[:, sl], qm) + bias_ref[h, n * tq:(n + 1) * tq, :]
                for n, k_ref in enumerate((k0_ref, k1_ref, k2_ref))]
        return near + [_dot_nt(kc_ref[:, sl], qm)]

    lookahead = min(NA_LOOKAHEAD, D_HEADS)
    queue = [scores(h) for h in range(lookahead)]
    outs = []
    for h in range(D_HEADS):
        tiles = queue.pop(0)
        if h + lookahead < D_HEADS:
            queue.append(scores(h + lookahead))
        mx = functools.reduce(jnp.maximum, [jnp.max(t, axis=0, keepdims=True) for t in tiles])
        p = jnp.concatenate([jnp.exp2(t - mx).astype(BF16) for t in tiles], axis=0)
        rs = slice(h * V_GROUP, (h + 1) * V_GROUP)
        vt = jnp.concatenate([vt0_ref[rs, :], vt1_ref[rs, :], vt2_ref[rs, :], vtc_ref[rs, :]], axis=1)
        outs.append(_dot(vt, p))
    for pb in range(D_HEADS // 2):
        parts = [outs[h][:HEAD_DIM] / outs[h][HEAD_DIM:HEAD_DIM + 1] for h in (2 * pb, 2 * pb + 1)]
        o_ref[:, pb * LANES:(pb + 1) * LANES] = jnp.concatenate(parts, axis=0).T.astype(BF16)


def _na_bias_kernel(rpb_ref, o_ref, *, rows):
    h = pl.program_id(0)
    n_ro, n_co = 2 * NA_WIN_ROWS - 1, 2 * NA_WIN_COLS - 1
    kc = lax.broadcasted_iota(jnp.int32, (GRID_W, LANES), 0)
    lane = lax.broadcasted_iota(jnp.int32, (GRID_W, LANES), 1)
    qc = lane % GRID_W
    d = kc - qc + (NA_WIN_COLS - 1)
    cs = jnp.clip(qc - NA_WIN_COLS // 2, 0, GRID_W - NA_WIN_COLS)
    col_ok = (kc >= cs) & (kc < cs + NA_WIN_COLS)
    neg = jnp.full((GRID_W, LANES), NEG_BIG, F32)
    log2e = math.log2(math.e)
    band = []
    for ro in range(n_ro):
        acc = neg
        for co in range(n_co):
            acc = jnp.where(d == co, rpb_ref[(h * n_ro + ro) * n_co + co] * log2e, acc)
        band.append(jnp.where(col_ok, acc, neg))
    qr = NA_QROWS
    for kind, r0 in enumerate((0, qr, rows - qr)):
        for b in range(3 * qr):
            rk = r0 - qr + b
            for pair in range(qr // 2):
                halves = []
                for a in (2 * pair, 2 * pair + 1):
                    r = r0 + a
                    start = min(max(r - NA_WIN_ROWS // 2, 0), rows - NA_WIN_ROWS)
                    ok = start <= rk < start + NA_WIN_ROWS and 0 <= rk < rows
                    halves.append(band[rk - r + NA_WIN_ROWS - 1] if ok else neg)
                o_ref[kind, b * GRID_W:(b + 1) * GRID_W, pair * LANES:(pair + 1) * LANES] = jnp.where(
                    lane < GRID_W, halves[0], halves[1])


def _na_bias(rpb, rows):
    heads = rpb.shape[0]
    tq = NA_QROWS * GRID_W
    return pl.pallas_call(
        functools.partial(_na_bias_kernel, rows=rows),
        grid=(heads,),
        in_specs=[pl.BlockSpec(memory_space=pltpu.SMEM)],
        out_specs=pl.BlockSpec((3, None, 3 * tq, tq), lambda h: (0, h, 0, 0)),
        out_shape=jax.ShapeDtypeStruct((3, heads, 3 * tq, tq), F32),
        compiler_params=_cparams(("arbitrary",), 32),
        name="na_bias",
    )(rpb.reshape(-1))


def _na(q, k, vt, rpb, *, bsz, seq, lc):
    rows = seq // GRID_W
    tq = NA_QROWS * GRID_W
    nblk = seq // tq
    assert rows >= NA_WIN_ROWS and rows % NA_QROWS == 0 and NA_WIN_ROWS == 2 * NA_QROWS and nblk >= 3
    assert lc == tq
    bias = _na_bias(rpb, rows)
    n_lat = bsz * seq
    qmap = lambda b, i: (b * nblk + i, 0)

    def near(off, transposed):
        def index(b, i):
            blk = b * nblk + jnp.clip(i + off, 0, nblk - 1)
            return (0, blk) if transposed else (blk, 0)
        return index

    kind = lambda b, i: (jnp.where(i == 0, 0, jnp.where(i == nblk - 1, 2, 1)), 0, 0, 0)
    rows_blk = lambda m: pl.BlockSpec((tq, D_W), m)
    cols_blk = lambda m: pl.BlockSpec((D_HEADS * V_GROUP, tq), m)
    return pl.pallas_call(
        _na_kernel,
        grid=(bsz, nblk),
        in_specs=[
            rows_blk(qmap),
            rows_blk(near(-1, False)), rows_blk(near(0, False)), rows_blk(near(1, False)),
            cols_blk(near(-1, True)), cols_blk(near(0, True)), cols_blk(near(1, True)),
            rows_blk(lambda b, i: (n_lat // lc + b, 0)),
            cols_blk(lambda b, i: (0, n_lat // lc + b)),
            pl.BlockSpec((None, D_HEADS, 3 * tq, tq), kind),
        ],
        out_specs=rows_blk(qmap),
        out_shape=jax.ShapeDtypeStruct((n_lat, D_W), BF16),
        compiler_params=_cparams(("arbitrary", "arbitrary"), 48),
        name="neighbourhood_attention",
    )(q, k, k, k, vt, vt, vt, k, vt, bias)


def kernel(x, c, ctx, c_ctx, w_mod, b_mod, ffn_w_gate, ffn_w_up, ffn_w_down, ab_w_in, ab_w_out, ab_q_norm,
           ab_k_norm, cd_w_in, cd_w_out, cd_dw_w, cd_dw_b, cd_ln_w, cd_ln_b, cd_rpb, final_norm):
    bsz, seq, d = x.shape
    lc = ctx.shape[1]
    depth = w_mod.shape[0]
    assert depth == 2, "layer 0 (A/B mixer) updates the context, layer 1 (C/D mixer) is the last layer"
    assert bsz * lc == TM and seq % TM == 0 and seq % ATT_TQ == 0 and seq % ATT_TK == 0
    n_lat_tiles = bsz * seq // TM
    n_all_tiles = n_lat_tiles + 1
    tiles_per_seq = seq // TM
    mod_row = lambda i: jnp.where(i == n_lat_tiles, bsz, i // tiles_per_seq)
    rope_row = lambda i: jnp.where(i == n_lat_tiles, tiles_per_seq, i % tiles_per_seq)

    mod = _mod_params(c, c_ctx, w_mod, b_mod)
    cos_t, sin_t = _rope_tables(seq)

    ffn_w = (ffn_w_gate, ffn_w_up, ffn_w_down)

    xt = x.reshape(bsz * seq, d)
    for layer in range(depth):
        last = layer == depth - 1
        i = layer // 2
        mod_l = mod[layer]
        xt = _ffn(xt, mod_l, ffn_w, layer, 0, base=0, n_tiles=n_all_tiles, mod_row=mod_row,
                  x_ctx=ctx.reshape(bsz * lc, d) if layer == 0 else None)
        if layer % 2 == 0:
            q, k, vt, y = _inproj_ab(xt, mod_l, ab_w_in, i, ab_q_norm[i], ab_k_norm[i], cos_t, sin_t,
                                     n_tiles=n_all_tiles, mod_row=mod_row, rope_row=rope_row)
            a1, a1_ctx = _gqa(q, k, vt, ab_q_norm[i], ab_k_norm[i], bsz=bsz, seq=seq, lc=lc)
            a2, a2_ctx = _fourier(y, bsz=bsz, seq=seq, lc=lc)
            mixer = (a1, a2, ab_w_out, i, a1_ctx, a2_ctx)
        else:
            u, q, k, vt = _inproj_cd(xt, mod_l, cd_w_in, i, n_tiles=n_all_tiles, mod_row=mod_row)
            a1 = _conv_module(u, cd_dw_w[i], cd_dw_b[i], cd_ln_w[i], cd_ln_b[i], bsz=bsz, seq=seq)
            a2 = _na(q, k, vt, cd_rpb[i], bsz=bsz, seq=seq, lc=lc)
            mixer = (a1, a2, cd_w_out, i, None, None)
        assert last == (mixer[4] is None)
        xt = _ffn(xt, mod_l, ffn_w, layer, 1, base=6, n_tiles=n_lat_tiles if last else n_all_tiles, mod_row=mod_row,
                  final_norm=final_norm if last else None, mixer=mixer)
    return xt.reshape(bsz, seq, d)
```

```python
import functools
import math

import numpy as np
import jax
import jax.numpy as jnp
from jax import lax
from jax.experimental import pallas as pl
from jax.experimental.pallas import tpu as pltpu

F32 = jnp.float32
BF16 = jnp.bfloat16

GRID_W = 64
HEAD_DIM = 64
A_Q_HEADS = 12
A_KV_HEADS = 4
B_GROUPS = 4
B_GROUP_DIM = 64
C_CHANNELS = 512
C_KERNEL = 31
D_HEADS = 8
NA_WIN_ROWS = 8
NA_WIN_COLS = 16
ROPE_THETA = 10000.0
NORM_EPS = 1e-6
N_MOD = 9
FFN_RES_WEIGHT = 0.5

A_Q_W = A_Q_HEADS * HEAD_DIM
A_KV_W = A_KV_HEADS * HEAD_DIM
B_W = B_GROUPS * B_GROUP_DIM
D_W = D_HEADS * HEAD_DIM

LANES = 128
SUBLANES = 8
MXU_DIM = 256

TM = 512
FFN_CHUNK = 256
ATT_TQ = 1024
ATT_TK = 1024
ATT_KBODY = 256
NA_QROWS = 4
NA_LOOKAHEAD = 4
CONV_HALO = 16
ATT_QCHUNK = 256
ATT_LOOKAHEAD = 4
SAFE_SCORE_BOUND_LOG2 = 100.0
ONES_ROWS = 16
V_GROUP = HEAD_DIM + ONES_ROWS
NEG_BIG = -1e30
Q_SCALE_LOG2 = (HEAD_DIM ** -0.5) * math.log2(math.e)

_MiB = 1 << 20


def _cparams(sem, vmem_mib):
    return pltpu.CompilerParams(dimension_semantics=sem, vmem_limit_bytes=vmem_mib * _MiB)


def _dot(a, b):
    return jnp.dot(a, b, preferred_element_type=F32)


def _dot_nt(a, b):
    return lax.dot_general(a, b, (((1,), (1,)), ((), ())), preferred_element_type=F32)


def _sigmoid(x):
    return 1.0 / (1.0 + jnp.exp(-x))


def _cos_sin(n):
    idx = np.arange(n)
    ang = 2.0 * np.pi * (np.outer(idx, idx) % n) / n
    return np.cos(ang), np.sin(ang)


def _const_bf16(a):
    return jnp.asarray(np.asarray(a, np.float32)).astype(BF16)


def _modulate(x, shift, scale):
    ms = jnp.mean(x * x, axis=-1, keepdims=True)
    return (x * lax.rsqrt(ms + NORM_EPS)) * (1.0 + scale) + shift


def _mod_kernel(ct_ref, w_ref, b_ref, o_ref, *, n_rows):
    ct = ct_ref[...]
    a = ct * _sigmoid(ct)
    w = w_ref[...]
    rows = [jnp.sum(w * a[:, r:r + 1], axis=0, keepdims=True) for r in range(n_rows)]
    rows.append(jnp.zeros((SUBLANES - n_rows, w.shape[1]), F32))
    o_ref[...] = jnp.concatenate(rows, axis=0) + b_ref[...]


def _mod_params(c, c_ctx, w_mod, b_mod):
    depth, d, nmd = w_mod.shape
    bsz = c.shape[0]
    n_rows = bsz + 1
    assert n_rows <= SUBLANES
    ct = jnp.concatenate([c, c_ctx[None], jnp.zeros((SUBLANES - n_rows, d), F32)], axis=0).T
    tn = d
    out = pl.pallas_call(
        functools.partial(_mod_kernel, n_rows=n_rows),
        grid=(depth, nmd // tn),
        in_specs=[
            pl.BlockSpec((d, SUBLANES), lambda l, n: (0, 0)),
            pl.BlockSpec((None, d, tn), lambda l, n: (l, 0, n)),
            pl.BlockSpec((None, 1, tn), lambda l, n: (l, 0, n)),
        ],
        out_specs=pl.BlockSpec((None, SUBLANES, tn), lambda l, n: (l, 0, n)),
        out_shape=jax.ShapeDtypeStruct((depth, SUBLANES, nmd), F32),
        compiler_params=_cparams(("arbitrary", "arbitrary"), 32),
        name="adaln_params",
    )(ct, w_mod, b_mod.reshape(depth, 1, nmd))
    return out.reshape(depth, SUBLANES, N_MOD, d)


def _pick_rows(lat_ref, ctx_ref, n_lat_tiles):
    return jnp.where(pl.program_id(0) == n_lat_tiles, ctx_ref[...], lat_ref[...])


def _ffn_kernel(*refs, base, final, n_chunks, ctx_tile, x_has_ctx, mixer):
    refs = list(refs)
    x_ref = refs.pop(0)
    x = x_ref[...] if not x_has_ctx else _pick_rows(x_ref, refs.pop(0), ctx_tile)
    if mixer:
        a1_ref, a2_ref = refs.pop(0), refs.pop(0)
        if ctx_tile is None:
            a1, a2 = a1_ref[...], a2_ref[...]
        else:
            a1 = _pick_rows(a1_ref, refs.pop(0), ctx_tile)
            a2 = _pick_rows(a2_ref, refs.pop(0), ctx_tile)
        w1_ref, w2_ref = refs.pop(0), refs.pop(0)
    mod_ref, wg_ref, wu_ref, wd_ref = refs[:4]
    if final:
        fn_ref, o_ref, acc_ref = refs[4:]
    else:
        o_ref, acc_ref = refs[4:]
    m = mod_ref[...]
    if mixer:
        x = x + m[5:6] * (_dot(a1.astype(BF16), w1_ref[...].astype(BF16)) +
                          _dot(a2.astype(BF16), w2_ref[...].astype(BF16)))
    h = _modulate(x, m[base:base + 1], m[base + 1:base + 2]).astype(BF16)
    for c in range(n_chunks):
        sl = slice(c * FFN_CHUNK, (c + 1) * FFN_CHUNK)
        g = _dot(h, wg_ref[:, sl].astype(BF16))
        u = _dot(h, wu_ref[:, sl].astype(BF16))
        a = ((g * _sigmoid(g)) * u).astype(BF16)
        part = _dot(a, wd_ref[sl, :].astype(BF16))
        if c == 0:
            acc_ref[...] = part
        else:
            acc_ref[...] += part
    y = x + (FFN_RES_WEIGHT * m[base + 2:base + 3]) * acc_ref[...]
    if final:
        ms = jnp.mean(y * y, axis=-1, keepdims=True)
        y = (y * lax.rsqrt(ms + NORM_EPS)) * fn_ref[...]
    o_ref[...] = y


def _ffn(x, mod_l, ffn_w, layer, half, *, base, n_tiles, mod_row, final_norm=None, x_ctx=None, mixer=None):
    d = x.shape[1]
    wg, wu, wd = ffn_w
    dff = wg.shape[-1]
    assert dff % FFN_CHUNK == 0
    final = final_norm is not None
    resident = dict(pipeline_mode=pl.Buffered(1))
    const = lambda i: (0, 0)
    last_lat = n_tiles - 2
    lat_only = lambda i: (jnp.minimum(i, last_lat), 0)
    ctx_tile = None
    in_specs = [pl.BlockSpec((TM, d), lambda i: (i, 0))]
    args = [x]
    if x_ctx is not None:
        ctx_tile = n_tiles - 1
        assert x.shape[0] == ctx_tile * TM and x_ctx.shape == (TM, d)
        in_specs = [pl.BlockSpec((TM, d), lat_only), pl.BlockSpec((TM, d), const)]
        args.append(x_ctx)
    if mixer is not None:
        a1, a2, w_out, w_idx, a1_ctx, a2_ctx = mixer
        d1, d2 = a1.shape[1], a2.shape[1]
        assert d1 % d2 == 0
        if a1_ctx is None:
            in_specs += [pl.BlockSpec((TM, d1), lambda i: (i, 0)), pl.BlockSpec((TM, d2), lambda i: (i, 0))]
            args += [a1, a2]
        else:
            ctx_tile = n_tiles - 1
            assert a1.shape[0] == ctx_tile * TM and a1_ctx.shape[0] == TM and a2_ctx.shape[0] == TM
            in_specs += [pl.BlockSpec((TM, d1), lat_only), pl.BlockSpec((TM, d2), lat_only),
                         pl.BlockSpec((TM, d1), const), pl.BlockSpec((TM, d2), const)]
            args += [a1, a2, a1_ctx, a2_ctx]
        in_specs += [pl.BlockSpec((None, d1, d), lambda i: (w_idx, 0, 0), **resident),
                     pl.BlockSpec((None, d2, d), lambda i: (w_idx, d1 // d2, 0), **resident)]
        args += [w_out, w_out]
    sel = lambda i: (layer, half, 0, 0)
    in_specs += [
        pl.BlockSpec((None, N_MOD, d), lambda i: (mod_row(i), 0, 0)),
        pl.BlockSpec((None, None, d, dff), sel, **resident),
        pl.BlockSpec((None, None, d, dff), sel, **resident),
        pl.BlockSpec((None, None, dff, d), sel, **resident),
    ]
    args += [mod_l, wg, wu, wd]
    if final:
        in_specs.append(pl.BlockSpec((1, d), const))
        args.append(final_norm.reshape(1, d))
    return pl.pallas_call(
        functools.partial(_ffn_kernel, base=base, final=final, n_chunks=dff // FFN_CHUNK, ctx_tile=ctx_tile,
                          x_has_ctx=x_ctx is not None, mixer=mixer is not None),
        grid=(n_tiles,),
        in_specs=in_specs,
        out_specs=pl.BlockSpec((TM, d), lambda i: (i, 0)),
        out_shape=jax.ShapeDtypeStruct((n_tiles * TM, d), F32),
        scratch_shapes=[pltpu.VMEM((TM, d), F32)],
        compiler_params=_cparams(("arbitrary",), 60),
        name=("mix_ffn_final" if final else "mix_ffn") if mixer is not None else "ffn",
    )(*args)


def _swap_halves(z, low_half):
    n = z.shape[1]
    return jnp.where(low_half, pltpu.roll(z, n - HEAD_DIM // 2, 1), pltpu.roll(z, HEAD_DIM // 2, 1))


def _head_sumsq(z, ones_bd):
    zz = z * z
    hi = zz.astype(BF16)
    lo = (zz - hi.astype(F32)).astype(BF16)
    return _dot(hi, ones_bd) + _dot(lo, ones_bd)


def _inproj_ab_kernel(x_ref, mod_ref, w_ref, qn_ref, kn_ref, cos_ref, sin_ref, ones_ref, dft_ref,
                      q_ref, k_ref, vt_ref, y_ref):
    x = x_ref[...]
    m = mod_ref[...]
    h = _modulate(x, m[3:4], m[4:5]).astype(BF16)
    y = _dot(h, w_ref[...].astype(BF16))
    ones_bd = ones_ref[...]
    cos = jnp.concatenate([cos_ref[...]] * 2, axis=1)
    sin = jnp.concatenate([sin_ref[...]] * 2, axis=1)
    lane = lax.broadcasted_iota(jnp.int32, (1, MXU_DIM), 1)
    low_half = (lane % HEAD_DIM) < (HEAD_DIM // 2)

    def norm_rope(z, nw):
        ss = _head_sumsq(z, ones_bd)
        zn = (z * lax.rsqrt(ss * (1.0 / HEAD_DIM) + NORM_EPS)) * nw
        return zn * cos + _swap_halves(zn, low_half) * sin

    qn = qn_ref[...]
    kn = kn_ref[...]
    for blk in range(A_Q_W // MXU_DIM):
        sl = slice(blk * MXU_DIM, (blk + 1) * MXU_DIM)
        q_ref[:, sl] = (norm_rope(y[:, sl], qn) * Q_SCALE_LOG2).astype(BF16)
    k_ref[...] = norm_rope(y[:, A_Q_W:A_Q_W + A_KV_W], kn).astype(BF16)
    vt_ref[...] = _values_with_ones_t(y[:, A_Q_W + A_KV_W:A_Q_W + 2 * A_KV_W], A_KV_HEADS).astype(BF16)
    f = y[:, A_Q_W + 2 * A_KV_W:].astype(BF16)
    y_ref[...] = _dot(f, dft_ref[...])


def _inproj_ab(x, mod_l, w_in, w_idx, q_norm, k_norm, cos_t, sin_t, *, n_tiles, mod_row, rope_row):
    ntok, d = x.shape
    n_in = w_in.shape[-1]
    ones_bd = _const_bf16(np.kron(np.eye(MXU_DIM // HEAD_DIM), np.ones((HEAD_DIM, HEAD_DIM))))
    gc, gs = _cos_sin(B_GROUP_DIM)
    eye = np.eye(B_GROUPS)
    dft = _const_bf16(np.concatenate([np.kron(eye, gc), np.kron(eye, gs)], axis=1))
    qn = jnp.tile(q_norm, MXU_DIM // HEAD_DIM).reshape(1, MXU_DIM)
    kn = jnp.tile(k_norm, MXU_DIM // HEAD_DIM).reshape(1, MXU_DIM)
    const = lambda i: (0, 0)
    row = lambda i: (i, 0)
    return pl.pallas_call(
        _inproj_ab_kernel,
        grid=(n_tiles,),
        in_specs=[
            pl.BlockSpec((TM, d), row),
            pl.BlockSpec((None, N_MOD, d), lambda i: (mod_row(i), 0, 0)),
            pl.BlockSpec((None, d, n_in), lambda i: (w_idx, 0, 0)),
            pl.BlockSpec((1, MXU_DIM), const),
            pl.BlockSpec((1, MXU_DIM), const),
            pl.BlockSpec((TM, LANES), lambda i: (rope_row(i), 0)),
            pl.BlockSpec((TM, LANES), lambda i: (rope_row(i), 0)),
            pl.BlockSpec((MXU_DIM, MXU_DIM), const),
            pl.BlockSpec((B_W, 2 * B_W), const),
        ],
        out_specs=[
            pl.BlockSpec((TM, A_Q_W), row),
            pl.BlockSpec((TM, A_KV_W), row),
            pl.BlockSpec((A_KV_HEADS * V_GROUP, TM), lambda i: (0, i)),
            pl.BlockSpec((TM, 2 * B_W), row),
        ],
        out_shape=[
            jax.ShapeDtypeStruct((ntok, A_Q_W), BF16),
            jax.ShapeDtypeStruct((ntok, A_KV_W), BF16),
            jax.ShapeDtypeStruct((A_KV_HEADS * V_GROUP, ntok), BF16),
            jax.ShapeDtypeStruct((ntok, 2 * B_W), F32),
        ],
        compiler_params=_cparams(("arbitrary",), 40),
        name="in_proj_ab",
    )(x, mod_l, w_in, qn, kn, cos_t, sin_t, ones_bd, dft)


def _rope_tables(seq):
    t = np.arange(seq)
    row = (t // GRID_W).astype(np.float64)
    col = (t % GRID_W).astype(np.float64)
    n_ax = HEAD_DIM // 4
    inv = ROPE_THETA ** (-np.arange(n_ax, dtype=np.float64) / n_ax)
    ang = np.concatenate([row[:, None] * inv, col[:, None] * inv], axis=-1)
    cos, sin = np.cos(ang), np.sin(ang)
    cos_h = np.concatenate([cos, cos], axis=-1)
    sin_h = np.concatenate([-sin, sin], axis=-1)
    cos_t = np.concatenate([np.tile(cos_h, (1, 2)), np.ones((TM, LANES))], axis=0)
    sin_t = np.concatenate([np.tile(sin_h, (1, 2)), np.zeros((TM, LANES))], axis=0)
    return jnp.asarray(cos_t.astype(np.float32)), jnp.asarray(sin_t.astype(np.float32))


def _gqa_kernel(bound_ref, q_ref, kc_ref, vtc_ref, *rest, has_lat):
    if has_lat:
        kl_ref, vtl_ref, o_ref, qm_ref, m_ref, acc_ref = rest
    else:
        o_ref, qm_ref, m_ref, acc_ref = rest
    j = pl.program_id(2)
    group = A_Q_HEADS // A_KV_HEADS
    lane = lax.broadcasted_iota(jnp.int32, (1, LANES), 1)
    hi_half = lane >= HEAD_DIM
    n_chunks = q_ref.shape[0] // ATT_QCHUNK
    no_shift = bound_ref[0] <= SAFE_SCORE_BOUND_LOG2

    def update(k_ref, vt_ref, shifted):
        kbody = min(ATT_KBODY, k_ref.shape[0])
        bodies = [(kb, h, c) for kb in range(k_ref.shape[0] // kbody) for h in range(A_Q_HEADS) for c in range(n_chunks)]

        def scores(kb, h, c):
            kvp = (h // group) // 2
            kp = k_ref[kb * kbody:(kb + 1) * kbody, kvp * LANES:(kvp + 1) * LANES]
            return _dot_nt(kp, qm_ref[h, c * ATT_QCHUNK:(c + 1) * ATT_QCHUNK, :])

        def accumulate(h, c, alpha, pv):
            cs = slice(c * ATT_QCHUNK, (c + 1) * ATT_QCHUNK)
            prev = acc_ref[h, :, cs] if alpha is None else alpha * acc_ref[h, :, cs]
            acc_ref[h, :, cs] = prev + pv

        queue = [scores(*bodies[t]) for t in range(min(ATT_LOOKAHEAD, len(bodies)))]
        pending = None
        for idx, (kb, h, c) in enumerate(bodies):
            s = queue.pop(0)
            if idx + ATT_LOOKAHEAD < len(bodies):
                queue.append(scores(*bodies[idx + ATT_LOOKAHEAD]))
            cs = slice(c * ATT_QCHUNK, (c + 1) * ATT_QCHUNK)
            kvh = h // group
            vt = vt_ref[kvh * V_GROUP:(kvh + 1) * V_GROUP, kb * kbody:(kb + 1) * kbody]
            if shifted:
                m_prev = m_ref[h, :, cs]
                m_new = jnp.maximum(m_prev, jnp.max(s, axis=0, keepdims=True))
                alpha = jnp.exp2(m_prev - m_new)
                p = jnp.exp2(s - m_new).astype(BF16)
                m_ref[h, :, cs] = m_new
            else:
                alpha = None
                p = jnp.exp2(s).astype(BF16)
            pv = _dot(vt, p)
            if pending is not None:
                accumulate(*pending)
            pending = (h, c, alpha, pv)
        accumulate(*pending)

    def update_either(k_ref, vt_ref):
        @pl.when(no_shift)
        def _plain():
            update(k_ref, vt_ref, shifted=False)

        @pl.when(jnp.logical_not(no_shift))
        def _running_max():
            update(k_ref, vt_ref, shifted=True)

    @pl.when(j == 0)
    def _first():
        m_ref[...] = jnp.full(m_ref.shape, NEG_BIG, F32)
        acc_ref[...] = jnp.zeros(acc_ref.shape, F32)
        for h in range(A_Q_HEADS):
            qp = q_ref[:, (h // 2) * LANES:(h // 2 + 1) * LANES].astype(F32)
            kv_half = (h // group) % 2
            if kv_half != h % 2:
                qp = pltpu.roll(qp, HEAD_DIM, 1)
            keep = hi_half if kv_half else jnp.logical_not(hi_half)
            qm_ref[h] = jnp.where(keep, qp, jnp.zeros_like(qp)).astype(BF16)
        update_either(kc_ref, vtc_ref)

    if has_lat:
        @pl.when(j > 0)
        def _rest():
            update_either(kl_ref, vtl_ref)

    @pl.when(j == pl.num_programs(2) - 1)
    def _final():
        for pb in range(A_Q_HEADS // 2):
            for c in range(n_chunks):
                cs = slice(c * ATT_QCHUNK, (c + 1) * ATT_QCHUNK)
                parts = []
                for h in (2 * pb, 2 * pb + 1):
                    a = acc_ref[h, :, cs]
                    parts.append(a[:HEAD_DIM] / a[HEAD_DIM:HEAD_DIM + 1])
                o_ref[cs, pb * LANES:(pb + 1) * LANES] = jnp.concatenate(parts, axis=0).T.astype(BF16)


def _gqa_scratch(tq):
    return [
        pltpu.VMEM((A_Q_HEADS, tq, LANES), BF16),
        pltpu.VMEM((A_Q_HEADS, 1, tq), F32),
        pltpu.VMEM((A_Q_HEADS, V_GROUP, tq), F32),
    ]


def _gqa(q, k, vt, q_norm, k_norm, *, bsz, seq, lc):
    n_lat = bsz * seq
    tq, tk = ATT_TQ, ATT_TK
    ctx_blk = lambda b: n_lat // lc + b
    bound = (1.05 * HEAD_DIM * Q_SCALE_LOG2 * jnp.max(jnp.abs(q_norm)) * jnp.max(jnp.abs(k_norm))).reshape(1)
    smem = pl.BlockSpec(memory_space=pltpu.SMEM)
    o = pl.pallas_call(
        functools.partial(_gqa_kernel, has_lat=True),
        grid=(bsz, seq // tq, 1 + seq // tk),
        in_specs=[
            smem,
            pl.BlockSpec((tq, A_Q_W), lambda b, i, j: (b * (seq // tq) + i, 0)),
            pl.BlockSpec((lc, A_KV_W), lambda b, i, j: (ctx_blk(b), 0)),
            pl.BlockSpec((A_KV_HEADS * V_GROUP, lc), lambda b, i, j: (0, ctx_blk(b))),
            pl.BlockSpec((tk, A_KV_W), lambda b, i, j: (b * (seq // tk) + jnp.maximum(j - 1, 0), 0)),
            pl.BlockSpec((A_KV_HEADS * V_GROUP, tk), lambda b, i, j: (0, b * (seq // tk) + jnp.maximum(j - 1, 0))),
        ],
        out_specs=pl.BlockSpec((tq, A_Q_W), lambda b, i, j: (b * (seq // tq) + i, 0)),
        out_shape=jax.ShapeDtypeStruct((n_lat, A_Q_W), BF16),
        scratch_shapes=_gqa_scratch(tq),
        compiler_params=_cparams(("arbitrary", "arbitrary", "arbitrary"), 48),
        name="gqa_latent",
    )(bound, q, k, vt, k, vt)
    o_ctx = pl.pallas_call(
        functools.partial(_gqa_kernel, has_lat=False),
        grid=(bsz, 1, 1),
        in_specs=[
            smem,
            pl.BlockSpec((lc, A_Q_W), lambda b, i, j: (ctx_blk(b), 0)),
            pl.BlockSpec((lc, A_KV_W), lambda b, i, j: (ctx_blk(b), 0)),
            pl.BlockSpec((A_KV_HEADS * V_GROUP, lc), lambda b, i, j: (0, ctx_blk(b))),
        ],
        out_specs=pl.BlockSpec((lc, A_Q_W), lambda b, i, j: (b, 0)),
        out_shape=jax.ShapeDtypeStruct((bsz * lc, A_Q_W), BF16),
        scratch_shapes=_gqa_scratch(lc),
        compiler_params=_cparams(("arbitrary", "arbitrary", "arbitrary"), 32),
        name="gqa_context",
    )(bound, q, k, vt)
    return o, o_ctx


def _dft_a_kernel(y_ref, ca_ref, tc_ref, ts_ref, d_ref, *, nbb):
    ca = ca_ref[...]
    ra = ca.shape[1]
    for j in range(nbb):
        yj = y_ref[:, j, :].astype(BF16)
        p = _dot(ca, yj)
        br = p[:ra, :B_W] - p[ra:, B_W:]
        bi = -p[:ra, B_W:] - p[ra:, :B_W]
        tc = jnp.concatenate([tc_ref[j]] * (B_W // LANES), axis=1)
        ts = jnp.concatenate([ts_ref[j]] * (B_W // LANES), axis=1)
        d_ref[:, 0, j, :] = br * tc + bi * ts
        d_ref[:, 1, j, :] = bi * tc - br * ts


def _dft_b_kernel(d_ref, cb_ref, o_ref, *, kb, scale):
    cb = cb_ref[...]
    for j in range(kb):
        o_ref[:, j, :] = _dot(cb, d_ref[j].astype(BF16)) * scale


def _dft_ctx_kernel(y_ref, cs_ref, o_ref, *, scale):
    y = y_ref[...]
    cs = cs_ref[...]
    lc = cs.shape[0]
    yc = y[:, :B_W].astype(BF16)
    ys = y[:, B_W:].astype(BF16)
    o_ref[...] = (_dot(cs[:, :lc], yc) - _dot(cs[:, lc:], ys)) * scale


def _fourier(y, *, bsz, seq, lc):
    ntok = y.shape[0]
    ra = seq // GRID_W
    nb = GRID_W
    nbb = SUBLANES
    kb = SUBLANES
    assert ntok % nb == 0 and ra % SUBLANES == 0 and (ntok // nb) % 1 == 0
    ca_c, ca_s = _cos_sin(ra)
    ca = _const_bf16(np.concatenate([ca_c, ca_s], axis=0))
    th = 2.0 * np.pi * np.outer(np.arange(nb), np.arange(ra)) / seq
    tc = jnp.asarray(np.broadcast_to(np.cos(th)[:, :, None], (nb, ra, LANES)).astype(np.float32))
    ts = jnp.asarray(np.broadcast_to(np.sin(th)[:, :, None], (nb, ra, LANES)).astype(np.float32))
    cb_c, cb_s = _cos_sin(nb)
    cb = _const_bf16(np.concatenate([cb_c, cb_s], axis=1))

    y3 = y.reshape(ntok // nb, nb, 2 * B_W)
    d = pl.pallas_call(
        functools.partial(_dft_a_kernel, nbb=nbb),
        grid=(bsz, nb // nbb),
        in_specs=[
            pl.BlockSpec((ra, nbb, 2 * B_W), lambda b, i: (b, i, 0)),
            pl.BlockSpec((2 * ra, ra), lambda b, i: (0, 0)),
            pl.BlockSpec((nbb, ra, LANES), lambda b, i: (i, 0, 0)),
            pl.BlockSpec((nbb, ra, LANES), lambda b, i: (i, 0, 0)),
        ],
        out_specs=pl.BlockSpec((None, ra, 2, nbb, B_W), lambda b, i: (b, 0, 0, i, 0)),
        out_shape=jax.ShapeDtypeStruct((bsz, ra, 2, nb, B_W), F32),
        compiler_params=_cparams(("arbitrary", "arbitrary"), 32),
        name="fourier_rows",
    )(y3, ca, tc, ts)
    d = d.reshape(bsz, ra, 2 * nb, B_W)

    scale = 1.0 / math.sqrt(seq * B_GROUP_DIM)
    n_out = bsz * nb
    z3 = pl.pallas_call(
        functools.partial(_dft_b_kernel, kb=kb, scale=scale),
        grid=(bsz, ra // kb),
        in_specs=[
            pl.BlockSpec((None, kb, 2 * nb, B_W), lambda b, i: (b, i, 0, 0)),
            pl.BlockSpec((nb, 2 * nb), lambda b, i: (0, 0)),
        ],
        out_specs=pl.BlockSpec((nb, kb, B_W), lambda b, i: (b, i, 0)),
        out_shape=jax.ShapeDtypeStruct((n_out, ra, B_W), F32),
        compiler_params=_cparams(("arbitrary", "arbitrary"), 32),
        name="fourier_cols",
    )(d, cb)
    z = z3.reshape(bsz * seq, B_W)

    cc, cs_ = _cos_sin(lc)
    csm = _const_bf16(np.concatenate([cc, cs_], axis=1))
    ctx_blk = lambda b: (bsz * seq) // lc + b
    z_ctx = pl.pallas_call(
        functools.partial(_dft_ctx_kernel, scale=1.0 / math.sqrt(lc * B_GROUP_DIM)),
        grid=(bsz,),
        in_specs=[
            pl.BlockSpec((lc, 2 * B_W), lambda b: (ctx_blk(b), 0)),
            pl.BlockSpec((lc, 2 * lc), lambda b: (0, 0)),
        ],
        out_specs=pl.BlockSpec((lc, B_W), lambda b: (b, 0)),
        out_shape=jax.ShapeDtypeStruct((bsz * lc, B_W), F32),
        compiler_params=_cparams(("arbitrary",), 32),
        name="fourier_context",
    )(y, csm)
    return z, z_ctx


def _values_with_ones_t(v, n_heads):
    vt = v.T
    ones = jnp.ones((ONES_ROWS, vt.shape[1]), F32)
    groups = []
    for h in range(n_heads):
        groups += [vt[h * HEAD_DIM:(h + 1) * HEAD_DIM], ones]
    return jnp.concatenate(groups, axis=0)


def _inproj_cd_kernel(x_ref, mod_ref, w_ref, u_ref, q_ref, k_ref, vt_ref):
    x = x_ref[...]
    m = mod_ref[...]
    h = _modulate(x, m[3:4], m[4:5]).astype(BF16)
    y = _dot(h, w_ref[...].astype(BF16))
    c = C_CHANNELS
    u_ref[...] = y[:, :c] * _sigmoid(y[:, c:2 * c])
    q_ref[...] = (y[:, 2 * c:2 * c + D_W] * Q_SCALE_LOG2).astype(BF16)
    k_ref[...] = y[:, 2 * c + D_W:2 * c + 2 * D_W].astype(BF16)
    vt_ref[...] = _values_with_ones_t(y[:, 2 * c + 2 * D_W:], D_HEADS).astype(BF16)


def _inproj_cd(x, mod_l, w_in, w_idx, *, n_tiles, mod_row):
    ntok, d = x.shape
    n_in = w_in.shape[-1]
    row = lambda i: (i, 0)
    return pl.pallas_call(
        _inproj_cd_kernel,
        grid=(n_tiles,),
        in_specs=[
            pl.BlockSpec((TM, d), row),
            pl.BlockSpec((None, N_MOD, d), lambda i: (mod_row(i), 0, 0)),
            pl.BlockSpec((None, d, n_in), lambda i: (w_idx, 0, 0)),
        ],
        out_specs=[
            pl.BlockSpec((TM, C_CHANNELS), row),
            pl.BlockSpec((TM, D_W), row),
            pl.BlockSpec((TM, D_W), row),
            pl.BlockSpec((D_HEADS * V_GROUP, TM), lambda i: (0, i)),
        ],
        out_shape=[
            jax.ShapeDtypeStruct((ntok, C_CHANNELS), F32),
            jax.ShapeDtypeStruct((ntok, D_W), BF16),
            jax.ShapeDtypeStruct((ntok, D_W), BF16),
            jax.ShapeDtypeStruct((D_HEADS * V_GROUP, ntok), BF16),
        ],
        compiler_params=_cparams(("arbitrary",), 40),
        name="in_proj_cd",
    )(x, mod_l, w_in)


def _conv_kernel(prev_ref, cur_ref, next_ref, w_ref, b_ref, lnw_ref, lnb_ref, o_ref, buf_ref, shift_ref, *,
                 tiles_per_seq):
    i = pl.program_id(1)
    halo = CONV_HALO
    tm = cur_ref.shape[0]
    zeros = jnp.zeros((halo, cur_ref.shape[1]), F32)
    buf_ref[0:halo, :] = jnp.where(i > 0, prev_ref[...], zeros)
    buf_ref[halo:halo + tm, :] = cur_ref[...]
    buf_ref[halo + tm:, :] = jnp.where(i < tiles_per_seq - 1, next_ref[...], zeros)
    w = w_ref[...]
    off = halo - C_KERNEL // 2
    y = None
    for r in range(SUBLANES):
        taps = [t for t in range(C_KERNEL) if (t + off) % SUBLANES == r]
        part = None
        for t in taps:
            start = t + off - r
            term = buf_ref[start:start + tm + SUBLANES, :] * w[t:t + 1]
            part = term if part is None else part + term
        if r == 0:
            shifted = part[:tm]
        else:
            shift_ref[...] = part
            shifted = shift_ref[r:r + tm, :]
        y = shifted if y is None else y + shifted
    y = y + b_ref[...]
    mu = jnp.mean(y, axis=-1, keepdims=True)
    yc = y - mu
    var = jnp.mean(yc * yc, axis=-1, keepdims=True)
    z = (yc * lax.rsqrt(var + NORM_EPS)) * lnw_ref[...] + lnb_ref[...]
    o_ref[...] = (z * _sigmoid(z)).astype(BF16)


def _conv_module(u, dw_w, dw_b, ln_w, ln_b, *, bsz, seq):
    c = u.shape[1]
    tiles = seq // TM
    hpt = TM // CONV_HALO
    n_halo_blocks = u.shape[0] // CONV_HALO
    cur = lambda b, i: (b * tiles + i, 0)
    prev = lambda b, i: (jnp.maximum((b * tiles + i) * hpt - 1, 0), 0)
    nxt = lambda b, i: (jnp.minimum((b * tiles + i + 1) * hpt, n_halo_blocks - 1), 0)
    vec = lambda b, i: (0, 0)
    return pl.pallas_call(
        functools.partial(_conv_kernel, tiles_per_seq=tiles),
        grid=(bsz, tiles),
        in_specs=[
            pl.BlockSpec((CONV_HALO, c), prev),
            pl.BlockSpec((TM, c), cur),
            pl.BlockSpec((CONV_HALO, c), nxt),
            pl.BlockSpec((C_KERNEL, c), vec),
            pl.BlockSpec((1, c), vec),
            pl.BlockSpec((1, c), vec),
            pl.BlockSpec((1, c), vec),
        ],
        out_specs=pl.BlockSpec((TM, c), cur),
        out_shape=jax.ShapeDtypeStruct((bsz * seq, c), BF16),
        scratch_shapes=[pltpu.VMEM((TM + 2 * CONV_HALO, c), F32), pltpu.VMEM((TM + SUBLANES, c), F32)],
        compiler_params=_cparams(("arbitrary", "arbitrary"), 32),
        name="conv_module",
    )(u, u, u, dw_w, dw_b.reshape(1, c), ln_w.reshape(1, c), ln_b.reshape(1, c))


def _na_kernel(q_ref, k0_ref, k1_ref, k2_ref, vt0_ref, vt1_ref, vt2_ref, kc_ref, vtc_ref, bias_ref, o_ref):
    lane = lax.broadcasted_iota(jnp.int32, (1, LANES), 1)
    hi_half = lane >= HEAD_DIM
    tq = q_ref.shape[0]
    k_refs = (k0_ref, k1_ref, k2_ref, kc_ref)
    vt_refs = (vt0_ref, vt1_ref, vt2_ref, vtc_ref)
    n_near = len(k_refs) - 1
    bodies = [(h, t) for h in range(D_HEADS) for t in range(len(k_refs))]
    masked_q = {}

    def scores(h, t):
        sl = slice((h // 2) * LANES, (h // 2 + 1) * LANES)
        if h not in masked_q:
            qp = q_ref[:, sl]
            keep = hi_half if h % 2 else jnp.logical_not(hi_half)
            masked_q[h] = jnp.where(keep, qp, jnp.zeros_like(qp))
        s = _dot_nt(k_refs[t][:, sl], masked_q[h])
        return s + bias_ref[h, t * tq:(t + 1) * tq, :] if t < n_near else s

    queue = [scores(*bodies[i]) for i in range(NA_LOOKAHEAD)]
    acc, m_run, pending = {}, {}, None

    def fold(h, alpha, pv):
        acc[h] = pv if alpha is None else alpha * acc[h] + pv

    for idx, (h, t) in enumerate(bodies):
        s = queue.pop(0)
        if idx + NA_LOOKAHEAD < len(bodies):
            queue.append(scores(*bodies[idx + NA_LOOKAHEAD]))
        m_tile = jnp.max(s, axis=0, keepdims=True)
        if t == 0:
            m_new, alpha = m_tile, None
        else:
            m_new = jnp.maximum(m_run[h], m_tile)
            alpha = jnp.exp2(m_run[h] - m_new)
        m_run[h] = m_new
        p = jnp.exp2(s - m_new).astype(BF16)
        pv = _dot(vt_refs[t][h * V_GROUP:(h + 1) * V_GROUP, :], p)
        if pending is not None:
            fold(*pending)
        pending = (h, alpha, pv)
    fold(*pending)
    for pb in range(D_HEADS // 2):
        parts = [acc[h][:HEAD_DIM] / acc[h][HEAD_DIM:HEAD_DIM + 1] for h in (2 * pb, 2 * pb + 1)]
        o_ref[:, pb * LANES:(pb + 1) * LANES] = jnp.concatenate(parts, axis=0).T.astype(BF16)


def _na_bias_kernel(rpb_ref, o_ref, *, rows):
    h = pl.program_id(0)
    n_ro, n_co = 2 * NA_WIN_ROWS - 1, 2 * NA_WIN_COLS - 1
    kc = lax.broadcasted_iota(jnp.int32, (GRID_W, LANES), 0)
    lane = lax.broadcasted_iota(jnp.int32, (GRID_W, LANES), 1)
    qc = lane % GRID_W
    d = kc - qc + (NA_WIN_COLS - 1)
    cs = jnp.clip(qc - NA_WIN_COLS // 2, 0, GRID_W - NA_WIN_COLS)
    col_ok = (kc >= cs) & (kc < cs + NA_WIN_COLS)
    neg = jnp.full((GRID_W, LANES), NEG_BIG, F32)
    log2e = math.log2(math.e)
    band = []
    for ro in range(n_ro):
        acc = neg
        for co in range(n_co):
            acc = jnp.where(d == co, rpb_ref[(h * n_ro + ro) * n_co + co] * log2e, acc)
        band.append(jnp.where(col_ok, acc, neg))
    qr = NA_QROWS
    for kind, r0 in enumerate((0, qr, rows - qr)):
        for b in range(3 * qr):
            rk = r0 - qr + b
            for pair in range(qr // 2):
                halves = []
                for a in (2 * pair, 2 * pair + 1):
                    r = r0 + a
                    start = min(max(r - NA_WIN_ROWS // 2, 0), rows - NA_WIN_ROWS)
                    ok = start <= rk < start + NA_WIN_ROWS and 0 <= rk < rows
                    halves.append(band[rk - r + NA_WIN_ROWS - 1] if ok else neg)
                o_ref[kind, b * GRID_W:(b + 1) * GRID_W, pair * LANES:(pair + 1) * LANES] = jnp.where(
                    lane < GRID_W, halves[0], halves[1])


def _na_bias(rpb, rows):
    heads = rpb.shape[0]
    tq = NA_QROWS * GRID_W
    return pl.pallas_call(
        functools.partial(_na_bias_kernel, rows=rows),
        grid=(heads,),
        in_specs=[pl.BlockSpec(memory_space=pltpu.SMEM)],
        out_specs=pl.BlockSpec((3, None, 3 * tq, tq), lambda h: (0, h, 0, 0)),
        out_shape=jax.ShapeDtypeStruct((3, heads, 3 * tq, tq), F32),
        compiler_params=_cparams(("arbitrary",), 32),
        name="na_bias",
    )(rpb.reshape(-1))


def _na(q, k, vt, rpb, *, bsz, seq, lc):
    rows = seq // GRID_W
    tq = NA_QROWS * GRID_W
    nblk = seq // tq
    assert rows >= NA_WIN_ROWS and rows % NA_QROWS == 0 and NA_WIN_ROWS == 2 * NA_QROWS and nblk >= 3
    assert lc == tq
    bias = _na_bias(rpb, rows)
    n_lat = bsz * seq
    qmap = lambda b, i: (b * nblk + i, 0)

    def near(off, transposed):
        def index(b, i):
            blk = b * nblk + jnp.clip(i + off, 0, nblk - 1)
            return (0, blk) if transposed else (blk, 0)
        return index

    kind = lambda b, i: (jnp.where(i == 0, 0, jnp.where(i == nblk - 1, 2, 1)), 0, 0, 0)
    rows_blk = lambda m: pl.BlockSpec((tq, D_W), m)
    cols_blk = lambda m: pl.BlockSpec((D_HEADS * V_GROUP, tq), m)
    return pl.pallas_call(
        _na_kernel,
        grid=(bsz, nblk),
        in_specs=[
            rows_blk(qmap),
            rows_blk(near(-1, False)), rows_blk(near(0, False)), rows_blk(near(1, False)),
            cols_blk(near(-1, True)), cols_blk(near(0, True)), cols_blk(near(1, True)),
            rows_blk(lambda b, i: (n_lat // lc + b, 0)),
            cols_blk(lambda b, i: (0, n_lat // lc + b)),
            pl.BlockSpec((None, D_HEADS, 3 * tq, tq), kind),
        ],
        out_specs=rows_blk(qmap),
        out_shape=jax.ShapeDtypeStruct((n_lat, D_W), BF16),
        compiler_params=_cparams(("arbitrary", "arbitrary"), 48),
        name="neighbourhood_attention",
    )(q, k, k, k, vt, vt, vt, k, vt, bias)


def kernel(x, c, ctx, c_ctx, w_mod, b_mod, ffn_w_gate, ffn_w_up, ffn_w_down, ab_w_in, ab_w_out, ab_q_norm,
           ab_k_norm, cd_w_in, cd_w_out, cd_dw_w, cd_dw_b, cd_ln_w, cd_ln_b, cd_rpb, final_norm):
    bsz, seq, d = x.shape
    lc = ctx.shape[1]
    depth = w_mod.shape[0]
    assert depth == 2, "layer 0 (A/B mixer) updates the context, layer 1 (C/D mixer) is the last layer"
    assert bsz * lc == TM and seq % TM == 0 and seq % ATT_TQ == 0 and seq % ATT_TK == 0
    n_lat_tiles = bsz * seq // TM
    n_all_tiles = n_lat_tiles + 1
    tiles_per_seq = seq // TM
    mod_row = lambda i: jnp.where(i == n_lat_tiles, bsz, i // tiles_per_seq)
    rope_row = lambda i: jnp.where(i == n_lat_tiles, tiles_per_seq, i % tiles_per_seq)

    mod = _mod_params(c, c_ctx, w_mod, b_mod)
    cos_t, sin_t = _rope_tables(seq)

    ffn_w = (ffn_w_gate, ffn_w_up, ffn_w_down)

    xt = x.reshape(bsz * seq, d)
    for layer in range(depth):
        last = layer == depth - 1
        i = layer // 2
        mod_l = mod[layer]
        xt = _ffn(xt, mod_l, ffn_w, layer, 0, base=0, n_tiles=n_all_tiles, mod_row=mod_row,
                  x_ctx=ctx.reshape(bsz * lc, d) if layer == 0 else None)
        if layer % 2 == 0:
            q, k, vt, y = _inproj_ab(xt, mod_l, ab_w_in, i, ab_q_norm[i], ab_k_norm[i], cos_t, sin_t,
                                     n_tiles=n_all_tiles, mod_row=mod_row, rope_row=rope_row)
            a1, a1_ctx = _gqa(q, k, vt, ab_q_norm[i], ab_k_norm[i], bsz=bsz, seq=seq, lc=lc)
            a2, a2_ctx = _fourier(y, bsz=bsz, seq=seq, lc=lc)
            mixer = (a1, a2, ab_w_out, i, a1_ctx, a2_ctx)
        else:
            u, q, k, vt = _inproj_cd(xt, mod_l, cd_w_in, i, n_tiles=n_all_tiles, mod_row=mod_row)
            a1 = _conv_module(u, cd_dw_w[i], cd_dw_b[i], cd_ln_w[i], cd_ln_b[i], bsz=bsz, seq=seq)
            a2 = _na(q, k, vt, cd_rpb[i], bsz=bsz, seq=seq, lc=lc)
            mixer = (a1, a2, cd_w_out, i, None, None)
        assert last == (mixer[4] is None)
        xt = _ffn(xt, mod_l, ffn_w, layer, 1, base=6, n_tiles=n_lat_tiles if last else n_all_tiles, mod_row=mod_row,
                  final_norm=final_norm if last else None, mixer=mixer)
    return xt.reshape(bsz, seq, d)
```

```python
import functools
import math

import numpy as np
import jax
import jax.numpy as jnp
from jax import lax
from jax.experimental import pallas as pl
from jax.experimental.pallas import tpu as pltpu

F32 = jnp.float32
BF16 = jnp.bfloat16

GRID_W = 64
HEAD_DIM = 64
A_Q_HEADS = 12
A_KV_HEADS = 4
B_GROUPS = 4
B_GROUP_DIM = 64
C_CHANNELS = 512
C_KERNEL = 31
D_HEADS = 8
NA_WIN_ROWS = 8
NA_WIN_COLS = 16
ROPE_THETA = 10000.0
NORM_EPS = 1e-6
N_MOD = 9
FFN_RES_WEIGHT = 0.5

A_Q_W = A_Q_HEADS * HEAD_DIM
A_KV_W = A_KV_HEADS * HEAD_DIM
B_W = B_GROUPS * B_GROUP_DIM
D_W = D_HEADS * HEAD_DIM

LANES = 128
SUBLANES = 8
MXU_DIM = 256

TM = 512
FFN_CHUNK = 256
ATT_TQ = 1024
ATT_TK = 1024
ATT_KBODY = 256
NA_QROWS = 4
NA_LOOKAHEAD = 4
NA_QBLOCKS = 4
CONV_HALO = 16
ATT_QCHUNK = 256
ATT_LOOKAHEAD = 4
SAFE_SCORE_BOUND_LOG2 = 100.0
ONES_ROWS = 16
V_GROUP = HEAD_DIM + ONES_ROWS
NEG_BIG = -1e30
Q_SCALE_LOG2 = (HEAD_DIM ** -0.5) * math.log2(math.e)

_MiB = 1 << 20


def _cparams(sem, vmem_mib):
    return pltpu.CompilerParams(dimension_semantics=sem, vmem_limit_bytes=vmem_mib * _MiB)


def _dot(a, b):
    return jnp.dot(a, b, preferred_element_type=F32)


def _dot_nt(a, b):
    return lax.dot_general(a, b, (((1,), (1,)), ((), ())), preferred_element_type=F32)


def _sigmoid(x):
    return 1.0 / (1.0 + jnp.exp(-x))


def _cos_sin(n):
    idx = np.arange(n)
    ang = 2.0 * np.pi * (np.outer(idx, idx) % n) / n
    return np.cos(ang), np.sin(ang)


def _const_bf16(a):
    return jnp.asarray(np.asarray(a, np.float32)).astype(BF16)


def _modulate(x, shift, scale):
    ms = jnp.mean(x * x, axis=-1, keepdims=True)
    return (x * lax.rsqrt(ms + NORM_EPS)) * (1.0 + scale) + shift


def _mod_kernel(ct_ref, w_ref, b_ref, o_ref, *, n_rows):
    ct = ct_ref[...]
    a = ct * _sigmoid(ct)
    w = w_ref[...]
    rows = [jnp.sum(w * a[:, r:r + 1], axis=0, keepdims=True) for r in range(n_rows)]
    rows.append(jnp.zeros((SUBLANES - n_rows, w.shape[1]), F32))
    o_ref[...] = jnp.concatenate(rows, axis=0) + b_ref[...]


def _mod_params(c, c_ctx, w_mod, b_mod):
    depth, d, nmd = w_mod.shape
    bsz = c.shape[0]
    n_rows = bsz + 1
    assert n_rows <= SUBLANES
    ct = jnp.concatenate([c, c_ctx[None], jnp.zeros((SUBLANES - n_rows, d), F32)], axis=0).T
    tn = d
    out = pl.pallas_call(
        functools.partial(_mod_kernel, n_rows=n_rows),
        grid=(depth, nmd // tn),
        in_specs=[
            pl.BlockSpec((d, SUBLANES), lambda l, n: (0, 0)),
            pl.BlockSpec((None, d, tn), lambda l, n: (l, 0, n)),
            pl.BlockSpec((None, 1, tn), lambda l, n: (l, 0, n)),
        ],
        out_specs=pl.BlockSpec((None, SUBLANES, tn), lambda l, n: (l, 0, n)),
        out_shape=jax.ShapeDtypeStruct((depth, SUBLANES, nmd), F32),
        compiler_params=_cparams(("arbitrary", "arbitrary"), 32),
        name="adaln_params",
    )(ct, w_mod, b_mod.reshape(depth, 1, nmd))
    return out.reshape(depth, SUBLANES, N_MOD, d)


def _pick_rows(lat_ref, ctx_ref, n_lat_tiles):
    return jnp.where(pl.program_id(0) == n_lat_tiles, ctx_ref[...], lat_ref[...])


def _ffn_kernel(*refs, base, final, n_chunks, ctx_tile, x_has_ctx, mixer):
    refs = list(refs)
    x_ref = refs.pop(0)
    x = x_ref[...] if not x_has_ctx else _pick_rows(x_ref, refs.pop(0), ctx_tile)
    if mixer:
        a1_ref, a2_ref = refs.pop(0), refs.pop(0)
        if ctx_tile is None:
            a1, a2 = a1_ref[...], a2_ref[...]
        else:
            a1 = _pick_rows(a1_ref, refs.pop(0), ctx_tile)
            a2 = _pick_rows(a2_ref, refs.pop(0), ctx_tile)
        w1_ref, w2_ref = refs.pop(0), refs.pop(0)
    mod_ref, wg_ref, wu_ref, wd_ref = refs[:4]
    if final:
        fn_ref, o_ref, acc_ref = refs[4:]
    else:
        o_ref, acc_ref = refs[4:]
    m = mod_ref[...]
    if mixer:
        x = x + m[5:6] * (_dot(a1.astype(BF16), w1_ref[...].astype(BF16)) +
                          _dot(a2.astype(BF16), w2_ref[...].astype(BF16)))
    h = _modulate(x, m[base:base + 1], m[base + 1:base + 2]).astype(BF16)
    for c in range(n_chunks):
        sl = slice(c * FFN_CHUNK, (c + 1) * FFN_CHUNK)
        g = _dot(h, wg_ref[:, sl].astype(BF16))
        u = _dot(h, wu_ref[:, sl].astype(BF16))
        a = ((g * _sigmoid(g)) * u).astype(BF16)
        part = _dot(a, wd_ref[sl, :].astype(BF16))
        if c == 0:
            acc_ref[...] = part
        else:
            acc_ref[...] += part
    y = x + (FFN_RES_WEIGHT * m[base + 2:base + 3]) * acc_ref[...]
    if final:
        ms = jnp.mean(y * y, axis=-1, keepdims=True)
        y = (y * lax.rsqrt(ms + NORM_EPS)) * fn_ref[...]
    o_ref[...] = y


def _ffn(x, mod_l, ffn_w, layer, half, *, base, n_tiles, mod_row, final_norm=None, x_ctx=None, mixer=None):
    d = x.shape[1]
    wg, wu, wd = ffn_w
    dff = wg.shape[-1]
    assert dff % FFN_CHUNK == 0
    final = final_norm is not None
    resident = dict(pipeline_mode=pl.Buffered(1))
    const = lambda i: (0, 0)
    last_lat = n_tiles - 2
    lat_only = lambda i: (jnp.minimum(i, last_lat), 0)
    ctx_tile = None
    in_specs = [pl.BlockSpec((TM, d), lambda i: (i, 0))]
    args = [x]
    if x_ctx is not None:
        ctx_tile = n_tiles - 1
        assert x.shape[0] == ctx_tile * TM and x_ctx.shape == (TM, d)
        in_specs = [pl.BlockSpec((TM, d), lat_only), pl.BlockSpec((TM, d), const)]
        args.append(x_ctx)
    if mixer is not None:
        a1, a2, w_out, w_idx, a1_ctx, a2_ctx = mixer
        d1, d2 = a1.shape[1], a2.shape[1]
        assert d1 % d2 == 0
        if a1_ctx is None:
            in_specs += [pl.BlockSpec((TM, d1), lambda i: (i, 0)), pl.BlockSpec((TM, d2), lambda i: (i, 0))]
            args += [a1, a2]
        else:
            ctx_tile = n_tiles - 1
            assert a1.shape[0] == ctx_tile * TM and a1_ctx.shape[0] == TM and a2_ctx.shape[0] == TM
            in_specs += [pl.BlockSpec((TM, d1), lat_only), pl.BlockSpec((TM, d2), lat_only),
                         pl.BlockSpec((TM, d1), const), pl.BlockSpec((TM, d2), const)]
            args += [a1, a2, a1_ctx, a2_ctx]
        in_specs += [pl.BlockSpec((None, d1, d), lambda i: (w_idx, 0, 0), **resident),
                     pl.BlockSpec((None, d2, d), lambda i: (w_idx, d1 // d2, 0), **resident)]
        args += [w_out, w_out]
    sel = lambda i: (layer, half, 0, 0)
    in_specs += [
        pl.BlockSpec((None, N_MOD, d), lambda i: (mod_row(i), 0, 0)),
        pl.BlockSpec((None, None, d, dff), sel, **resident),
        pl.BlockSpec((None, None, d, dff), sel, **resident),
        pl.BlockSpec((None, None, dff, d), sel, **resident),
    ]
    args += [mod_l, wg, wu, wd]
    if final:
        in_specs.append(pl.BlockSpec((1, d), const))
        args.append(final_norm.reshape(1, d))
    return pl.pallas_call(
        functools.partial(_ffn_kernel, base=base, final=final, n_chunks=dff // FFN_CHUNK, ctx_tile=ctx_tile,
                          x_has_ctx=x_ctx is not None, mixer=mixer is not None),
        grid=(n_tiles,),
        in_specs=in_specs,
        out_specs=pl.BlockSpec((TM, d), lambda i: (i, 0)),
        out_shape=jax.ShapeDtypeStruct((n_tiles * TM, d), F32),
        scratch_shapes=[pltpu.VMEM((TM, d), F32)],
        compiler_params=_cparams(("arbitrary",), 60),
        name=("mix_ffn_final" if final else "mix_ffn") if mixer is not None else "ffn",
    )(*args)


def _swap_halves(z, low_half):
    n = z.shape[1]
    return jnp.where(low_half, pltpu.roll(z, n - HEAD_DIM // 2, 1), pltpu.roll(z, HEAD_DIM // 2, 1))


def _head_sumsq(z, ones_bd):
    zz = z * z
    hi = zz.astype(BF16)
    lo = (zz - hi.astype(F32)).astype(BF16)
    return _dot(hi, ones_bd) + _dot(lo, ones_bd)


def _inproj_ab_kernel(x_ref, mod_ref, w_ref, qn_ref, kn_ref, cos_ref, sin_ref, ones_ref, dft_ref,
                      q_ref, k_ref, vt_ref, y_ref):
    x = x_ref[...]
    m = mod_ref[...]
    h = _modulate(x, m[3:4], m[4:5]).astype(BF16)
    y = _dot(h, w_ref[...].astype(BF16))
    ones_bd = ones_ref[...]
    cos = jnp.concatenate([cos_ref[...]] * 2, axis=1)
    sin = jnp.concatenate([sin_ref[...]] * 2, axis=1)
    lane = lax.broadcasted_iota(jnp.int32, (1, MXU_DIM), 1)
    low_half = (lane % HEAD_DIM) < (HEAD_DIM // 2)

    def norm_rope(z, nw):
        ss = _head_sumsq(z, ones_bd)
        zn = (z * lax.rsqrt(ss * (1.0 / HEAD_DIM) + NORM_EPS)) * nw
        return zn * cos + _swap_halves(zn, low_half) * sin

    qn = qn_ref[...]
    kn = kn_ref[...]
    for blk in range(A_Q_W // MXU_DIM):
        sl = slice(blk * MXU_DIM, (blk + 1) * MXU_DIM)
        q_ref[:, sl] = (norm_rope(y[:, sl], qn) * Q_SCALE_LOG2).astype(BF16)
    k_ref[...] = norm_rope(y[:, A_Q_W:A_Q_W + A_KV_W], kn).astype(BF16)
    vt_ref[...] = _values_with_ones_t(y[:, A_Q_W + A_KV_W:A_Q_W + 2 * A_KV_W], A_KV_HEADS).astype(BF16)
    f = y[:, A_Q_W + 2 * A_KV_W:].astype(BF16)
    y_ref[...] = _dot(f, dft_ref[...])


def _inproj_ab(x, mod_l, w_in, w_idx, q_norm, k_norm, cos_t, sin_t, *, n_tiles, mod_row, rope_row):
    ntok, d = x.shape
    n_in = w_in.shape[-1]
    ones_bd = _const_bf16(np.kron(np.eye(MXU_DIM // HEAD_DIM), np.ones((HEAD_DIM, HEAD_DIM))))
    gc, gs = _cos_sin(B_GROUP_DIM)
    eye = np.eye(B_GROUPS)
    dft = _const_bf16(np.concatenate([np.kron(eye, gc), np.kron(eye, gs)], axis=1))
    qn = jnp.tile(q_norm, MXU_DIM // HEAD_DIM).reshape(1, MXU_DIM)
    kn = jnp.tile(k_norm, MXU_DIM // HEAD_DIM).reshape(1, MXU_DIM)
    const = lambda i: (0, 0)
    row = lambda i: (i, 0)
    return pl.pallas_call(
        _inproj_ab_kernel,
        grid=(n_tiles,),
        in_specs=[
            pl.BlockSpec((TM, d), row),
            pl.BlockSpec((None, N_MOD, d), lambda i: (mod_row(i), 0, 0)),
            pl.BlockSpec((None, d, n_in), lambda i: (w_idx, 0, 0)),
            pl.BlockSpec((1, MXU_DIM), const),
            pl.BlockSpec((1, MXU_DIM), const),
            pl.BlockSpec((TM, LANES), lambda i: (rope_row(i), 0)),
            pl.BlockSpec((TM, LANES), lambda i: (rope_row(i), 0)),
            pl.BlockSpec((MXU_DIM, MXU_DIM), const),
            pl.BlockSpec((B_W, 2 * B_W), const),
        ],
        out_specs=[
            pl.BlockSpec((TM, A_Q_W), row),
            pl.BlockSpec((TM, A_KV_W), row),
            pl.BlockSpec((A_KV_HEADS * V_GROUP, TM), lambda i: (0, i)),
            pl.BlockSpec((TM, 2 * B_W), row),
        ],
        out_shape=[
            jax.ShapeDtypeStruct((ntok, A_Q_W), BF16),
            jax.ShapeDtypeStruct((ntok, A_KV_W), BF16),
            jax.ShapeDtypeStruct((A_KV_HEADS * V_GROUP, ntok), BF16),
            jax.ShapeDtypeStruct((ntok, 2 * B_W), F32),
        ],
        compiler_params=_cparams(("arbitrary",), 40),
        name="in_proj_ab",
    )(x, mod_l, w_in, qn, kn, cos_t, sin_t, ones_bd, dft)


def _rope_tables(seq):
    t = np.arange(seq)
    row = (t // GRID_W).astype(np.float64)
    col = (t % GRID_W).astype(np.float64)
    n_ax = HEAD_DIM // 4
    inv = ROPE_THETA ** (-np.arange(n_ax, dtype=np.float64) / n_ax)
    ang = np.concatenate([row[:, None] * inv, col[:, None] * inv], axis=-1)
    cos, sin = np.cos(ang), np.sin(ang)
    cos_h = np.concatenate([cos, cos], axis=-1)
    sin_h = np.concatenate([-sin, sin], axis=-1)
    cos_t = np.concatenate([np.tile(cos_h, (1, 2)), np.ones((TM, LANES))], axis=0)
    sin_t = np.concatenate([np.tile(sin_h, (1, 2)), np.zeros((TM, LANES))], axis=0)
    return jnp.asarray(cos_t.astype(np.float32)), jnp.asarray(sin_t.astype(np.float32))


def _gqa_kernel(bound_ref, q_ref, kc_ref, vtc_ref, *rest, has_lat):
    if has_lat:
        kl_ref, vtl_ref, o_ref, qm_ref, m_ref, acc_ref = rest
    else:
        o_ref, qm_ref, m_ref, acc_ref = rest
    j = pl.program_id(2)
    group = A_Q_HEADS // A_KV_HEADS
    lane = lax.broadcasted_iota(jnp.int32, (1, LANES), 1)
    hi_half = lane >= HEAD_DIM
    n_chunks = q_ref.shape[0] // ATT_QCHUNK
    no_shift = bound_ref[0] <= SAFE_SCORE_BOUND_LOG2

    def update(k_ref, vt_ref, shifted):
        kbody = min(ATT_KBODY, k_ref.shape[0])
        bodies = [(kb, h, c) for kb in range(k_ref.shape[0] // kbody) for h in range(A_Q_HEADS) for c in range(n_chunks)]

        def scores(kb, h, c):
            kvp = (h // group) // 2
            kp = k_ref[kb * kbody:(kb + 1) * kbody, kvp * LANES:(kvp + 1) * LANES]
            return _dot_nt(kp, qm_ref[h, c * ATT_QCHUNK:(c + 1) * ATT_QCHUNK, :])

        def accumulate(h, c, alpha, pv):
            cs = slice(c * ATT_QCHUNK, (c + 1) * ATT_QCHUNK)
            prev = acc_ref[h, :, cs] if alpha is None else alpha * acc_ref[h, :, cs]
            acc_ref[h, :, cs] = prev + pv

        queue = [scores(*bodies[t]) for t in range(min(ATT_LOOKAHEAD, len(bodies)))]
        pending = None
        for idx, (kb, h, c) in enumerate(bodies):
            s = queue.pop(0)
            if idx + ATT_LOOKAHEAD < len(bodies):
                queue.append(scores(*bodies[idx + ATT_LOOKAHEAD]))
            cs = slice(c * ATT_QCHUNK, (c + 1) * ATT_QCHUNK)
            kvh = h // group
            vt = vt_ref[kvh * V_GROUP:(kvh + 1) * V_GROUP, kb * kbody:(kb + 1) * kbody]
            if shifted:
                m_prev = m_ref[h, :, cs]
                m_new = jnp.maximum(m_prev, jnp.max(s, axis=0, keepdims=True))
                alpha = jnp.exp2(m_prev - m_new)
                p = jnp.exp2(s - m_new).astype(BF16)
                m_ref[h, :, cs] = m_new
            else:
                alpha = None
                p = jnp.exp2(s).astype(BF16)
            pv = _dot(vt, p)
            if pending is not None:
                accumulate(*pending)
            pending = (h, c, alpha, pv)
        accumulate(*pending)

    def update_either(k_ref, vt_ref):
        @pl.when(no_shift)
        def _plain():
            update(k_ref, vt_ref, shifted=False)

        @pl.when(jnp.logical_not(no_shift))
        def _running_max():
            update(k_ref, vt_ref, shifted=True)

    @pl.when(j == 0)
    def _first():
        m_ref[...] = jnp.full(m_ref.shape, NEG_BIG, F32)
        acc_ref[...] = jnp.zeros(acc_ref.shape, F32)
        for h in range(A_Q_HEADS):
            qp = q_ref[:, (h // 2) * LANES:(h // 2 + 1) * LANES].astype(F32)
            kv_half = (h // group) % 2
            if kv_half != h % 2:
                qp = pltpu.roll(qp, HEAD_DIM, 1)
            keep = hi_half if kv_half else jnp.logical_not(hi_half)
            qm_ref[h] = jnp.where(keep, qp, jnp.zeros_like(qp)).astype(BF16)
        update_either(kc_ref, vtc_ref)

    if has_lat:
        @pl.when(j > 0)
        def _rest():
            update_either(kl_ref, vtl_ref)

    @pl.when(j == pl.num_programs(2) - 1)
    def _final():
        for pb in range(A_Q_HEADS // 2):
            for c in range(n_chunks):
                cs = slice(c * ATT_QCHUNK, (c + 1) * ATT_QCHUNK)
                parts = []
                for h in (2 * pb, 2 * pb + 1):
                    a = acc_ref[h, :, cs]
                    parts.append(a[:HEAD_DIM] / a[HEAD_DIM:HEAD_DIM + 1])
                o_ref[cs, pb * LANES:(pb + 1) * LANES] = jnp.concatenate(parts, axis=0).T.astype(BF16)


def _gqa_scratch(tq):
    return [
        pltpu.VMEM((A_Q_HEADS, tq, LANES), BF16),
        pltpu.VMEM((A_Q_HEADS, 1, tq), F32),
        pltpu.VMEM((A_Q_HEADS, V_GROUP, tq), F32),
    ]


def _gqa(q, k, vt, q_norm, k_norm, *, bsz, seq, lc):
    n_lat = bsz * seq
    tq, tk = ATT_TQ, ATT_TK
    ctx_blk = lambda b: n_lat // lc + b
    bound = (1.05 * HEAD_DIM * Q_SCALE_LOG2 * jnp.max(jnp.abs(q_norm)) * jnp.max(jnp.abs(k_norm))).reshape(1)
    smem = pl.BlockSpec(memory_space=pltpu.SMEM)
    o = pl.pallas_call(
        functools.partial(_gqa_kernel, has_lat=True),
        grid=(bsz, seq // tq, 1 + seq // tk),
        in_specs=[
            smem,
            pl.BlockSpec((tq, A_Q_W), lambda b, i, j: (b * (seq // tq) + i, 0)),
            pl.BlockSpec((lc, A_KV_W), lambda b, i, j: (ctx_blk(b), 0)),
            pl.BlockSpec((A_KV_HEADS * V_GROUP, lc), lambda b, i, j: (0, ctx_blk(b))),
            pl.BlockSpec((tk, A_KV_W), lambda b, i, j: (b * (seq // tk) + jnp.maximum(j - 1, 0), 0)),
            pl.BlockSpec((A_KV_HEADS * V_GROUP, tk), lambda b, i, j: (0, b * (seq // tk) + jnp.maximum(j - 1, 0))),
        ],
        out_specs=pl.BlockSpec((tq, A_Q_W), lambda b, i, j: (b * (seq // tq) + i, 0)),
        out_shape=jax.ShapeDtypeStruct((n_lat, A_Q_W), BF16),
        scratch_shapes=_gqa_scratch(tq),
        compiler_params=_cparams(("arbitrary", "arbitrary", "arbitrary"), 48),
        name="gqa_latent",
    )(bound, q, k, vt, k, vt)
    o_ctx = pl.pallas_call(
        functools.partial(_gqa_kernel, has_lat=False),
        grid=(bsz, 1, 1),
        in_specs=[
            smem,
            pl.BlockSpec((lc, A_Q_W), lambda b, i, j: (ctx_blk(b), 0)),
            pl.BlockSpec((lc, A_KV_W), lambda b, i, j: (ctx_blk(b), 0)),
            pl.BlockSpec((A_KV_HEADS * V_GROUP, lc), lambda b, i, j: (0, ctx_blk(b))),
        ],
        out_specs=pl.BlockSpec((lc, A_Q_W), lambda b, i, j: (b, 0)),
        out_shape=jax.ShapeDtypeStruct((bsz * lc, A_Q_W), BF16),
        scratch_shapes=_gqa_scratch(lc),
        compiler_params=_cparams(("arbitrary", "arbitrary", "arbitrary"), 32),
        name="gqa_context",
    )(bound, q, k, vt)
    return o, o_ctx


def _dft_a_kernel(y_ref, ca_ref, tc_ref, ts_ref, d_ref, *, nbb):
    ca = ca_ref[...]
    ra = ca.shape[1]
    for j in range(nbb):
        yj = y_ref[:, j, :].astype(BF16)
        p = _dot(ca, yj)
        br = p[:ra, :B_W] - p[ra:, B_W:]
        bi = -p[:ra, B_W:] - p[ra:, :B_W]
        tc = jnp.concatenate([tc_ref[j]] * (B_W // LANES), axis=1)
        ts = jnp.concatenate([ts_ref[j]] * (B_W // LANES), axis=1)
        d_ref[:, 0, j, :] = br * tc + bi * ts
        d_ref[:, 1, j, :] = bi * tc - br * ts


def _dft_b_kernel(d_ref, cb_ref, o_ref, *, kb, scale):
    cb = cb_ref[...]
    for j in range(kb):
        o_ref[:, j, :] = _dot(cb, d_ref[j].astype(BF16)) * scale


def _dft_ctx_kernel(y_ref, cs_ref, o_ref, *, scale):
    y = y_ref[...]
    cs = cs_ref[...]
    lc = cs.shape[0]
    yc = y[:, :B_W].astype(BF16)
    ys = y[:, B_W:].astype(BF16)
    o_ref[...] = (_dot(cs[:, :lc], yc) - _dot(cs[:, lc:], ys)) * scale


def _fourier(y, *, bsz, seq, lc):
    ntok = y.shape[0]
    ra = seq // GRID_W
    nb = GRID_W
    nbb = SUBLANES
    kb = SUBLANES
    assert ntok % nb == 0 and ra % SUBLANES == 0 and (ntok // nb) % 1 == 0
    ca_c, ca_s = _cos_sin(ra)
    ca = _const_bf16(np.concatenate([ca_c, ca_s], axis=0))
    th = 2.0 * np.pi * np.outer(np.arange(nb), np.arange(ra)) / seq
    tc = jnp.asarray(np.broadcast_to(np.cos(th)[:, :, None], (nb, ra, LANES)).astype(np.float32))
    ts = jnp.asarray(np.broadcast_to(np.sin(th)[:, :, None], (nb, ra, LANES)).astype(np.float32))
    cb_c, cb_s = _cos_sin(nb)
    cb = _const_bf16(np.concatenate([cb_c, cb_s], axis=1))

    y3 = y.reshape(ntok // nb, nb, 2 * B_W)
    d = pl.pallas_call(
        functools.partial(_dft_a_kernel, nbb=nbb),
        grid=(bsz, nb // nbb),
        in_specs=[
            pl.BlockSpec((ra, nbb, 2 * B_W), lambda b, i: (b, i, 0)),
            pl.BlockSpec((2 * ra, ra), lambda b, i: (0, 0)),
            pl.BlockSpec((nbb, ra, LANES), lambda b, i: (i, 0, 0)),
            pl.BlockSpec((nbb, ra, LANES), lambda b, i: (i, 0, 0)),
        ],
        out_specs=pl.BlockSpec((None, ra, 2, nbb, B_W), lambda b, i: (b, 0, 0, i, 0)),
        out_shape=jax.ShapeDtypeStruct((bsz, ra, 2, nb, B_W), F32),
        compiler_params=_cparams(("arbitrary", "arbitrary"), 32),
        name="fourier_rows",
    )(y3, ca, tc, ts)
    d = d.reshape(bsz, ra, 2 * nb, B_W)

    scale = 1.0 / math.sqrt(seq * B_GROUP_DIM)
    n_out = bsz * nb
    z3 = pl.pallas_call(
        functools.partial(_dft_b_kernel, kb=kb, scale=scale),
        grid=(bsz, ra // kb),
        in_specs=[
            pl.BlockSpec((None, kb, 2 * nb, B_W), lambda b, i: (b, i, 0, 0)),
            pl.BlockSpec((nb, 2 * nb), lambda b, i: (0, 0)),
        ],
        out_specs=pl.BlockSpec((nb, kb, B_W), lambda b, i: (b, i, 0)),
        out_shape=jax.ShapeDtypeStruct((n_out, ra, B_W), F32),
        compiler_params=_cparams(("arbitrary", "arbitrary"), 32),
        name="fourier_cols",
    )(d, cb)
    z = z3.reshape(bsz * seq, B_W)

    cc, cs_ = _cos_sin(lc)
    csm = _const_bf16(np.concatenate([cc, cs_], axis=1))
    ctx_blk = lambda b: (bsz * seq) // lc + b
    z_ctx = pl.pallas_call(
        functools.partial(_dft_ctx_kernel, scale=1.0 / math.sqrt(lc * B_GROUP_DIM)),
        grid=(bsz,),
        in_specs=[
            pl.BlockSpec((lc, 2 * B_W), lambda b: (ctx_blk(b), 0)),
            pl.BlockSpec((lc, 2 * lc), lambda b: (0, 0)),
        ],
        out_specs=pl.BlockSpec((lc, B_W), lambda b: (b, 0)),
        out_shape=jax.ShapeDtypeStruct((bsz * lc, B_W), F32),
        compiler_params=_cparams(("arbitrary",), 32),
        name="fourier_context",
    )(y, csm)
    return z, z_ctx


def _values_with_ones_t(v, n_heads):
    vt = v.T
    ones = jnp.ones((ONES_ROWS, vt.shape[1]), F32)
    groups = []
    for h in range(n_heads):
        groups += [vt[h * HEAD_DIM:(h + 1) * HEAD_DIM], ones]
    return jnp.concatenate(groups, axis=0)


def _inproj_cd_kernel(x_ref, mod_ref, w_ref, u_ref, q_ref, k_ref, vt_ref):
    x = x_ref[...]
    m = mod_ref[...]
    h = _modulate(x, m[3:4], m[4:5]).astype(BF16)
    y = _dot(h, w_ref[...].astype(BF16))
    c = C_CHANNELS
    u_ref[...] = y[:, :c] * _sigmoid(y[:, c:2 * c])
    q_ref[...] = (y[:, 2 * c:2 * c + D_W] * Q_SCALE_LOG2).astype(BF16)
    k_ref[...] = y[:, 2 * c + D_W:2 * c + 2 * D_W].astype(BF16)
    vt_ref[...] = _values_with_ones_t(y[:, 2 * c + 2 * D_W:], D_HEADS).astype(BF16)


def _inproj_cd(x, mod_l, w_in, w_idx, *, n_tiles, mod_row):
    ntok, d = x.shape
    n_in = w_in.shape[-1]
    row = lambda i: (i, 0)
    return pl.pallas_call(
        _inproj_cd_kernel,
        grid=(n_tiles,),
        in_specs=[
            pl.BlockSpec((TM, d), row),
            pl.BlockSpec((None, N_MOD, d), lambda i: (mod_row(i), 0, 0)),
            pl.BlockSpec((None, d, n_in), lambda i: (w_idx, 0, 0)),
        ],
        out_specs=[
            pl.BlockSpec((TM, C_CHANNELS), row),
            pl.BlockSpec((TM, D_W), row),
            pl.BlockSpec((TM, D_W), row),
            pl.BlockSpec((D_HEADS * V_GROUP, TM), lambda i: (0, i)),
        ],
        out_shape=[
            jax.ShapeDtypeStruct((ntok, C_CHANNELS), F32),
            jax.ShapeDtypeStruct((ntok, D_W), BF16),
            jax.ShapeDtypeStruct((ntok, D_W), BF16),
            jax.ShapeDtypeStruct((D_HEADS * V_GROUP, ntok), BF16),
        ],
        compiler_params=_cparams(("arbitrary",), 40),
        name="in_proj_cd",
    )(x, mod_l, w_in)


def _conv_kernel(prev_ref, cur_ref, next_ref, w_ref, b_ref, lnw_ref, lnb_ref, o_ref, buf_ref, shift_ref, *,
                 tiles_per_seq):
    i = pl.program_id(1)
    halo = CONV_HALO
    tm = cur_ref.shape[0]
    zeros = jnp.zeros((halo, cur_ref.shape[1]), F32)
    buf_ref[0:halo, :] = jnp.where(i > 0, prev_ref[...], zeros)
    buf_ref[halo:halo + tm, :] = cur_ref[...]
    buf_ref[halo + tm:, :] = jnp.where(i < tiles_per_seq - 1, next_ref[...], zeros)
    w = w_ref[...]
    off = halo - C_KERNEL // 2
    y = None
    for r in range(SUBLANES):
        taps = [t for t in range(C_KERNEL) if (t + off) % SUBLANES == r]
        part = None
        for t in taps:
            start = t + off - r
            term = buf_ref[start:start + tm + SUBLANES, :] * w[t:t + 1]
            part = term if part is None else part + term
        if r == 0:
            shifted = part[:tm]
        else:
            shift_ref[...] = part
            shifted = shift_ref[r:r + tm, :]
        y = shifted if y is None else y + shifted
    y = y + b_ref[...]
    mu = jnp.mean(y, axis=-1, keepdims=True)
    yc = y - mu
    var = jnp.mean(yc * yc, axis=-1, keepdims=True)
    z = (yc * lax.rsqrt(var + NORM_EPS)) * lnw_ref[...] + lnb_ref[...]
    o_ref[...] = (z * _sigmoid(z)).astype(BF16)


def _conv_module(u, dw_w, dw_b, ln_w, ln_b, *, bsz, seq):
    c = u.shape[1]
    tiles = seq // TM
    hpt = TM // CONV_HALO
    n_halo_blocks = u.shape[0] // CONV_HALO
    cur = lambda b, i: (b * tiles + i, 0)
    prev = lambda b, i: (jnp.maximum((b * tiles + i) * hpt - 1, 0), 0)
    nxt = lambda b, i: (jnp.minimum((b * tiles + i + 1) * hpt, n_halo_blocks - 1), 0)
    vec = lambda b, i: (0, 0)
    return pl.pallas_call(
        functools.partial(_conv_kernel, tiles_per_seq=tiles),
        grid=(bsz, tiles),
        in_specs=[
            pl.BlockSpec((CONV_HALO, c), prev),
            pl.BlockSpec((TM, c), cur),
            pl.BlockSpec((CONV_HALO, c), nxt),
            pl.BlockSpec((C_KERNEL, c), vec),
            pl.BlockSpec((1, c), vec),
            pl.BlockSpec((1, c), vec),
            pl.BlockSpec((1, c), vec),
        ],
        out_specs=pl.BlockSpec((TM, c), cur),
        out_shape=jax.ShapeDtypeStruct((bsz * seq, c), BF16),
        scratch_shapes=[pltpu.VMEM((TM + 2 * CONV_HALO, c), F32), pltpu.VMEM((TM + SUBLANES, c), F32)],
        compiler_params=_cparams(("arbitrary", "arbitrary"), 32),
        name="conv_module",
    )(u, u, u, dw_w, dw_b.reshape(1, c), ln_w.reshape(1, c), ln_b.reshape(1, c))


def _na_kernel(q_ref, *refs):
    n_kb = NA_QBLOCKS + 2
    near_k, near_vt = refs[:n_kb], refs[n_kb:2 * n_kb]
    kc_ref, vtc_ref = refs[2 * n_kb:2 * n_kb + 2]
    bias_refs = refs[2 * n_kb + 2:2 * n_kb + 2 + NA_QBLOCKS]
    o_ref = refs[-1]
    lane = lax.broadcasted_iota(jnp.int32, (1, LANES), 1)
    hi_half = lane >= HEAD_DIM
    tq = q_ref.shape[0] // NA_QBLOCKS
    n_near = 3
    bodies = [(qb, h, t) for qb in range(NA_QBLOCKS) for h in range(D_HEADS) for t in range(n_near + 1)]
    masked_q = {}

    def scores(qb, h, t):
        sl = slice((h // 2) * LANES, (h // 2 + 1) * LANES)
        if (qb, h) not in masked_q:
            qp = q_ref[qb * tq:(qb + 1) * tq, sl]
            keep = hi_half if h % 2 else jnp.logical_not(hi_half)
            masked_q[qb, h] = jnp.where(keep, qp, jnp.zeros_like(qp))
        k_ref = near_k[qb + t] if t < n_near else kc_ref
        s = _dot_nt(k_ref[:, sl], masked_q[qb, h])
        return s + bias_refs[qb][h, t * tq:(t + 1) * tq, :] if t < n_near else s

    queue = [scores(*bodies[i]) for i in range(NA_LOOKAHEAD)]
    acc, m_run, pending = {}, {}, None

    def fold(key, alpha, pv):
        acc[key] = pv if alpha is None else alpha * acc[key] + pv

    for idx, (qb, h, t) in enumerate(bodies):
        s = queue.pop(0)
        if idx + NA_LOOKAHEAD < len(bodies):
            queue.append(scores(*bodies[idx + NA_LOOKAHEAD]))
        m_tile = jnp.max(s, axis=0, keepdims=True)
        if t == 0:
            m_new, alpha = m_tile, None
        else:
            m_new = jnp.maximum(m_run[qb, h], m_tile)
            alpha = jnp.exp2(m_run[qb, h] - m_new)
        m_run[qb, h] = m_new
        p = jnp.exp2(s - m_new).astype(BF16)
        vt_ref = near_vt[qb + t] if t < n_near else vtc_ref
        pv = _dot(vt_ref[h * V_GROUP:(h + 1) * V_GROUP, :], p)
        if pending is not None:
            fold(*pending)
        pending = ((qb, h), alpha, pv)
    fold(*pending)
    for qb in range(NA_QBLOCKS):
        for pb in range(D_HEADS // 2):
            parts = [acc[qb, h][:HEAD_DIM] / acc[qb, h][HEAD_DIM:HEAD_DIM + 1] for h in (2 * pb, 2 * pb + 1)]
            o_ref[qb * tq:(qb + 1) * tq, pb * LANES:(pb + 1) * LANES] = (
                jnp.concatenate(parts, axis=0).T.astype(BF16))


def _na_bias_kernel(rpb_ref, o_ref, *, rows):
    h = pl.program_id(0)
    n_ro, n_co = 2 * NA_WIN_ROWS - 1, 2 * NA_WIN_COLS - 1
    kc = lax.broadcasted_iota(jnp.int32, (GRID_W, LANES), 0)
    lane = lax.broadcasted_iota(jnp.int32, (GRID_W, LANES), 1)
    qc = lane % GRID_W
    d = kc - qc + (NA_WIN_COLS - 1)
    cs = jnp.clip(qc - NA_WIN_COLS // 2, 0, GRID_W - NA_WIN_COLS)
    col_ok = (kc >= cs) & (kc < cs + NA_WIN_COLS)
    neg = jnp.full((GRID_W, LANES), NEG_BIG, F32)
    log2e = math.log2(math.e)
    band = []
    for ro in range(n_ro):
        acc = neg
        for co in range(n_co):
            acc = jnp.where(d == co, rpb_ref[(h * n_ro + ro) * n_co + co] * log2e, acc)
        band.append(jnp.where(col_ok, acc, neg))
    qr = NA_QROWS
    for kind, r0 in enumerate((0, qr, rows - qr)):
        for b in range(3 * qr):
            rk = r0 - qr + b
            for pair in range(qr // 2):
                halves = []
                for a in (2 * pair, 2 * pair + 1):
                    r = r0 + a
                    start = min(max(r - NA_WIN_ROWS // 2, 0), rows - NA_WIN_ROWS)
                    ok = start <= rk < start + NA_WIN_ROWS and 0 <= rk < rows
                    halves.append(band[rk - r + NA_WIN_ROWS - 1] if ok else neg)
                o_ref[kind, b * GRID_W:(b + 1) * GRID_W, pair * LANES:(pair + 1) * LANES] = jnp.where(
                    lane < GRID_W, halves[0], halves[1])


def _na_bias(rpb, rows):
    heads = rpb.shape[0]
    tq = NA_QROWS * GRID_W
    return pl.pallas_call(
        functools.partial(_na_bias_kernel, rows=rows),
        grid=(heads,),
        in_specs=[pl.BlockSpec(memory_space=pltpu.SMEM)],
        out_specs=pl.BlockSpec((3, None, 3 * tq, tq), lambda h: (0, h, 0, 0)),
        out_shape=jax.ShapeDtypeStruct((3, heads, 3 * tq, tq), F32),
        compiler_params=_cparams(("arbitrary",), 32),
        name="na_bias",
    )(rpb.reshape(-1))


def _na(q, k, vt, rpb, *, bsz, seq, lc):
    rows = seq // GRID_W
    tq = NA_QROWS * GRID_W
    nblk = seq // tq
    assert rows >= NA_WIN_ROWS and rows % NA_QROWS == 0 and NA_WIN_ROWS == 2 * NA_QROWS and nblk >= 3
    assert lc == tq and nblk % NA_QBLOCKS == 0
    bias = _na_bias(rpb, rows)
    n_lat = bsz * seq
    steps = nblk // NA_QBLOCKS
    offsets = range(-1, NA_QBLOCKS + 1)
    qmap = lambda b, i: (b * steps + i, 0)

    def near(off, transposed):
        def index(b, i):
            blk = b * nblk + jnp.clip(NA_QBLOCKS * i + off, 0, nblk - 1)
            return (0, blk) if transposed else (blk, 0)
        return index

    def kind(qb):
        def index(b, i):
            j = NA_QBLOCKS * i + qb
            return (jnp.where(j == 0, 0, jnp.where(j == nblk - 1, 2, 1)), 0, 0, 0)
        return index

    q_blk = pl.BlockSpec((NA_QBLOCKS * tq, D_W), qmap)
    rows_blk = lambda m: pl.BlockSpec((tq, D_W), m)
    cols_blk = lambda m: pl.BlockSpec((D_HEADS * V_GROUP, tq), m)
    return pl.pallas_call(
        _na_kernel,
        grid=(bsz, steps),
        in_specs=(
            [q_blk]
            + [rows_blk(near(off, False)) for off in offsets]
            + [cols_blk(near(off, True)) for off in offsets]
            + [rows_blk(lambda b, i: (n_lat // lc + b, 0)), cols_blk(lambda b, i: (0, n_lat // lc + b))]
            + [pl.BlockSpec((None, D_HEADS, 3 * tq, tq), kind(qb), pipeline_mode=pl.Buffered(1))
               for qb in range(NA_QBLOCKS)]
        ),
        out_specs=q_blk,
        out_shape=jax.ShapeDtypeStruct((n_lat, D_W), BF16),
        compiler_params=_cparams(("arbitrary", "arbitrary"), 56),
        name="neighbourhood_attention",
    )(q, *([k] * len(offsets)), *([vt] * len(offsets)), k, vt, *([bias] * NA_QBLOCKS))


def kernel(x, c, ctx, c_ctx, w_mod, b_mod, ffn_w_gate, ffn_w_up, ffn_w_down, ab_w_in, ab_w_out, ab_q_norm,
           ab_k_norm, cd_w_in, cd_w_out, cd_dw_w, cd_dw_b, cd_ln_w, cd_ln_b, cd_rpb, final_norm):
    bsz, seq, d = x.shape
    lc = ctx.shape[1]
    depth = w_mod.shape[0]
    assert depth == 2, "layer 0 (A/B mixer) updates the context, layer 1 (C/D mixer) is the last layer"
    assert bsz * lc == TM and seq % TM == 0 and seq % ATT_TQ == 0 and seq % ATT_TK == 0
    n_lat_tiles = bsz * seq // TM
    n_all_tiles = n_lat_tiles + 1
    tiles_per_seq = seq // TM
    mod_row = lambda i: jnp.where(i == n_lat_tiles, bsz, i // tiles_per_seq)
    rope_row = lambda i: jnp.where(i == n_lat_tiles, tiles_per_seq, i % tiles_per_seq)

    mod = _mod_params(c, c_ctx, w_mod, b_mod)
    cos_t, sin_t = _rope_tables(seq)

    ffn_w = (ffn_w_gate, ffn_w_up, ffn_w_down)

    xt = x.reshape(bsz * seq, d)
    for layer in range(depth):
        last = layer == depth - 1
        i = layer // 2
        mod_l = mod[layer]
        xt = _ffn(xt, mod_l, ffn_w, layer, 0, base=0, n_tiles=n_all_tiles, mod_row=mod_row,
                  x_ctx=ctx.reshape(bsz * lc, d) if layer == 0 else None)
        if layer % 2 == 0:
            q, k, vt, y = _inproj_ab(xt, mod_l, ab_w_in, i, ab_q_norm[i], ab_k_norm[i], cos_t, sin_t,
                                     n_tiles=n_all_tiles, mod_row=mod_row, rope_row=rope_row)
            a1, a1_ctx = _gqa(q, k, vt, ab_q_norm[i], ab_k_norm[i], bsz=bsz, seq=seq, lc=lc)
            a2, a2_ctx = _fourier(y, bsz=bsz, seq=seq, lc=lc)
            mixer = (a1, a2, ab_w_out, i, a1_ctx, a2_ctx)
        else:
            u, q, k, vt = _inproj_cd(xt, mod_l, cd_w_in, i, n_tiles=n_all_tiles, mod_row=mod_row)
            a1 = _conv_module(u, cd_dw_w[i], cd_dw_b[i], cd_ln_w[i], cd_ln_b[i], bsz=bsz, seq=seq)
            a2 = _na(q, k, vt, cd_rpb[i], bsz=bsz, seq=seq, lc=lc)
            mixer = (a1, a2, cd_w_out, i, None, None)
        assert last == (mixer[4] is None)
        xt = _ffn(xt, mod_l, ffn_w, layer, 1, base=6, n_tiles=n_lat_tiles if last else n_all_tiles, mod_row=mod_row,
                  final_norm=final_norm if last else None, mixer=mixer)
    return xt.reshape(bsz, seq, d)
```
